```python
import math
import jax, jax.numpy as jnp
from jax import lax
import numpy as np

D_MODEL = 1024
BATCH = 8
SEQ = 4096
DEPTH = 4

N_MIXERS = 3
HEAD_DIM = 64
N_HEADS = D_MODEL // HEAD_DIM
MIX_WIDTH = N_HEADS * HEAD_DIM
ROT_DIM = HEAD_DIM // 4
ROPE_THETA = 500000.0
Q_BLOCK = 128
IDX_HEADS = 8
IDX_DIM = 64
IDX_ROT = IDX_DIM // 4
TOPK_MAX = 256
DSA_BLOCK = 32
FOX_HEADS = N_HEADS
MLA_HEADS = N_HEADS
MLA_NOPE = 64
MLA_ROPE = 32
MLA_V = 64
Q_LORA = 384
KV_LORA = 256
D_FF = 2816
ALPHA = (2.0 * DEPTH) ** 0.25
BETA = (8.0 * DEPTH) ** -0.25
LN_EPS = 1e-5
RMS_EPS = 1e-6
N_DSA = (DEPTH + 2) // 3
N_FOX = (DEPTH + 1) // 3
N_MLA = DEPTH // 3
DSA_IN = 3 * MIX_WIDTH + IDX_HEADS * IDX_DIM + IDX_DIM + IDX_HEADS
FOX_IN = 3 * MIX_WIDTH + FOX_HEADS
MLA_IN = Q_LORA + KV_LORA + MLA_ROPE

kernel_name = "hybrid_dsa_fox_mla_macaron_deepnorm"


def layer_norm(x, g, b):
    xf = x.astype(jnp.float32)
    mu = jnp.mean(xf, axis=-1, keepdims=True)
    var = jnp.mean(jnp.square(xf - mu), axis=-1, keepdims=True)
    return ((xf - mu) * lax.rsqrt(var + LN_EPS) * g.astype(jnp.float32) + b.astype(jnp.float32)).astype(x.dtype)


def rms_norm(x, g):
    xf = x.astype(jnp.float32)
    ms = jnp.mean(jnp.square(xf), axis=-1, keepdims=True)
    return (xf * lax.rsqrt(ms + RMS_EPS) * g.astype(jnp.float32)).astype(x.dtype)


def rope_tables(seq_len, dim):
    inv = ROPE_THETA ** (-jnp.arange(0, dim, 2, dtype=jnp.float32) / dim)
    ang = jnp.arange(seq_len, dtype=jnp.float32)[:, None] * inv[None, :]
    return jnp.cos(ang), jnp.sin(ang)


def apply_rope(x, cos, sin):
    half = x.shape[-1] // 2
    x1, x2 = x[..., :half], x[..., half:]
    c = cos[None, :, None, :].astype(x.dtype)
    s = sin[None, :, None, :].astype(x.dtype)
    return jnp.concatenate([x1 * c - x2 * s, x1 * s + x2 * c], axis=-1)


def partial_rope(x, cos, sin, rot):
    return jnp.concatenate([apply_rope(x[..., :rot], cos, sin), x[..., rot:]], axis=-1)


def swiglu(x, w13, w2):
    gate, up = jnp.split(x @ w13, 2, axis=-1)
    return (jax.nn.silu(gate) * up) @ w2


def causal_block_attention(q, k, v, scale, cum=None):
    S = q.shape[1]
    outs = []
    for i in range(S // Q_BLOCK):
        lo, hi = i * Q_BLOCK, (i + 1) * Q_BLOCK
        s = jnp.einsum('bqhd,bkhd->bhqk', q[:, lo:hi], k[:, :hi]).astype(jnp.float32) * scale
        if cum is not None:
            s = s + cum[:, :, lo:hi, None] - cum[:, :, None, :hi]
        mask = jnp.arange(lo, hi)[:, None] >= jnp.arange(hi)[None, :]
        p = jax.nn.softmax(jnp.where(mask, s, -jnp.inf), axis=-1).astype(v.dtype)
        outs.append(jnp.einsum('bhqk,bkhd->bqhd', p, v[:, :hi]))
    return jnp.concatenate(outs, axis=1)


def dsa_mixer(x, w_in, w_out, cos_p, sin_p):
    B, S, _ = x.shape
    d = MIX_WIDTH
    di = IDX_HEADS * IDX_DIM
    q, k, v, qi, ki, wi = jnp.split(x @ w_in, [d, 2 * d, 3 * d, 3 * d + di, 3 * d + di + IDX_DIM], axis=-1)
    q = partial_rope(q.reshape(B, S, N_HEADS, HEAD_DIM), cos_p, sin_p, ROT_DIM)
    k = partial_rope(k.reshape(B, S, N_HEADS, HEAD_DIM), cos_p, sin_p, ROT_DIM)
    v = v.reshape(B, S, N_HEADS, HEAD_DIM)
    qi = partial_rope(qi.reshape(B, S, IDX_HEADS, IDX_DIM), cos_p, sin_p, IDX_ROT)
    ki = partial_rope(ki.reshape(B, S, 1, IDX_DIM), cos_p, sin_p, IDX_ROT)[:, :, 0]
    wi = wi.astype(jnp.float32) * (IDX_HEADS ** -0.5)
    topk = min(TOPK_MAX, S // 4)
    scale = HEAD_DIM ** -0.5
    key_pos = jnp.arange(S)

    def block(i):
        lo = i * DSA_BLOCK
        t = lo + jnp.arange(DSA_BLOCK)
        qb = lax.dynamic_slice_in_dim(q, lo, DSA_BLOCK, axis=1)
        qib = lax.dynamic_slice_in_dim(qi, lo, DSA_BLOCK, axis=1)
        wib = lax.dynamic_slice_in_dim(wi, lo, DSA_BLOCK, axis=1)
        dots = jnp.einsum('bqhd,bkd->bqhk', qib, ki).astype(jnp.float32) * (IDX_DIM ** -0.5)
        idx_score = jnp.einsum('bqh,bqhk->bqk', wib, jax.nn.relu(dots))
        idx_score = jnp.where(t[:, None] >= key_pos[None, :], idx_score, -jnp.inf)
        _, sel = lax.top_k(idx_score, topk)
        ks = jax.vmap(lambda kk, ii: kk[ii])(k, sel)
        vs = jax.vmap(lambda vv, ii: vv[ii])(v, sel)
        logits = jnp.einsum('bqhd,bqkhd->bhqk', qb, ks).astype(jnp.float32) * scale
        valid = sel <= t[None, :, None]
        p = jax.nn.softmax(jnp.where(valid[:, None], logits, -jnp.inf), axis=-1).astype(v.dtype)
        return jnp.einsum('bhqk,bqkhd->bqhd', p, vs)

    out = lax.map(block, jnp.arange(S // DSA_BLOCK))
    out = jnp.transpose(out, (1, 0, 2, 3, 4)).reshape(B, S, d)
    return out @ w_out


def fox_mixer(x, w_in, b_f, w_out):
    B, S, _ = x.shape
    d = MIX_WIDTH
    q, k, v, f = jnp.split(x @ w_in, [d, 2 * d, 3 * d], axis=-1)
    q = q.reshape(B, S, FOX_HEADS, HEAD_DIM)
    k = k.reshape(B, S, FOX_HEADS, HEAD_DIM)
    v = v.reshape(B, S, FOX_HEADS, HEAD_DIM)
    log_f = jax.nn.log_sigmoid(f.astype(jnp.float32) + b_f.astype(jnp.float32))
    cum = jnp.transpose(jnp.cumsum(log_f, axis=1), (0, 2, 1))
    o = causal_block_attention(q, k, v, HEAD_DIM ** -0.5, cum)
    return o.reshape(B, S, d) @ w_out


def mla_mixer(x, w_dqkv, q_norm_g, w_uq, kv_norm_g, w_ukv, w_out, cos_m, sin_m):
    B, S, _ = x.shape
    cq, ckv, k_rope = jnp.split(x @ w_dqkv, [Q_LORA, Q_LORA + KV_LORA], axis=-1)
    q = (rms_norm(cq, q_norm_g) @ w_uq).reshape(B, S, MLA_HEADS, MLA_NOPE + MLA_ROPE)
    q_nope, q_rope = q[..., :MLA_NOPE], apply_rope(q[..., MLA_NOPE:], cos_m, sin_m)
    kv = (rms_norm(ckv, kv_norm_g) @ w_ukv).reshape(B, S, MLA_HEADS, MLA_NOPE + MLA_V)
    k_nope, v = kv[..., :MLA_NOPE], kv[..., MLA_NOPE:]
    k_rope = apply_rope(k_rope[:, :, None, :], cos_m, sin_m)
    q = jnp.concatenate([q_nope, q_rope], axis=-1)
    k = jnp.concatenate([k_nope, jnp.broadcast_to(k_rope, (B, S, MLA_HEADS, MLA_ROPE))], axis=-1)
    o = causal_block_attention(q, k, v, (MLA_NOPE + MLA_ROPE) ** -0.5)
    return o.reshape(B, S, MLA_HEADS * MLA_V) @ w_out


def setup_inputs(seed: int = 0) -> dict:
    key = jax.random.key(seed)
    ks = jax.random.split(key, 20)
    nrm = lambda k, shape, fan_in, s=1.0: jax.random.normal(k, shape, jnp.float32) * (fan_in ** -0.5) * s
    x = jax.random.normal(ks[0], (BATCH, SEQ, D_MODEL), jnp.float32)
    ffn1_w13 = nrm(ks[1], (DEPTH, D_MODEL, 2 * D_FF), D_MODEL)
    ffn1_w2 = nrm(ks[2], (DEPTH, D_FF, D_MODEL), D_FF, BETA)
    ffn2_w13 = nrm(ks[3], (DEPTH, D_MODEL, 2 * D_FF), D_MODEL)
    ffn2_w2 = nrm(ks[4], (DEPTH, D_FF, D_MODEL), D_FF, BETA)
    ln_g = 1.0 + 0.02 * jax.random.normal(ks[5], (DEPTH, 3, D_MODEL), jnp.float32)
    ln_b = 0.02 * jax.random.normal(ks[6], (DEPTH, 3, D_MODEL), jnp.float32)
    w_out = nrm(ks[7], (DEPTH, MIX_WIDTH, D_MODEL), MIX_WIDTH, BETA)
    dsa_w_in = nrm(ks[8], (N_DSA, D_MODEL, DSA_IN), D_MODEL)
    fox_w_in = nrm(ks[9], (N_FOX, D_MODEL, FOX_IN), D_MODEL)
    fox_b_f = (jnp.broadcast_to(jnp.linspace(1.0, 6.0, FOX_HEADS, dtype=jnp.float32), (N_FOX, FOX_HEADS))
               + 0.1 * jax.random.normal(ks[10], (N_FOX, FOX_HEADS), jnp.float32))
    mla_w_dqkv = nrm(ks[11], (N_MLA, D_MODEL, MLA_IN), D_MODEL)
    mla_q_norm_g = 1.0 + 0.02 * jax.random.normal(ks[12], (N_MLA, Q_LORA), jnp.float32)
    mla_w_uq = nrm(ks[13], (N_MLA, Q_LORA, MLA_HEADS * (MLA_NOPE + MLA_ROPE)), Q_LORA)
    mla_kv_norm_g = 1.0 + 0.02 * jax.random.normal(ks[14], (N_MLA, KV_LORA), jnp.float32)
    mla_w_ukv = nrm(ks[15], (N_MLA, KV_LORA, MLA_HEADS * (MLA_NOPE + MLA_V)), KV_LORA)
    return {"x": x, "ffn1_w13": ffn1_w13, "ffn1_w2": ffn1_w2, "ffn2_w13": ffn2_w13, "ffn2_w2": ffn2_w2,
            "ln_g": ln_g, "ln_b": ln_b, "w_out": w_out, "dsa_w_in": dsa_w_in, "fox_w_in": fox_w_in,
            "fox_b_f": fox_b_f, "mla_w_dqkv": mla_w_dqkv, "mla_q_norm_g": mla_q_norm_g,
            "mla_w_uq": mla_w_uq, "mla_kv_norm_g": mla_kv_norm_g, "mla_w_ukv": mla_w_ukv}


def reference(x, ffn1_w13, ffn1_w2, ffn2_w13, ffn2_w2, ln_g, ln_b, w_out, dsa_w_in, fox_w_in,
              fox_b_f, mla_w_dqkv, mla_q_norm_g, mla_w_uq, mla_kv_norm_g, mla_w_ukv):
    S = x.shape[1]
    cos_p, sin_p = rope_tables(S, ROT_DIM)
    cos_m, sin_m = rope_tables(S, MLA_ROPE)
    for i in range(DEPTH):
        x = layer_norm(ALPHA * x + 0.5 * swiglu(x, ffn1_w13[i], ffn1_w2[i]), ln_g[i, 0], ln_b[i, 0])
        kind, j = i % N_MIXERS, i // N_MIXERS
        if kind == 0:
            y = dsa_mixer(x, dsa_w_in[j], w_out[i], cos_p, sin_p)
        elif kind == 1:
            y = fox_mixer(x, fox_w_in[j], fox_b_f[j], w_out[i])
        else:
            y = mla_mixer(x, mla_w_dqkv[j], mla_q_norm_g[j], mla_w_uq[j], mla_kv_norm_g[j],
                          mla_w_ukv[j], w_out[i], cos_m, sin_m)
        x = layer_norm(ALPHA * x + y, ln_g[i, 1], ln_b[i, 1])
        x = layer_norm(ALPHA * x + 0.5 * swiglu(x, ffn2_w13[i], ffn2_w2[i]), ln_g[i, 2], ln_b[i, 2])
    return x
```

```python
import functools

import jax
import jax.numpy as jnp
from jax import lax
from jax.experimental import pallas as pl
from jax.experimental.pallas import tpu as pltpu

F32 = jnp.float32
BF16 = jnp.bfloat16
I32 = jnp.int32

D_MODEL = 1024
DEPTH = 4
N_MIXERS = 3
HEAD_DIM = 64
N_HEADS = D_MODEL // HEAD_DIM
MIX_WIDTH = N_HEADS * HEAD_DIM
ROT_DIM = HEAD_DIM // 4
ROPE_THETA = 500000.0
IDX_HEADS = 8
IDX_DIM = 64
TOPK_MAX = 256
MLA_NOPE = 64
MLA_ROPE = 32
MLA_V = 64
Q_LORA = 384
KV_LORA = 256
D_FF = 2816
ALPHA = (2.0 * DEPTH) ** 0.25
LN_EPS = 1e-5
RMS_EPS = 1e-6

LANES = 128
N_SLABS = MIX_WIDTH // LANES
LOG2E = 1.4426950408889634
NEG = -1e30
INT_MIN = -2147483648

TM = 512
TF = 256
TQ = 256
TK = 512
VMEM_LIMIT = 56 * 1024 * 1024


def _params(n_axes, vmem=VMEM_LIMIT):
    return pltpu.CompilerParams(dimension_semantics=("arbitrary",) * n_axes, vmem_limit_bytes=vmem)


def _const_spec(shape):
    nd = len(shape)
    return pl.BlockSpec(shape, lambda *_: (0,) * nd)


def _dot(a, b):
    return jnp.dot(a, b, preferred_element_type=F32)


def _dot_nt(a, b):
    return lax.dot_general(a, b, (((1,), (1,)), ((), ())), preferred_element_type=F32)


def _layer_norm(z, g, b):
    mu = jnp.mean(z, axis=-1, keepdims=True)
    d = z - mu
    var = jnp.mean(d * d, axis=-1, keepdims=True)
    return d * lax.rsqrt(var + LN_EPS) * g + b


def _rms_norm(z, g):
    ms = jnp.mean(z * z, axis=-1, keepdims=True)
    return z * lax.rsqrt(ms + RMS_EPS) * g


def _rope_slab(y, cos, sin_lo, sin_hi, half):
    return y * cos + pltpu.roll(y, half, 1) * sin_hi + pltpu.roll(y, LANES - half, 1) * sin_lo


def _ffn_ln_kernel(x_ref, wgu_ref, w2_ref, g_ref, b_ref, o_ref):
    x = x_ref[...]
    xb = x.astype(BF16)
    n_chunks, _, two_tf = wgu_ref.shape
    tf = two_tf // 2
    acc = jnp.zeros(x.shape, F32)
    for c in range(n_chunks):
        gu = _dot(xb, wgu_ref[c])
        gate, up = gu[:, :tf], gu[:, tf:]
        h = gate * jax.nn.sigmoid(gate) * up
        acc = acc + _dot(h.astype(BF16), w2_ref[c])
    o_ref[...] = _layer_norm(ALPHA * x + 0.5 * acc, g_ref[...], b_ref[...])


def _ffn_ln(x2, wgu, w2, g, b):
    n, d = x2.shape
    return pl.pallas_call(
        _ffn_ln_kernel,
        grid=(n // TM,),
        in_specs=[pl.BlockSpec((TM, d), lambda i: (i, 0)),
                  _const_spec(wgu.shape), _const_spec(w2.shape),
                  _const_spec(g.shape), _const_spec(b.shape)],
        out_specs=pl.BlockSpec((TM, d), lambda i: (i, 0)),
        out_shape=jax.ShapeDtypeStruct((n, d), F32),
        compiler_params=_params(1),
        name="ffn_ln",
    )(x2, wgu, w2, g, b)


def _prep_ffn(w13, w2):
    d, two_ff = w13.shape
    ff = two_ff // 2
    nc = ff // TF
    gate = w13[:, :ff].reshape(d, nc, 1, TF)
    up = w13[:, ff:].reshape(d, nc, 1, TF)
    wgu = jnp.concatenate([gate, up], axis=2).transpose(1, 0, 2, 3).reshape(nc, d, 2 * TF)
    return wgu.astype(BF16), w2.reshape(nc, TF, d).astype(BF16)


def _outproj_ln_kernel(x_ref, o_ref, w_ref, g_ref, b_ref, out_ref):
    y = _dot(o_ref[...], w_ref[...])
    out_ref[...] = _layer_norm(ALPHA * x_ref[...] + y, g_ref[...], b_ref[...])


def _outproj_ln(x2, o2, w, g, b):
    n, d = x2.shape
    return pl.pallas_call(
        _outproj_ln_kernel,
        grid=(n // TM,),
        in_specs=[pl.BlockSpec((TM, d), lambda i: (i, 0)),
                  pl.BlockSpec((TM, o2.shape[1]), lambda i: (i, 0)),
                  _const_spec(w.shape), _const_spec(g.shape), _const_spec(b.shape)],
        out_specs=pl.BlockSpec((TM, d), lambda i: (i, 0)),
        out_shape=jax.ShapeDtypeStruct((n, d), F32),
        compiler_params=_params(1),
        name="outproj_ln",
    )(x2, o2, w, g, b)


def _rope_tables(seq, rot, lane_of_rot0, period):
    half = rot // 2
    inv = ROPE_THETA ** (-jnp.arange(0, rot, 2, dtype=F32) / rot)
    ang = jnp.arange(seq, dtype=F32)[:, None] * inv[None, :]
    cos_h, sin_h = jnp.cos(ang), jnp.sin(ang)
    r = (jnp.arange(LANES) % period) - lane_of_rot0
    in_lo = (r >= 0) & (r < half)
    in_hi = (r >= half) & (r < rot)
    idx = jnp.clip(jnp.where(in_hi, r - half, r), 0, half - 1)
    cos = jnp.where((in_lo | in_hi)[None, :], cos_h[:, idx], 1.0)
    sin_lo = jnp.where(in_lo[None, :], -sin_h[:, idx], 0.0)
    sin_hi = jnp.where(in_hi[None, :], sin_h[:, idx], 0.0)
    return cos, sin_lo, sin_hi


def _dsa_proj_kernel(x_ref, w_ref, cos_ref, slo_ref, shi_ref,
                     q_ref, k_ref, v_ref, qi_ref, ki_ref, wi_ref):
    xb = x_ref[...].astype(BF16)
    cos, slo, shi = cos_ref[...], slo_ref[...], shi_ref[...]
    half = ROT_DIM // 2
    d = MIX_WIDTH
    di = IDX_HEADS * IDX_DIM

    def roped(col0, width, scale, out_ref):
        y = _dot(xb, w_ref[:, col0:col0 + width])
        for j in range(width // LANES):
            s = _rope_slab(y[:, j * LANES:(j + 1) * LANES], cos, slo, shi, half)
            if scale != 1.0:
                s = s * scale
            out_ref[:, j * LANES:(j + 1) * LANES] = s.astype(out_ref.dtype)

    roped(0, d, HEAD_DIM ** -0.5 * LOG2E, q_ref)
    roped(d, d, 1.0, k_ref)
    v_ref[...] = _dot(xb, w_ref[:, 2 * d:3 * d]).astype(BF16)
    roped(3 * d, di, IDX_DIM ** -0.5, qi_ref)
    roped(3 * d + di, LANES, 1.0, ki_ref)
    wi_ref[...] = _dot(xb, w_ref[:, 3 * d + di + LANES:]) * (IDX_HEADS ** -0.5)


def _dsa_proj(x2, w, tabs, seq):
    n, dm = x2.shape
    d, di = MIX_WIDTH, IDX_HEADS * IDX_DIM
    tiles_per_seq = seq // TM
    row = lambda width: pl.BlockSpec((TM, width), lambda i: (i, 0))
    tab = pl.BlockSpec((TM, LANES), lambda i: (i % tiles_per_seq, 0))
    return pl.pallas_call(
        _dsa_proj_kernel,
        grid=(n // TM,),
        in_specs=[row(dm), _const_spec(w.shape), tab, tab, tab],
        out_specs=[row(d), row(d), row(d), row(di), row(LANES), row(LANES)],
        out_shape=[jax.ShapeDtypeStruct((n, d), BF16), jax.ShapeDtypeStruct((n, d), BF16),
                   jax.ShapeDtypeStruct((n, d), BF16), jax.ShapeDtypeStruct((n, di), BF16),
                   jax.ShapeDtypeStruct((n, LANES), BF16), jax.ShapeDtypeStruct((n, LANES), F32)],
        compiler_params=_params(1),
        name="dsa_proj",
    )(x2, w, *tabs)


def _prep_dsa_w(w_in):
    d, di = MIX_WIDTH, IDX_HEADS * IDX_DIM
    ki = w_in[:, 3 * d + di:3 * d + di + IDX_DIM]
    wi = w_in[:, 3 * d + di + IDX_DIM:]
    pad = jnp.zeros((w_in.shape[0], LANES - IDX_HEADS), w_in.dtype)
    return jnp.concatenate([w_in[:, :3 * d + di], ki, ki, wi, pad], axis=1).astype(BF16)


def _fox_proj_kernel(x_ref, w_ref, wf_ref, bf_ref, q_ref, k_ref, v_ref, cum_ref, carry_ref,
                     *, tiles_per_seq):
    i = pl.program_id(0)
    xb = x_ref[...].astype(BF16)
    d = MIX_WIDTH
    q_ref[...] = (_dot(xb, w_ref[:, :d]) * (HEAD_DIM ** -0.5 * LOG2E)).astype(BF16)
    k_ref[...] = _dot(xb, w_ref[:, d:2 * d]).astype(BF16)
    v_ref[...] = _dot(xb, w_ref[:, 2 * d:]).astype(BF16)

    f = _dot(xb, wf_ref[...]) + bf_ref[...]
    log_f = (jnp.minimum(f, 0.0) - jnp.log1p(jnp.exp(-jnp.abs(f)))) * LOG2E
    tm = log_f.shape[0]
    tri = (lax.broadcasted_iota(I32, (tm, tm), 0) >= lax.broadcasted_iota(I32, (tm, tm), 1))
    tri = jnp.where(tri, 1.0, 0.0).astype(BF16)
    p0 = log_f.astype(BF16)
    r1 = log_f - p0.astype(F32)
    p1 = r1.astype(BF16)
    p2 = (r1 - p1.astype(F32)).astype(BF16)
    local = _dot(tri, p0) + _dot(tri, p1) + _dot(tri, p2)

    @pl.when(i % tiles_per_seq == 0)
    def _():
        carry_ref[...] = jnp.zeros_like(carry_ref)

    cum = local + carry_ref[...]
    cum_ref[...] = cum
    carry_ref[...] = cum[tm - 1:tm, :]


def _fox_proj(x2, w, wf, bf, seq):
    n, dm = x2.shape
    d = MIX_WIDTH
    row = lambda width: pl.BlockSpec((TM, width), lambda i: (i, 0))
    return pl.pallas_call(
        functools.partial(_fox_proj_kernel, tiles_per_seq=seq // TM),
        grid=(n // TM,),
        in_specs=[row(dm), _const_spec(w.shape), _const_spec(wf.shape), _const_spec(bf.shape)],
        out_specs=[row(d), row(d), row(d), row(LANES)],
        out_shape=[jax.ShapeDtypeStruct((n, d), BF16)] * 3 + [jax.ShapeDtypeStruct((n, LANES), F32)],
        scratch_shapes=[pltpu.VMEM((1, LANES), F32)],
        compiler_params=_params(1),
        name="fox_proj",
    )(x2, w, wf, bf)


def _mla_proj_kernel(x_ref, wd_ref, gq_ref, gkv_ref, wuq_ref, wuk_ref, wuv_ref,
                     cos_ref, slo_ref, shi_ref, q_ref, k_ref, v_ref):
    xb = x_ref[...].astype(BF16)
    cos, slo, shi = cos_ref[...], slo_ref[...], shi_ref[...]
    half = MLA_ROPE // 2
    c = _dot(xb, wd_ref[...])
    cq = _rms_norm(c[:, :Q_LORA], gq_ref[...]).astype(BF16)
    ckv = _rms_norm(c[:, Q_LORA:Q_LORA + KV_LORA], gkv_ref[...]).astype(BF16)
    k_rope = _rope_slab(c[:, Q_LORA + KV_LORA:], cos, slo, shi, half)
    scale = (MLA_NOPE + MLA_ROPE) ** -0.5 * LOG2E
    for h in range(N_HEADS):
        sl = slice(h * LANES, (h + 1) * LANES)
        q = _rope_slab(_dot(cq, wuq_ref[:, sl]), cos, slo, shi, half)
        q_ref[:, sl] = (q * scale).astype(BF16)
        k_ref[:, sl] = (_dot(ckv, wuk_ref[:, sl]) + k_rope).astype(BF16)
    v_ref[...] = _dot(ckv, wuv_ref[...]).astype(BF16)


def _mla_proj(x2, wd, gq, gkv, wuq, wuk, wuv, tabs, seq):
    n, dm = x2.shape
    tiles_per_seq = seq // TM
    row = lambda width: pl.BlockSpec((TM, width), lambda i: (i, 0))
    tab = pl.BlockSpec((TM, LANES), lambda i: (i % tiles_per_seq, 0))
    wide = N_HEADS * LANES
    return pl.pallas_call(
        _mla_proj_kernel,
        grid=(n // TM,),
        in_specs=[row(dm)] + [_const_spec(a.shape) for a in (wd, gq, gkv, wuq, wuk, wuv)] + [tab] * 3,
        out_specs=[row(wide), row(wide), row(MIX_WIDTH)],
        out_shape=[jax.ShapeDtypeStruct((n, wide), BF16), jax.ShapeDtypeStruct((n, wide), BF16),
                   jax.ShapeDtypeStruct((n, MIX_WIDTH), BF16)],
        compiler_params=_params(1),
        name="mla_proj",
    )(x2, wd, gq, gkv, wuq, wuk, wuv, *tabs)


def _prep_mla_w(w_dqkv, w_uq, w_ukv):
    dm = w_dqkv.shape[0]
    zeros = lambda r, c: jnp.zeros((r, c), w_dqkv.dtype)
    k_rope_w = w_dqkv[:, Q_LORA + KV_LORA:]
    wd = jnp.concatenate([w_dqkv[:, :Q_LORA + KV_LORA], zeros(dm, MLA_NOPE), k_rope_w,
                          zeros(dm, LANES - MLA_NOPE - MLA_ROPE)], axis=1)
    uq = w_uq.reshape(Q_LORA, N_HEADS, MLA_NOPE + MLA_ROPE)
    uq = jnp.pad(uq, ((0, 0), (0, 0), (0, LANES - MLA_NOPE - MLA_ROPE))).reshape(Q_LORA, N_HEADS * LANES)
    ukv = w_ukv.reshape(KV_LORA, N_HEADS, MLA_NOPE + MLA_V)
    uk = jnp.pad(ukv[:, :, :MLA_NOPE], ((0, 0), (0, 0), (0, LANES - MLA_NOPE))).reshape(KV_LORA, N_HEADS * LANES)
    uv = ukv[:, :, MLA_NOPE:].reshape(KV_LORA, N_HEADS * MLA_V)
    return wd.astype(BF16), uq.astype(BF16), uk.astype(BF16), uv.astype(BF16)


def _softmax_step(state, q, k, v, bias, mask):
    m, l, acc = state
    s = _dot_nt(q, k)
    if bias is not None:
        s = s + bias
    if mask is not None:
        s = jnp.where(mask, s, NEG)
    m_new = jnp.maximum(m, jnp.max(s, axis=1, keepdims=True))
    alpha = jnp.exp2(m - m_new)
    p = jnp.exp2(s - m_new)
    l = alpha * l + jnp.sum(p, axis=1, keepdims=True)
    acc = alpha * acc + _dot(p.astype(BF16), v)
    return m_new, l, acc


def _attend_pair(qs, k_loads, v_load, bias_fns, n_plain, masked_chunk, row0, tq, tk):
    init = tuple((jnp.full((tq, 1), NEG, F32), jnp.zeros((tq, 1), F32), jnp.zeros((tq, LANES), F32))
                 for _ in range(2))

    def step(c, carry, mask):
        v = v_load(c)
        return tuple(_softmax_step(carry[x], qs[x], k_loads[x](c), v,
                                   None if bias_fns[x] is None else bias_fns[x](c), mask)
                     for x in range(2))

    carry = lax.fori_loop(0, n_plain, lambda c, cr: step(c, cr, None), init)
    if masked_chunk is not None:
        rows = row0 + lax.broadcasted_iota(I32, (tq, tk), 0)
        cols = masked_chunk * tk + lax.broadcasted_iota(I32, (tq, tk), 1)
        carry = step(masked_chunk, carry, cols <= rows)
    (_, la, acca), (_, lb, accb) = carry
    lane = lax.broadcasted_iota(I32, (tq, LANES), 1)
    return jnp.where(lane < HEAD_DIM, acca / la, accb / lb)


def _split_heads(q_slab):
    lane = lax.broadcasted_iota(I32, q_slab.shape, 1)
    zero = jnp.zeros_like(q_slab)
    return jnp.where(lane < HEAD_DIM, q_slab, zero), jnp.where(lane >= HEAD_DIM, q_slab, zero)


def _kv_chunk(ref, c, tk, lanes):
    return ref[0, pl.ds(pl.multiple_of(c * tk, tk), tk), lanes]


def _fox_attn_kernel(q_ref, k_ref, v_ref, cq_ref, ck_ref, o_ref, *, tq, tk):
    i = pl.program_id(1)
    row0 = i * tq
    diag = row0 // tk
    cq = cq_ref[0]
    for j in range(N_SLABS):
        lanes = slice(j * LANES, (j + 1) * LANES)
        qs = _split_heads(q_ref[0, :, lanes])
        k_load = lambda c, lanes=lanes: _kv_chunk(k_ref, c, tk, lanes)
        v_load = lambda c, lanes=lanes: _kv_chunk(v_ref, c, tk, lanes)

        def bias_fn(h):
            cq_h = cq[:, h:h + 1]
            return lambda c: cq_h - ck_ref[0, h, pl.ds(c, 1), :]

        out = _attend_pair(qs, (k_load, k_load), v_load, (bias_fn(2 * j), bias_fn(2 * j + 1)),
                           diag, diag, row0, tq, tk)
        o_ref[0, :, lanes] = out.astype(o_ref.dtype)


def _fox_attn(q, k, v, cum_q, cum_k):
    b, s, d = q.shape
    tq, tk = min(TQ, s), min(TK, s)
    return pl.pallas_call(
        functools.partial(_fox_attn_kernel, tq=tq, tk=tk),
        grid=(b, s // tq),
        in_specs=[pl.BlockSpec((1, tq, d), lambda bi, i: (bi, i, 0)),
                  pl.BlockSpec((1, s, d), lambda bi, i: (bi, 0, 0)),
                  pl.BlockSpec((1, s, d), lambda bi, i: (bi, 0, 0)),
                  pl.BlockSpec((1, tq, LANES), lambda bi, i: (bi, i, 0)),
                  pl.BlockSpec((1,) + cum_k.shape[1:], lambda bi, i: (bi, 0, 0, 0))],
        out_specs=pl.BlockSpec((1, tq, d), lambda bi, i: (bi, i, 0)),
        out_shape=jax.ShapeDtypeStruct((b, s, d), BF16),
        compiler_params=_params(2),
        name="fox_attn",
    )(q, k, v, cum_q, cum_k)


def _mla_attn_kernel(q_ref, k_ref, v_ref, o_ref, *, tq, tk):
    i = pl.program_id(1)
    row0 = i * tq
    diag = row0 // tk
    for j in range(N_SLABS):
        head_lanes = [slice(h * LANES, (h + 1) * LANES) for h in (2 * j, 2 * j + 1)]
        qs = tuple(q_ref[0, :, hl] for hl in head_lanes)
        k_loads = tuple((lambda c, hl=hl: _kv_chunk(k_ref, c, tk, hl)) for hl in head_lanes)
        lanes = slice(j * LANES, (j + 1) * LANES)
        v_load = lambda c, lanes=lanes: _kv_chunk(v_ref, c, tk, lanes)
        out = _attend_pair(qs, k_loads, v_load, (None, None), diag, diag, row0, tq, tk)
        o_ref[0, :, lanes] = out.astype(o_ref.dtype)


def _mla_attn(q, k, v):
    b, s, wide = q.shape
    d = v.shape[2]
    tq, tk = min(TQ, s), min(TK, s)
    return pl.pallas_call(
        functools.partial(_mla_attn_kernel, tq=tq, tk=tk),
        grid=(b, s // tq),
        in_specs=[pl.BlockSpec((1, tq, wide), lambda bi, i: (bi, i, 0)),
                  pl.BlockSpec((1, s, wide), lambda bi, i: (bi, 0, 0)),
                  pl.BlockSpec((1, s, d), lambda bi, i: (bi, 0, 0))],
        out_specs=pl.BlockSpec((1, tq, d), lambda bi, i: (bi, i, 0)),
        out_shape=jax.ShapeDtypeStruct((b, s, d), BF16),
        compiler_params=_params(2),
        name="mla_attn",
    )(q, k, v)


ROW_BLOCK = 128


def _sortable_key(score):
    bits = pltpu.bitcast(score + 0.0, I32)
    return jnp.where(bits < 0, bits ^ 0x7FFFFFFF, bits)


def _dsa_attn_kernel(q_ref, k_ref, v_ref, qi_ref, ki_ref, wi_ref, o_ref, qim_ref, key_ref, bias_ref,
                     *, tq, tk, topk):
    i = pl.program_id(1)
    row0 = i * tq
    n_chunks = row0 // tk + 1
    n_rb = tq // ROW_BLOCK
    n_cg = tk // LANES

    for h in range(IDX_HEADS):
        pair = _split_heads(qi_ref[0, :, (h // 2) * LANES:(h // 2 + 1) * LANES])
        qim_ref[h] = pair[h % 2]
    wi = wi_ref[0]

    def score_chunk(c, _):
        ki = _kv_chunk(ki_ref, c, tk, slice(None))
        acc = jnp.zeros((tq, tk), F32)
        for h in range(IDX_HEADS):
            acc = acc + wi[:, h:h + 1] * jnp.maximum(_dot_nt(qim_ref[h], ki), 0.0)
        rows = row0 + lax.broadcasted_iota(I32, (tq, tk), 0)
        cols = c * tk + lax.broadcasted_iota(I32, (tq, tk), 1)
        key_ref[c] = jnp.where(cols <= rows, _sortable_key(acc), INT_MIN)
        return 0

    lax.fori_loop(0, n_chunks, score_chunk, 0)

    def count(rb, indicator):
        rows = slice(rb * ROW_BLOCK, (rb + 1) * ROW_BLOCK)

        def body(c, acc):
            for g in range(n_cg):
                blk = key_ref[c, rows, g * LANES:(g + 1) * LANES]
                acc = acc + indicator(blk, c, g)
            return acc

        acc = lax.fori_loop(0, n_chunks, body, jnp.zeros((ROW_BLOCK, LANES), F32))
        return jnp.sum(acc, axis=1, keepdims=True)

    lane_iota = lax.broadcasted_iota(I32, (ROW_BLOCK, LANES), 1)
    k_f = float(topk)
    for rb in range(n_rb):
        rows = slice(rb * ROW_BLOCK, (rb + 1) * ROW_BLOCK)

        def bit_step(t, prefix):
            cand = prefix | lax.shift_left(jnp.int32(1), 31 - t)
            cand_key = cand ^ INT_MIN
            cnt = count(rb, lambda blk, c, g: jnp.where(blk >= cand_key, 1.0, 0.0))
            return jnp.where(cnt >= k_f, cand, prefix)

        prefix = lax.fori_loop(0, 32, bit_step, jnp.zeros((ROW_BLOCK, 1), I32))
        thr = prefix ^ INT_MIN
        n_gt = count(rb, lambda blk, c, g: jnp.where(blk > thr, 1.0, 0.0))
        n_eq = count(rb, lambda blk, c, g: jnp.where(blk == thr, 1.0, 0.0))
        need = k_f - n_gt
        live = thr != INT_MIN

        def tie_step(t, jmax):
            cand = jmax | lax.shift_left(jnp.int32(1), 12 - t)
            cnt = count(rb, lambda blk, c, g: jnp.where(
                blk == thr, jnp.where(c * tk + g * LANES + lane_iota < cand, 1.0, 0.0), 0.0))
            return jnp.where(cnt < need, cand, jmax)

        has_ties = jnp.max(jnp.where(live, n_eq - need, 0.0)) > 0.0
        jmax = lax.cond(has_ties,
                        lambda: lax.fori_loop(0, 13, tie_step, jnp.zeros((ROW_BLOCK, 1), I32)),
                        lambda: jnp.full((ROW_BLOCK, 1), 2 ** 13 - 1, I32))
        jmax = jnp.where(live, jmax, -1)

        def bias_chunk(c, _):
            for g in range(n_cg):
                blk = key_ref[c, rows, g * LANES:(g + 1) * LANES]
                col = c * tk + g * LANES + lane_iota
                tie_ok = jnp.where(col <= jmax, 0.0, NEG)
                bias_ref[c, rows, g * LANES:(g + 1) * LANES] = jnp.where(
                    blk > thr, 0.0, jnp.where(blk == thr, tie_ok, NEG))
            return 0

        lax.fori_loop(0, n_chunks, bias_chunk, 0)

    for j in range(N_SLABS):
        lanes = slice(j * LANES, (j + 1) * LANES)
        qs = _split_heads(q_ref[0, :, lanes])
        k_load = lambda c, lanes=lanes: _kv_chunk(k_ref, c, tk, lanes)
        v_load = lambda c, lanes=lanes: _kv_chunk(v_ref, c, tk, lanes)
        bias_fn = lambda c: bias_ref[c]
        out = _attend_pair(qs, (k_load, k_load), v_load, (bias_fn, bias_fn),
                           n_chunks, None, row0, tq, tk)
        o_ref[0, :, lanes] = out.astype(o_ref.dtype)


def _dsa_attn(q, k, v, qi, ki, wi):
    b, s, d = q.shape
    tq, tk = min(TQ, s), min(TK, s)
    topk = min(TOPK_MAX, s // 4)
    per_q = lambda width: pl.BlockSpec((1, tq, width), lambda bi, i: (bi, i, 0))
    per_b = lambda width: pl.BlockSpec((1, s, width), lambda bi, i: (bi, 0, 0))
    return pl.pallas_call(
        functools.partial(_dsa_attn_kernel, tq=tq, tk=tk, topk=topk),
        grid=(b, s // tq),
        in_specs=[per_q(d), per_b(d), per_b(d), per_q(qi.shape[2]), per_b(LANES), per_q(LANES)],
        out_specs=per_q(d),
        out_shape=jax.ShapeDtypeStruct((b, s, d), BF16),
        scratch_shapes=[pltpu.VMEM((IDX_HEADS, tq, LANES), BF16),
                        pltpu.VMEM((s // tk, tq, tk), I32),
                        pltpu.VMEM((s // tk, tq, tk), F32)],
        compiler_params=_params(2),
        name="dsa_attn",
    )(q, k, v, qi, ki, wi)


def _dsa_mixer(x2, b, s, w_in, tabs):
    q, k, v, qi, ki, wi = _dsa_proj(x2, _prep_dsa_w(w_in), tabs, s)
    r3 = lambda a: a.reshape(b, s, a.shape[1])
    return _dsa_attn(r3(q), r3(k), r3(v), r3(qi), r3(ki), r3(wi)).reshape(b * s, MIX_WIDTH)


def _fox_mixer(x2, b, s, w_in, b_f):
    d = MIX_WIDTH
    wf = jnp.pad(w_in[:, 3 * d:], ((0, 0), (0, LANES - N_HEADS))).astype(BF16)
    bf = jnp.pad(b_f, (0, LANES - N_HEADS)).reshape(1, LANES)
    q, k, v, cum = _fox_proj(x2, w_in[:, :3 * d].astype(BF16), wf, bf, s)
    r3 = lambda a: a.reshape(b, s, a.shape[1])
    tk = min(TK, s)
    cum_k = cum.reshape(b, s, LANES)[:, :, :N_HEADS].transpose(0, 2, 1).reshape(b, N_HEADS, s // tk, tk)
    return _fox_attn(r3(q), r3(k), r3(v), r3(cum), cum_k).reshape(b * s, d)


def _mla_mixer(x2, b, s, w_dqkv, gq, w_uq, gkv, w_ukv, tabs):
    wd, uq, uk, uv = _prep_mla_w(w_dqkv, w_uq, w_ukv)
    q, k, v = _mla_proj(x2, wd, gq.reshape(1, -1), gkv.reshape(1, -1), uq, uk, uv, tabs, s)
    r3 = lambda a: a.reshape(b, s, a.shape[1])
    return _mla_attn(r3(q), r3(k), r3(v)).reshape(b * s, MIX_WIDTH)


def kernel(x, ffn1_w13, ffn1_w2, ffn2_w13, ffn2_w2, ln_g, ln_b, w_out, dsa_w_in, fox_w_in, fox_b_f,
           mla_w_dqkv, mla_q_norm_g, mla_w_uq, mla_kv_norm_g, mla_w_ukv):
    b, s, dm = x.shape
    x2 = x.reshape(b * s, dm)
    tabs_p = _rope_tables(s, ROT_DIM, 0, HEAD_DIM)
    tabs_m = _rope_tables(s, MLA_ROPE, MLA_NOPE, LANES)
    for i in range(DEPTH):
        g = lambda r, i=i: ln_g[i, r].reshape(1, dm)
        be = lambda r, i=i: ln_b[i, r].reshape(1, dm)
        x2 = _ffn_ln(x2, *_prep_ffn(ffn1_w13[i], ffn1_w2[i]), g(0), be(0))
        kind, j = i % N_MIXERS, i // N_MIXERS
        if kind == 0:
            o = _dsa_mixer(x2, b, s, dsa_w_in[j], tabs_p)
        elif kind == 1:
            o = _fox_mixer(x2, b, s, fox_w_in[j], fox_b_f[j])
        else:
            o = _mla_mixer(x2, b, s, mla_w_dqkv[j], mla_q_norm_g[j], mla_w_uq[j],
                           mla_kv_norm_g[j], mla_w_ukv[j], tabs_m)
        x2 = _outproj_ln(x2, o, w_out[i].astype(BF16), g(1), be(1))
        x2 = _ffn_ln(x2, *_prep_ffn(ffn2_w13[i], ffn2_w2[i]), g(2), be(2))
    return x2.reshape(b, s, dm)
```

```python
import functools

import jax
import jax.numpy as jnp
from jax import lax
from jax.experimental import pallas as pl
from jax.experimental.pallas import tpu as pltpu

F32 = jnp.float32
BF16 = jnp.bfloat16
I32 = jnp.int32

D_MODEL = 1024
DEPTH = 4
N_MIXERS = 3
HEAD_DIM = 64
N_HEADS = D_MODEL // HEAD_DIM
MIX_WIDTH = N_HEADS * HEAD_DIM
ROT_DIM = HEAD_DIM // 4
ROPE_THETA = 500000.0
IDX_HEADS = 8
IDX_DIM = 64
TOPK_MAX = 256
MLA_NOPE = 64
MLA_ROPE = 32
MLA_V = 64
Q_LORA = 384
KV_LORA = 256
D_FF = 2816
ALPHA = (2.0 * DEPTH) ** 0.25
LN_EPS = 1e-5
RMS_EPS = 1e-6

LANES = 128
N_SLABS = MIX_WIDTH // LANES
LOG2E = 1.4426950408889634
NEG = -1e30
INT_MIN = -2147483648

TM = 512
TF = 256
TQ = 256
TK = 512
SLAB_GROUP = 4
VMEM_LIMIT = 56 * 1024 * 1024


def _params(n_axes, vmem=VMEM_LIMIT):
    return pltpu.CompilerParams(dimension_semantics=("arbitrary",) * n_axes, vmem_limit_bytes=vmem)


def _const_spec(shape):
    nd = len(shape)
    return pl.BlockSpec(shape, lambda *_: (0,) * nd)


def _dot(a, b):
    return jnp.dot(a, b, preferred_element_type=F32)


def _dot_nt(a, b):
    return lax.dot_general(a, b, (((1,), (1,)), ((), ())), preferred_element_type=F32)


def _layer_norm(z, g, b):
    mu = jnp.mean(z, axis=-1, keepdims=True)
    d = z - mu
    var = jnp.mean(d * d, axis=-1, keepdims=True)
    return d * lax.rsqrt(var + LN_EPS) * g + b


def _rms_norm(z, g):
    ms = jnp.mean(z * z, axis=-1, keepdims=True)
    return z * lax.rsqrt(ms + RMS_EPS) * g


def _rope_slab(y, cos, sin_lo, sin_hi, half):
    return y * cos + pltpu.roll(y, half, 1) * sin_hi + pltpu.roll(y, LANES - half, 1) * sin_lo


def _ffn_ln_kernel(x_ref, wgu_ref, w2_ref, g_ref, b_ref, o_ref):
    x = x_ref[...]
    xb = x.astype(BF16)
    n_chunks, _, two_tf = wgu_ref.shape
    tf = two_tf // 2
    acc = jnp.zeros(x.shape, F32)
    for c in range(n_chunks):
        gu = _dot(xb, wgu_ref[c])
        gate, up = gu[:, :tf], gu[:, tf:]
        h = gate * jax.nn.sigmoid(gate) * up
        acc = acc + _dot(h.astype(BF16), w2_ref[c])
    o_ref[...] = _layer_norm(ALPHA * x + 0.5 * acc, g_ref[...], b_ref[...])


def _ffn_ln(x2, wgu, w2, g, b):
    n, d = x2.shape
    return pl.pallas_call(
        _ffn_ln_kernel,
        grid=(n // TM,),
        in_specs=[pl.BlockSpec((TM, d), lambda i: (i, 0)),
                  _const_spec(wgu.shape), _const_spec(w2.shape),
                  _const_spec(g.shape), _const_spec(b.shape)],
        out_specs=pl.BlockSpec((TM, d), lambda i: (i, 0)),
        out_shape=jax.ShapeDtypeStruct((n, d), F32),
        compiler_params=_params(1),
        name="ffn_ln",
    )(x2, wgu, w2, g, b)


def _prep_ffn(w13, w2):
    d, two_ff = w13.shape
    ff = two_ff // 2
    nc = ff // TF
    gate = w13[:, :ff].reshape(d, nc, 1, TF)
    up = w13[:, ff:].reshape(d, nc, 1, TF)
    wgu = jnp.concatenate([gate, up], axis=2).transpose(1, 0, 2, 3).reshape(nc, d, 2 * TF)
    return wgu.astype(BF16), w2.reshape(nc, TF, d).astype(BF16)


def _outproj_ln_kernel(x_ref, o_ref, w_ref, g_ref, b_ref, out_ref):
    y = _dot(o_ref[...], w_ref[...])
    out_ref[...] = _layer_norm(ALPHA * x_ref[...] + y, g_ref[...], b_ref[...])


def _outproj_ln(x2, o2, w, g, b):
    n, d = x2.shape
    return pl.pallas_call(
        _outproj_ln_kernel,
        grid=(n // TM,),
        in_specs=[pl.BlockSpec((TM, d), lambda i: (i, 0)),
                  pl.BlockSpec((TM, o2.shape[1]), lambda i: (i, 0)),
                  _const_spec(w.shape), _const_spec(g.shape), _const_spec(b.shape)],
        out_specs=pl.BlockSpec((TM, d), lambda i: (i, 0)),
        out_shape=jax.ShapeDtypeStruct((n, d), F32),
        compiler_params=_params(1),
        name="outproj_ln",
    )(x2, o2, w, g, b)


def _rope_tables(seq, rot, lane_of_rot0, period):
    half = rot // 2
    inv = ROPE_THETA ** (-jnp.arange(0, rot, 2, dtype=F32) / rot)
    ang = jnp.arange(seq, dtype=F32)[:, None] * inv[None, :]
    cos_h, sin_h = jnp.cos(ang), jnp.sin(ang)
    r = (jnp.arange(LANES) % period) - lane_of_rot0
    in_lo = (r >= 0) & (r < half)
    in_hi = (r >= half) & (r < rot)
    idx = jnp.clip(jnp.where(in_hi, r - half, r), 0, half - 1)
    cos = jnp.where((in_lo | in_hi)[None, :], cos_h[:, idx], 1.0)
    sin_lo = jnp.where(in_lo[None, :], -sin_h[:, idx], 0.0)
    sin_hi = jnp.where(in_hi[None, :], sin_h[:, idx], 0.0)
    return cos, sin_lo, sin_hi


def _dsa_proj_kernel(x_ref, w_ref, cos_ref, slo_ref, shi_ref,
                     q_ref, k_ref, v_ref, qi_ref, ki_ref, wi_ref):
    xb = x_ref[...].astype(BF16)
    cos, slo, shi = cos_ref[...], slo_ref[...], shi_ref[...]
    half = ROT_DIM // 2
    d = MIX_WIDTH
    di = IDX_HEADS * IDX_DIM

    def roped(col0, width, scale, out_ref):
        y = _dot(xb, w_ref[:, col0:col0 + width])
        for j in range(width // LANES):
            s = _rope_slab(y[:, j * LANES:(j + 1) * LANES], cos, slo, shi, half)
            if scale != 1.0:
                s = s * scale
            out_ref[:, j * LANES:(j + 1) * LANES] = s.astype(out_ref.dtype)

    roped(0, d, HEAD_DIM ** -0.5 * LOG2E, q_ref)
    roped(d, d, 1.0, k_ref)
    v_ref[...] = _dot(xb, w_ref[:, 2 * d:3 * d]).astype(BF16)
    roped(3 * d, di, IDX_DIM ** -0.5, qi_ref)
    roped(3 * d + di, LANES, 1.0, ki_ref)
    wi_ref[...] = _dot(xb, w_ref[:, 3 * d + di + LANES:]) * (IDX_HEADS ** -0.5)


def _dsa_proj(x2, w, tabs, seq):
    n, dm = x2.shape
    d, di = MIX_WIDTH, IDX_HEADS * IDX_DIM
    tiles_per_seq = seq // TM
    row = lambda width: pl.BlockSpec((TM, width), lambda i: (i, 0))
    tab = pl.BlockSpec((TM, LANES), lambda i: (i % tiles_per_seq, 0))
    return pl.pallas_call(
        _dsa_proj_kernel,
        grid=(n // TM,),
        in_specs=[row(dm), _const_spec(w.shape), tab, tab, tab],
        out_specs=[row(d), row(d), row(d), row(di), row(LANES), row(LANES)],
        out_shape=[jax.ShapeDtypeStruct((n, d), BF16), jax.ShapeDtypeStruct((n, d), BF16),
                   jax.ShapeDtypeStruct((n, d), BF16), jax.ShapeDtypeStruct((n, di), BF16),
                   jax.ShapeDtypeStruct((n, LANES), BF16), jax.ShapeDtypeStruct((n, LANES), F32)],
        compiler_params=_params(1),
        name="dsa_proj",
    )(x2, w, *tabs)


def _prep_dsa_w(w_in):
    d, di = MIX_WIDTH, IDX_HEADS * IDX_DIM
    ki = w_in[:, 3 * d + di:3 * d + di + IDX_DIM]
    wi = w_in[:, 3 * d + di + IDX_DIM:]
    pad = jnp.zeros((w_in.shape[0], LANES - IDX_HEADS), w_in.dtype)
    return jnp.concatenate([w_in[:, :3 * d + di], ki, ki, wi, pad], axis=1).astype(BF16)


def _fox_proj_kernel(x_ref, w_ref, wf_ref, bf_ref, q_ref, k_ref, v_ref, cum_ref, carry_ref,
                     *, tiles_per_seq):
    i = pl.program_id(0)
    xb = x_ref[...].astype(BF16)
    d = MIX_WIDTH
    q_ref[...] = (_dot(xb, w_ref[:, :d]) * (HEAD_DIM ** -0.5 * LOG2E)).astype(BF16)
    k_ref[...] = _dot(xb, w_ref[:, d:2 * d]).astype(BF16)
    v_ref[...] = _dot(xb, w_ref[:, 2 * d:]).astype(BF16)

    f = _dot(xb, wf_ref[...]) + bf_ref[...]
    log_f = (jnp.minimum(f, 0.0) - jnp.log1p(jnp.exp(-jnp.abs(f)))) * LOG2E
    tm = log_f.shape[0]
    tri = (lax.broadcasted_iota(I32, (tm, tm), 0) >= lax.broadcasted_iota(I32, (tm, tm), 1))
    tri = jnp.where(tri, 1.0, 0.0).astype(BF16)
    p0 = log_f.astype(BF16)
    r1 = log_f - p0.astype(F32)
    p1 = r1.astype(BF16)
    p2 = (r1 - p1.astype(F32)).astype(BF16)
    local = _dot(tri, p0) + _dot(tri, p1) + _dot(tri, p2)

    @pl.when(i % tiles_per_seq == 0)
    def _():
        carry_ref[...] = jnp.zeros_like(carry_ref)

    cum = local + carry_ref[...]
    cum_ref[...] = cum
    carry_ref[...] = cum[tm - 1:tm, :]


def _fox_proj(x2, w, wf, bf, seq):
    n, dm = x2.shape
    d = MIX_WIDTH
    row = lambda width: pl.BlockSpec((TM, width), lambda i: (i, 0))
    return pl.pallas_call(
        functools.partial(_fox_proj_kernel, tiles_per_seq=seq // TM),
        grid=(n // TM,),
        in_specs=[row(dm), _const_spec(w.shape), _const_spec(wf.shape), _const_spec(bf.shape)],
        out_specs=[row(d), row(d), row(d), row(LANES)],
        out_shape=[jax.ShapeDtypeStruct((n, d), BF16)] * 3 + [jax.ShapeDtypeStruct((n, LANES), F32)],
        scratch_shapes=[pltpu.VMEM((1, LANES), F32)],
        compiler_params=_params(1),
        name="fox_proj",
    )(x2, w, wf, bf)


def _mla_proj_kernel(x_ref, wd_ref, gq_ref, gkv_ref, wuq_ref, wuk_ref, wuv_ref,
                     cos_ref, slo_ref, shi_ref, q_ref, k_ref, v_ref):
    xb = x_ref[...].astype(BF16)
    cos, slo, shi = cos_ref[...], slo_ref[...], shi_ref[...]
    half = MLA_ROPE // 2
    c = _dot(xb, wd_ref[...])
    cq = _rms_norm(c[:, :Q_LORA], gq_ref[...]).astype(BF16)
    ckv = _rms_norm(c[:, Q_LORA:Q_LORA + KV_LORA], gkv_ref[...]).astype(BF16)
    k_rope = _rope_slab(c[:, Q_LORA + KV_LORA:], cos, slo, shi, half)
    scale = (MLA_NOPE + MLA_ROPE) ** -0.5 * LOG2E
    for h in range(N_HEADS):
        sl = slice(h * LANES, (h + 1) * LANES)
        q = _rope_slab(_dot(cq, wuq_ref[:, sl]), cos, slo, shi, half)
        q_ref[:, sl] = (q * scale).astype(BF16)
        k_ref[:, sl] = (_dot(ckv, wuk_ref[:, sl]) + k_rope).astype(BF16)
    v_ref[...] = _dot(ckv, wuv_ref[...]).astype(BF16)


def _mla_proj(x2, wd, gq, gkv, wuq, wuk, wuv, tabs, seq):
    n, dm = x2.shape
    tiles_per_seq = seq // TM
    row = lambda width: pl.BlockSpec((TM, width), lambda i: (i, 0))
    tab = pl.BlockSpec((TM, LANES), lambda i: (i % tiles_per_seq, 0))
    wide = N_HEADS * LANES
    return pl.pallas_call(
        _mla_proj_kernel,
        grid=(n // TM,),
        in_specs=[row(dm)] + [_const_spec(a.shape) for a in (wd, gq, gkv, wuq, wuk, wuv)] + [tab] * 3,
        out_specs=[row(wide), row(wide), row(MIX_WIDTH)],
        out_shape=[jax.ShapeDtypeStruct((n, wide), BF16), jax.ShapeDtypeStruct((n, wide), BF16),
                   jax.ShapeDtypeStruct((n, MIX_WIDTH), BF16)],
        compiler_params=_params(1),
        name="mla_proj",
    )(x2, wd, gq, gkv, wuq, wuk, wuv, *tabs)


def _prep_mla_w(w_dqkv, w_uq, w_ukv):
    dm = w_dqkv.shape[0]
    zeros = lambda r, c: jnp.zeros((r, c), w_dqkv.dtype)
    k_rope_w = w_dqkv[:, Q_LORA + KV_LORA:]
    wd = jnp.concatenate([w_dqkv[:, :Q_LORA + KV_LORA], zeros(dm, MLA_NOPE), k_rope_w,
                          zeros(dm, LANES - MLA_NOPE - MLA_ROPE)], axis=1)
    uq = w_uq.reshape(Q_LORA, N_HEADS, MLA_NOPE + MLA_ROPE)
    uq = jnp.pad(uq, ((0, 0), (0, 0), (0, LANES - MLA_NOPE - MLA_ROPE))).reshape(Q_LORA, N_HEADS * LANES)
    ukv = w_ukv.reshape(KV_LORA, N_HEADS, MLA_NOPE + MLA_V)
    uk = jnp.pad(ukv[:, :, :MLA_NOPE], ((0, 0), (0, 0), (0, LANES - MLA_NOPE))).reshape(KV_LORA, N_HEADS * LANES)
    uv = ukv[:, :, MLA_NOPE:].reshape(KV_LORA, N_HEADS * MLA_V)
    return wd.astype(BF16), uq.astype(BF16), uk.astype(BF16), uv.astype(BF16)


def _attend_slabs(slabs, n_plain, masked_chunk, qpos, tk):
    m2 = slabs[0][0].shape[0]
    tq = m2 // 2

    def step(c, carries, mask):
        scores = [_dot_nt(k_load(c), q2) for q2, k_load, _, _ in slabs]
        out = []
        for (_, _, vt_load, bias_fn), (m, l, acc), s in zip(slabs, carries, scores):
            if bias_fn is not None:
                s = bias_fn(c) + s
            if mask is not None:
                s = jnp.where(mask, s, NEG)
            m_new = jnp.maximum(m, jnp.max(s, axis=0, keepdims=True))
            alpha = jnp.exp2(m - m_new)
            p = jnp.exp2(s - m_new)
            l = alpha * l + jnp.sum(p, axis=0, keepdims=True)
            acc = alpha * acc + _dot(vt_load(c), p.astype(BF16))
            out.append((m_new, l, acc))
        return tuple(out)

    carries = tuple((jnp.full((1, m2), NEG, F32), jnp.zeros((1, m2), F32), jnp.zeros((LANES, m2), F32))
                    for _ in slabs)
    carries = lax.fori_loop(0, n_plain, lambda c, cr: step(c, cr, None), carries)
    if masked_chunk is not None:
        kpos = masked_chunk * tk + lax.broadcasted_iota(I32, (tk, m2), 0)
        carries = step(masked_chunk, carries, kpos <= qpos)
    row = lax.broadcasted_iota(I32, (LANES, tq), 0)
    outs = []
    for _, l, acc in carries:
        o = acc / l
        outs.append(jnp.where(row < HEAD_DIM, o[:, :tq], o[:, tq:]))
    return outs


def _split_heads(q_slab):
    lane = lax.broadcasted_iota(I32, q_slab.shape, 1)
    zero = jnp.zeros_like(q_slab)
    return jnp.where(lane < HEAD_DIM, q_slab, zero), jnp.where(lane >= HEAD_DIM, q_slab, zero)


def _kv_chunk(ref, c, tk, lanes):
    return ref[0, pl.ds(pl.multiple_of(c * tk, tk), tk), lanes]


def _per_batch_spec(a):
    nd = a.ndim
    return pl.BlockSpec((1,) + a.shape[1:], lambda bi, i: (bi,) + (0,) * (nd - 1),
                        pipeline_mode=pl.Buffered(1))


def _query_positions(row0, tq):
    pos = row0 + lax.broadcasted_iota(I32, (1, tq), 1)
    return jnp.concatenate([pos, pos], axis=1)


def _transpose_values(v, b, s, tk):
    return v.reshape(b, s // tk, tk, v.shape[1]).transpose(0, 1, 3, 2)


def _fox_attn_kernel(q_ref, k_ref, vt_ref, cq_ref, ck_ref, o_ref, *, tq, tk):
    i = pl.program_id(1)
    row0 = i * tq
    diag = row0 // tk
    qpos = _query_positions(row0, tq)
    cq = cq_ref[0]

    def slab(j):
        lanes = slice(j * LANES, (j + 1) * LANES)
        q2 = jnp.concatenate(_split_heads(q_ref[0, :, lanes]), axis=0)
        cq2 = jnp.concatenate([cq[2 * j:2 * j + 1, :], cq[2 * j + 1:2 * j + 2, :]], axis=1)

        def bias_fn(c):
            ck = ck_ref[0, pl.ds(pl.multiple_of(c * tk, tk), tk), :]
            ck2 = jnp.concatenate([jnp.broadcast_to(ck[:, 2 * j:2 * j + 1], (tk, tq)),
                                   jnp.broadcast_to(ck[:, 2 * j + 1:2 * j + 2], (tk, tq))], axis=1)
            return cq2 - ck2

        return (q2, lambda c: _kv_chunk(k_ref, c, tk, lanes), lambda c: vt_ref[0, c, lanes, :], bias_fn)

    for j0 in range(0, N_SLABS, SLAB_GROUP):
        outs = _attend_slabs([slab(j) for j in range(j0, j0 + SLAB_GROUP)], diag, diag, qpos, tk)
        for j, out in zip(range(j0, j0 + SLAB_GROUP), outs):
            o_ref[0, :, j * LANES:(j + 1) * LANES] = out.T.astype(o_ref.dtype)


def _fox_attn(q, k, vt, cum_q, cum_k):
    b, s, d = q.shape
    tq, tk = min(TQ, s), vt.shape[3]
    return pl.pallas_call(
        functools.partial(_fox_attn_kernel, tq=tq, tk=tk),
        grid=(b, s // tq),
        in_specs=[pl.BlockSpec((1, tq, d), lambda bi, i: (bi, i, 0)),
                  _per_batch_spec(k), _per_batch_spec(vt),
                  pl.BlockSpec((1, N_HEADS, tq), lambda bi, i: (bi, 0, i)),
                  _per_batch_spec(cum_k)],
        out_specs=pl.BlockSpec((1, tq, d), lambda bi, i: (bi, i, 0)),
        out_shape=jax.ShapeDtypeStruct((b, s, d), BF16),
        compiler_params=_params(2),
        name="fox_attn",
    )(q, k, vt, cum_q, cum_k)


def _mla_attn_kernel(q_ref, k_ref, vt_ref, o_ref, *, tq, tk):
    i = pl.program_id(1)
    row0 = i * tq
    diag = row0 // tk
    qpos = _query_positions(row0, tq)
    zero = jnp.zeros((tq, LANES), BF16)

    def slab(j):
        pair = slice(2 * j * LANES, (2 * j + 2) * LANES)
        qa = q_ref[0, :, 2 * j * LANES:(2 * j + 1) * LANES]
        qb = q_ref[0, :, (2 * j + 1) * LANES:(2 * j + 2) * LANES]
        q2 = jnp.concatenate([jnp.concatenate([qa, zero], axis=1),
                              jnp.concatenate([zero, qb], axis=1)], axis=0)
        lanes = slice(j * LANES, (j + 1) * LANES)
        return (q2, lambda c: _kv_chunk(k_ref, c, tk, pair), lambda c: vt_ref[0, c, lanes, :], None)

    for j0 in range(0, N_SLABS, SLAB_GROUP):
        outs = _attend_slabs([slab(j) for j in range(j0, j0 + SLAB_GROUP)], diag, diag, qpos, tk)
        for j, out in zip(range(j0, j0 + SLAB_GROUP), outs):
            o_ref[0, :, j * LANES:(j + 1) * LANES] = out.T.astype(o_ref.dtype)


def _mla_attn(q, k, vt):
    b, s, wide = q.shape
    d = vt.shape[2]
    tq, tk = min(TQ, s), vt.shape[3]
    return pl.pallas_call(
        functools.partial(_mla_attn_kernel, tq=tq, tk=tk),
        grid=(b, s // tq),
        in_specs=[pl.BlockSpec((1, tq, wide), lambda bi, i: (bi, i, 0)),
                  _per_batch_spec(k), _per_batch_spec(vt)],
        out_specs=pl.BlockSpec((1, tq, d), lambda bi, i: (bi, i, 0)),
        out_shape=jax.ShapeDtypeStruct((b, s, d), BF16),
        compiler_params=_params(2),
        name="mla_attn",
    )(q, k, vt)


def _sortable_key(score):
    bits = pltpu.bitcast(score + 0.0, I32)
    return jnp.where(bits < 0, bits ^ 0x7FFFFFFF, bits)


def _dsa_attn_kernel(q_ref, k_ref, vt_ref, qi_ref, ki_ref, wit_ref, o_ref, qim_ref, key_ref, bias_ref,
                     *, tq, tk, topk):
    i = pl.program_id(1)
    row0 = i * tq
    n_chunks = row0 // tk + 1
    qpos = row0 + lax.broadcasted_iota(I32, (1, tq), 1)
    kiota = lax.broadcasted_iota(I32, (tk, tq), 0)

    for h in range(IDX_HEADS):
        pair = _split_heads(qi_ref[0, :, (h // 2) * LANES:(h // 2 + 1) * LANES])
        qim_ref[h] = pair[h % 2]
    wit = wit_ref[0]

    def score_chunk(c, _):
        ki = _kv_chunk(ki_ref, c, tk, slice(None))
        acc = jnp.zeros((tk, tq), F32)
        for h in range(IDX_HEADS):
            acc = acc + wit[h:h + 1, :] * jnp.maximum(_dot_nt(ki, qim_ref[h]), 0.0)
        key_ref[c] = jnp.where(c * tk + kiota <= qpos, _sortable_key(acc), INT_MIN)
        return 0

    lax.fori_loop(0, n_chunks, score_chunk, 0)

    def count(indicator):
        def body(c, acc):
            ind = indicator(key_ref[c], c)
            return acc + jnp.sum(ind.reshape(tk // 8, 8, tq), axis=0)

        acc = lax.fori_loop(0, n_chunks, body, jnp.zeros((8, tq), F32))
        return jnp.sum(acc, axis=0, keepdims=True)

    k_f = float(topk)

    def bit_step(t, prefix):
        cand = prefix | lax.shift_left(jnp.int32(1), 31 - t)
        cand_key = cand ^ INT_MIN
        cnt = count(lambda blk, c: jnp.where(blk >= cand_key, 1.0, 0.0))
        return jnp.where(cnt >= k_f, cand, prefix)

    prefix = lax.fori_loop(0, 32, bit_step, jnp.zeros((1, tq), I32))
    thr = prefix ^ INT_MIN
    n_gt = count(lambda blk, c: jnp.where(blk > thr, 1.0, 0.0))
    n_eq = count(lambda blk, c: jnp.where(blk == thr, 1.0, 0.0))
    need = k_f - n_gt
    live = thr != INT_MIN

    def tie_step(t, jmax):
        cand = jmax | lax.shift_left(jnp.int32(1), 12 - t)
        cnt = count(lambda blk, c: jnp.where(
            blk == thr, jnp.where(c * tk + kiota < cand, 1.0, 0.0), 0.0))
        return jnp.where(cnt < need, cand, jmax)

    has_ties = jnp.max(jnp.where(live, n_eq - need, 0.0)) > 0.0
    jmax = lax.cond(has_ties,
                    lambda: lax.fori_loop(0, 13, tie_step, jnp.zeros((1, tq), I32)),
                    lambda: jnp.full((1, tq), 2 ** 13 - 1, I32))
    jmax = jnp.where(live, jmax, -1)

    def bias_chunk(c, _):
        blk = key_ref[c]
        tie_ok = jnp.where(c * tk + kiota <= jmax, 0.0, NEG)
        bias_ref[c] = jnp.where(blk > thr, 0.0, jnp.where(blk == thr, tie_ok, NEG))
        return 0

    lax.fori_loop(0, n_chunks, bias_chunk, 0)

    def bias_fn(c):
        b = bias_ref[c]
        return jnp.concatenate([b, b], axis=1)

    def slab(j):
        lanes = slice(j * LANES, (j + 1) * LANES)
        q2 = jnp.concatenate(_split_heads(q_ref[0, :, lanes]), axis=0)
        return (q2, lambda c: _kv_chunk(k_ref, c, tk, lanes), lambda c: vt_ref[0, c, lanes, :], bias_fn)

    for j0 in range(0, N_SLABS, SLAB_GROUP):
        outs = _attend_slabs([slab(j) for j in range(j0, j0 + SLAB_GROUP)], n_chunks, None, None, tk)
        for j, out in zip(range(j0, j0 + SLAB_GROUP), outs):
            o_ref[0, :, j * LANES:(j + 1) * LANES] = out.T.astype(o_ref.dtype)


def _dsa_attn(q, k, vt, qi, ki, wit):
    b, s, d = q.shape
    tq, tk = min(TQ, s), vt.shape[3]
    topk = min(TOPK_MAX, s // 4)
    per_q = lambda width: pl.BlockSpec((1, tq, width), lambda bi, i: (bi, i, 0))
    return pl.pallas_call(
        functools.partial(_dsa_attn_kernel, tq=tq, tk=tk, topk=topk),
        grid=(b, s // tq),
        in_specs=[per_q(d), _per_batch_spec(k), _per_batch_spec(vt),
                  per_q(qi.shape[2]), _per_batch_spec(ki),
                  pl.BlockSpec((1, IDX_HEADS, tq), lambda bi, i: (bi, 0, i))],
        out_specs=per_q(d),
        out_shape=jax.ShapeDtypeStruct((b, s, d), BF16),
        scratch_shapes=[pltpu.VMEM((IDX_HEADS, tq, LANES), BF16),
                        pltpu.VMEM((s // tk, tk, tq), I32),
                        pltpu.VMEM((s // tk, tk, tq), F32)],
        compiler_params=_params(2),
        name="dsa_attn",
    )(q, k, vt, qi, ki, wit)


def _dsa_mixer(x2, b, s, w_in, tabs):
    q, k, v, qi, ki, wi = _dsa_proj(x2, _prep_dsa_w(w_in), tabs, s)
    r3 = lambda a: a.reshape(b, s, a.shape[1])
    wit = r3(wi)[:, :, :IDX_HEADS].transpose(0, 2, 1)
    o = _dsa_attn(r3(q), r3(k), _transpose_values(v, b, s, min(TK, s)), r3(qi), r3(ki), wit)
    return o.reshape(b * s, MIX_WIDTH)


def _fox_mixer(x2, b, s, w_in, b_f):
    d = MIX_WIDTH
    wf = jnp.pad(w_in[:, 3 * d:], ((0, 0), (0, LANES - N_HEADS))).astype(BF16)
    bf = jnp.pad(b_f, (0, LANES - N_HEADS)).reshape(1, LANES)
    q, k, v, cum = _fox_proj(x2, w_in[:, :3 * d].astype(BF16), wf, bf, s)
    r3 = lambda a: a.reshape(b, s, a.shape[1])
    cum_q = r3(cum)[:, :, :N_HEADS].transpose(0, 2, 1)
    o = _fox_attn(r3(q), r3(k), _transpose_values(v, b, s, min(TK, s)), cum_q, r3(cum))
    return o.reshape(b * s, d)


def _mla_mixer(x2, b, s, w_dqkv, gq, w_uq, gkv, w_ukv, tabs):
    wd, uq, uk, uv = _prep_mla_w(w_dqkv, w_uq, w_ukv)
    q, k, v = _mla_proj(x2, wd, gq.reshape(1, -1), gkv.reshape(1, -1), uq, uk, uv, tabs, s)
    r3 = lambda a: a.reshape(b, s, a.shape[1])
    o = _mla_attn(r3(q), r3(k), _transpose_values(v, b, s, min(TK, s)))
    return o.reshape(b * s, MIX_WIDTH)


def kernel(x, ffn1_w13, ffn1_w2, ffn2_w13, ffn2_w2, ln_g, ln_b, w_out, dsa_w_in, fox_w_in, fox_b_f,
           mla_w_dqkv, mla_q_norm_g, mla_w_uq, mla_kv_norm_g, mla_w_ukv):
    b, s, dm = x.shape
    x2 = x.reshape(b * s, dm)
    tabs_p = _rope_tables(s, ROT_DIM, 0, HEAD_DIM)
    tabs_m = _rope_tables(s, MLA_ROPE, MLA_NOPE, LANES)
    for i in range(DEPTH):
        g = lambda r, i=i: ln_g[i, r].reshape(1, dm)
        be = lambda r, i=i: ln_b[i, r].reshape(1, dm)
        x2 = _ffn_ln(x2, *_prep_ffn(ffn1_w13[i], ffn1_w2[i]), g(0), be(0))
        kind, j = i % N_MIXERS, i // N_MIXERS
        if kind == 0:
            o = _dsa_mixer(x2, b, s, dsa_w_in[j], tabs_p)
        elif kind == 1:
            o = _fox_mixer(x2, b, s, fox_w_in[j], fox_b_f[j])
        else:
            o = _mla_mixer(x2, b, s, mla_w_dqkv[j], mla_q_norm_g[j], mla_w_uq[j],
                           mla_kv_norm_g[j], mla_w_ukv[j], tabs_m)
        x2 = _outproj_ln(x2, o, w_out[i].astype(BF16), g(1), be(1))
        x2 = _ffn_ln(x2, *_prep_ffn(ffn2_w13[i], ffn2_w2[i]), g(2), be(2))
    return x2.reshape(b, s, dm)
```

```python
import functools

import jax
import jax.numpy as jnp
from jax import lax
from jax.experimental import pallas as pl
from jax.experimental.pallas import tpu as pltpu

F32 = jnp.float32
BF16 = jnp.bfloat16
I32 = jnp.int32
I16 = jnp.int16

D_MODEL = 1024
DEPTH = 4
N_MIXERS = 3
HEAD_DIM = 64
N_HEADS = D_MODEL // HEAD_DIM
MIX_WIDTH = N_HEADS * HEAD_DIM
ROT_DIM = HEAD_DIM // 4
ROPE_THETA = 500000.0
IDX_HEADS = 8
IDX_DIM = 64
TOPK_MAX = 256
MLA_NOPE = 64
MLA_ROPE = 32
MLA_V = 64
Q_LORA = 384
KV_LORA = 256
D_FF = 2816
ALPHA = (2.0 * DEPTH) ** 0.25
LN_EPS = 1e-5
RMS_EPS = 1e-6

LANES = 128
N_SLABS = MIX_WIDTH // LANES
LOG2E = 1.4426950408889634
NEG = -1e30
INT_MIN = -2147483648

TM = 512
TF = 256
TQ = 256
TK = 512
SLAB_GROUP = 8
VMEM_LIMIT = 56 * 1024 * 1024


def _params(n_axes, vmem=VMEM_LIMIT):
    return pltpu.CompilerParams(dimension_semantics=("arbitrary",) * n_axes, vmem_limit_bytes=vmem)


def _const_spec(shape):
    nd = len(shape)
    return pl.BlockSpec(shape, lambda *_: (0,) * nd)


def _dot(a, b):
    return jnp.dot(a, b, preferred_element_type=F32)


def _dot_nt(a, b):
    return lax.dot_general(a, b, (((1,), (1,)), ((), ())), preferred_element_type=F32)


def _layer_norm(z, g, b):
    mu = jnp.mean(z, axis=-1, keepdims=True)
    d = z - mu
    var = jnp.mean(d * d, axis=-1, keepdims=True)
    return d * lax.rsqrt(var + LN_EPS) * g + b


def _rms_norm(z, g):
    ms = jnp.mean(z * z, axis=-1, keepdims=True)
    return z * lax.rsqrt(ms + RMS_EPS) * g


def _rope_slab(y, cos, sin_lo, sin_hi, half):
    return y * cos + pltpu.roll(y, half, 1) * sin_hi + pltpu.roll(y, LANES - half, 1) * sin_lo


def _ffn_ln_kernel(x_ref, wgu_ref, w2_ref, g_ref, b_ref, o_ref):
    x = x_ref[...]
    xb = x.astype(BF16)
    n_chunks, _, two_tf = wgu_ref.shape
    tf = two_tf // 2
    acc = jnp.zeros(x.shape, F32)
    for c in range(n_chunks):
        gu = _dot(xb, wgu_ref[c])
        gate, up = gu[:, :tf], gu[:, tf:]
        h = gate * jax.nn.sigmoid(gate) * up
        acc = acc + _dot(h.astype(BF16), w2_ref[c])
    o_ref[...] = _layer_norm(ALPHA * x + 0.5 * acc, g_ref[...], b_ref[...])


def _ffn_ln(x2, wgu, w2, g, b):
    n, d = x2.shape
    return pl.pallas_call(
        _ffn_ln_kernel,
        grid=(n // TM,),
        in_specs=[pl.BlockSpec((TM, d), lambda i: (i, 0)),
                  _const_spec(wgu.shape), _const_spec(w2.shape),
                  _const_spec(g.shape), _const_spec(b.shape)],
        out_specs=pl.BlockSpec((TM, d), lambda i: (i, 0)),
        out_shape=jax.ShapeDtypeStruct((n, d), F32),
        compiler_params=_params(1),
        name="ffn_ln",
    )(x2, wgu, w2, g, b)


def _prep_ffn(w13, w2):
    d, two_ff = w13.shape
    ff = two_ff // 2
    nc = ff // TF
    gate = w13[:, :ff].reshape(d, nc, 1, TF)
    up = w13[:, ff:].reshape(d, nc, 1, TF)
    wgu = jnp.concatenate([gate, up], axis=2).transpose(1, 0, 2, 3).reshape(nc, d, 2 * TF)
    return wgu.astype(BF16), w2.reshape(nc, TF, d).astype(BF16)


def _outproj_ln_kernel(x_ref, o_ref, w_ref, g_ref, b_ref, out_ref):
    y = _dot(o_ref[...], w_ref[...])
    out_ref[...] = _layer_norm(ALPHA * x_ref[...] + y, g_ref[...], b_ref[...])


def _outproj_ln(x2, o2, w, g, b):
    n, d = x2.shape
    return pl.pallas_call(
        _outproj_ln_kernel,
        grid=(n // TM,),
        in_specs=[pl.BlockSpec((TM, d), lambda i: (i, 0)),
                  pl.BlockSpec((TM, o2.shape[1]), lambda i: (i, 0)),
                  _const_spec(w.shape), _const_spec(g.shape), _const_spec(b.shape)],
        out_specs=pl.BlockSpec((TM, d), lambda i: (i, 0)),
        out_shape=jax.ShapeDtypeStruct((n, d), F32),
        compiler_params=_params(1),
        name="outproj_ln",
    )(x2, o2, w, g, b)


def _rope_tables(seq, rot, lane_of_rot0, period):
    half = rot // 2
    inv = ROPE_THETA ** (-jnp.arange(0, rot, 2, dtype=F32) / rot)
    ang = jnp.arange(seq, dtype=F32)[:, None] * inv[None, :]
    cos_h, sin_h = jnp.cos(ang), jnp.sin(ang)
    r = (jnp.arange(LANES) % period) - lane_of_rot0
    in_lo = (r >= 0) & (r < half)
    in_hi = (r >= half) & (r < rot)
    idx = jnp.clip(jnp.where(in_hi, r - half, r), 0, half - 1)
    cos = jnp.where((in_lo | in_hi)[None, :], cos_h[:, idx], 1.0)
    sin_lo = jnp.where(in_lo[None, :], -sin_h[:, idx], 0.0)
    sin_hi = jnp.where(in_hi[None, :], sin_h[:, idx], 0.0)
    return cos, sin_lo, sin_hi


def _dsa_proj_kernel(x_ref, w_ref, cos_ref, slo_ref, shi_ref,
                     q_ref, k_ref, v_ref, qi_ref, ki_ref, wi_ref):
    xb = x_ref[...].astype(BF16)
    cos, slo, shi = cos_ref[...], slo_ref[...], shi_ref[...]
    half = ROT_DIM // 2
    d = MIX_WIDTH
    di = IDX_HEADS * IDX_DIM

    def roped(col0, width, scale, out_ref):
        y = _dot(xb, w_ref[:, col0:col0 + width])
        for j in range(width // LANES):
            s = _rope_slab(y[:, j * LANES:(j + 1) * LANES], cos, slo, shi, half)
            if scale != 1.0:
                s = s * scale
            out_ref[:, j * LANES:(j + 1) * LANES] = s.astype(out_ref.dtype)

    roped(0, d, HEAD_DIM ** -0.5 * LOG2E, q_ref)
    roped(d, d, 1.0, k_ref)
    v_ref[...] = _dot(xb, w_ref[:, 2 * d:3 * d]).astype(BF16)
    roped(3 * d, di, IDX_DIM ** -0.5, qi_ref)
    roped(3 * d + di, LANES, 1.0, ki_ref)
    wi_ref[...] = _dot(xb, w_ref[:, 3 * d + di + LANES:]) * (IDX_HEADS ** -0.5)


def _dsa_proj(x2, w, tabs, seq):
    n, dm = x2.shape
    d, di = MIX_WIDTH, IDX_HEADS * IDX_DIM
    tiles_per_seq = seq // TM
    row = lambda width: pl.BlockSpec((TM, width), lambda i: (i, 0))
    tab = pl.BlockSpec((TM, LANES), lambda i: (i % tiles_per_seq, 0))
    return pl.pallas_call(
        _dsa_proj_kernel,
        grid=(n // TM,),
        in_specs=[row(dm), _const_spec(w.shape), tab, tab, tab],
        out_specs=[row(d), row(d), row(d), row(di), row(LANES), row(LANES)],
        out_shape=[jax.ShapeDtypeStruct((n, d), BF16), jax.ShapeDtypeStruct((n, d), BF16),
                   jax.ShapeDtypeStruct((n, d), BF16), jax.ShapeDtypeStruct((n, di), BF16),
                   jax.ShapeDtypeStruct((n, LANES), BF16), jax.ShapeDtypeStruct((n, LANES), F32)],
        compiler_params=_params(1),
        name="dsa_proj",
    )(x2, w, *tabs)


def _prep_dsa_w(w_in):
    d, di = MIX_WIDTH, IDX_HEADS * IDX_DIM
    ki = w_in[:, 3 * d + di:3 * d + di + IDX_DIM]
    wi = w_in[:, 3 * d + di + IDX_DIM:]
    pad = jnp.zeros((w_in.shape[0], LANES - IDX_HEADS), w_in.dtype)
    return jnp.concatenate([w_in[:, :3 * d + di], ki, ki, wi, pad], axis=1).astype(BF16)


def _fox_proj_kernel(x_ref, w_ref, wf_ref, bf_ref, q_ref, k_ref, v_ref, cum_ref, carry_ref,
                     *, tiles_per_seq):
    i = pl.program_id(0)
    xb = x_ref[...].astype(BF16)
    d = MIX_WIDTH
    q_ref[...] = (_dot(xb, w_ref[:, :d]) * (HEAD_DIM ** -0.5 * LOG2E)).astype(BF16)
    k_ref[...] = _dot(xb, w_ref[:, d:2 * d]).astype(BF16)
    v_ref[...] = _dot(xb, w_ref[:, 2 * d:]).astype(BF16)

    f = _dot(xb, wf_ref[...]) + bf_ref[...]
    log_f = (jnp.minimum(f, 0.0) - jnp.log1p(jnp.exp(-jnp.abs(f)))) * LOG2E
    tm = log_f.shape[0]
    tri = (lax.broadcasted_iota(I32, (tm, tm), 0) >= lax.broadcasted_iota(I32, (tm, tm), 1))
    tri = jnp.where(tri, 1.0, 0.0).astype(BF16)
    p0 = log_f.astype(BF16)
    r1 = log_f - p0.astype(F32)
    p1 = r1.astype(BF16)
    p2 = (r1 - p1.astype(F32)).astype(BF16)
    local = _dot(tri, p0) + _dot(tri, p1) + _dot(tri, p2)

    @pl.when(i % tiles_per_seq == 0)
    def _():
        carry_ref[...] = jnp.zeros_like(carry_ref)

    cum = local + carry_ref[...]
    cum_ref[...] = cum
    carry_ref[...] = cum[tm - 1:tm, :]


def _fox_proj(x2, w, wf, bf, seq):
    n, dm = x2.shape
    d = MIX_WIDTH
    row = lambda width: pl.BlockSpec((TM, width), lambda i: (i, 0))
    return pl.pallas_call(
        functools.partial(_fox_proj_kernel, tiles_per_seq=seq // TM),
        grid=(n // TM,),
        in_specs=[row(dm), _const_spec(w.shape), _const_spec(wf.shape), _const_spec(bf.shape)],
        out_specs=[row(d), row(d), row(d), row(LANES)],
        out_shape=[jax.ShapeDtypeStruct((n, d), BF16)] * 3 + [jax.ShapeDtypeStruct((n, LANES), F32)],
        scratch_shapes=[pltpu.VMEM((1, LANES), F32)],
        compiler_params=_params(1),
        name="fox_proj",
    )(x2, w, wf, bf)


def _mla_proj_kernel(x_ref, wd_ref, gq_ref, gkv_ref, wuq_ref, wuk_ref, wuv_ref,
                     cos_ref, slo_ref, shi_ref, q_ref, k_ref, v_ref):
    xb = x_ref[...].astype(BF16)
    cos, slo, shi = cos_ref[...], slo_ref[...], shi_ref[...]
    half = MLA_ROPE // 2
    c = _dot(xb, wd_ref[...])
    cq = _rms_norm(c[:, :Q_LORA], gq_ref[...]).astype(BF16)
    ckv = _rms_norm(c[:, Q_LORA:Q_LORA + KV_LORA], gkv_ref[...]).astype(BF16)
    k_rope = _rope_slab(c[:, Q_LORA + KV_LORA:], cos, slo, shi, half)
    scale = (MLA_NOPE + MLA_ROPE) ** -0.5 * LOG2E
    for h in range(N_HEADS):
        sl = slice(h * LANES, (h + 1) * LANES)
        q = _rope_slab(_dot(cq, wuq_ref[:, sl]), cos, slo, shi, half)
        q_ref[:, sl] = (q * scale).astype(BF16)
        k_ref[:, sl] = (_dot(ckv, wuk_ref[:, sl]) + k_rope).astype(BF16)
    v_ref[...] = _dot(ckv, wuv_ref[...]).astype(BF16)


def _mla_proj(x2, wd, gq, gkv, wuq, wuk, wuv, tabs, seq):
    n, dm = x2.shape
    tiles_per_seq = seq // TM
    row = lambda width: pl.BlockSpec((TM, width), lambda i: (i, 0))
    tab = pl.BlockSpec((TM, LANES), lambda i: (i % tiles_per_seq, 0))
    wide = N_HEADS * LANES
    return pl.pallas_call(
        _mla_proj_kernel,
        grid=(n // TM,),
        in_specs=[row(dm)] + [_const_spec(a.shape) for a in (wd, gq, gkv, wuq, wuk, wuv)] + [tab] * 3,
        out_specs=[row(wide), row(wide), row(MIX_WIDTH)],
        out_shape=[jax.ShapeDtypeStruct((n, wide), BF16), jax.ShapeDtypeStruct((n, wide), BF16),
                   jax.ShapeDtypeStruct((n, MIX_WIDTH), BF16)],
        compiler_params=_params(1),
        name="mla_proj",
    )(x2, wd, gq, gkv, wuq, wuk, wuv, *tabs)


def _prep_mla_w(w_dqkv, w_uq, w_ukv):
    dm = w_dqkv.shape[0]
    zeros = lambda r, c: jnp.zeros((r, c), w_dqkv.dtype)
    k_rope_w = w_dqkv[:, Q_LORA + KV_LORA:]
    wd = jnp.concatenate([w_dqkv[:, :Q_LORA + KV_LORA], zeros(dm, MLA_NOPE), k_rope_w,
                          zeros(dm, LANES - MLA_NOPE - MLA_ROPE)], axis=1)
    uq = w_uq.reshape(Q_LORA, N_HEADS, MLA_NOPE + MLA_ROPE)
    uq = jnp.pad(uq, ((0, 0), (0, 0), (0, LANES - MLA_NOPE - MLA_ROPE))).reshape(Q_LORA, N_HEADS * LANES)
    ukv = w_ukv.reshape(KV_LORA, N_HEADS, MLA_NOPE + MLA_V)
    uk = jnp.pad(ukv[:, :, :MLA_NOPE], ((0, 0), (0, 0), (0, LANES - MLA_NOPE))).reshape(KV_LORA, N_HEADS * LANES)
    uv = ukv[:, :, MLA_NOPE:].reshape(KV_LORA, N_HEADS * MLA_V)
    return wd.astype(BF16), uq.astype(BF16), uk.astype(BF16), uv.astype(BF16)


def _attend_slabs(slabs, n_plain, masked_chunk, qpos, tk):
    m2 = slabs[0][0].shape[0]
    tq = m2 // 2

    def step(c, carries, mask):
        scores = [_dot_nt(k_load(c), q2) for q2, k_load, _, _ in slabs]
        out = []
        for (_, _, vt_load, bias_fn), (m, l, acc), s in zip(slabs, carries, scores):
            if bias_fn is not None:
                s = bias_fn(c) + s
            if mask is not None:
                s = jnp.where(mask, s, NEG)
            m_new = jnp.maximum(m, jnp.max(s, axis=0, keepdims=True))
            alpha = jnp.exp2(m - m_new)
            p = jnp.exp2(s - m_new)
            l = alpha * l + jnp.sum(p, axis=0, keepdims=True)
            acc = alpha * acc + _dot(vt_load(c), p.astype(BF16))
            out.append((m_new, l, acc))
        return tuple(out)

    carries = tuple((jnp.full((1, m2), NEG, F32), jnp.zeros((1, m2), F32), jnp.zeros((LANES, m2), F32))
                    for _ in slabs)
    carries = lax.fori_loop(0, n_plain, lambda c, cr: step(c, cr, None), carries)
    if masked_chunk is not None:
        kpos = masked_chunk * tk + lax.broadcasted_iota(I32, (tk, m2), 0)
        carries = step(masked_chunk, carries, kpos <= qpos)
    row = lax.broadcasted_iota(I32, (LANES, tq), 0)
    outs = []
    for _, l, acc in carries:
        o = acc / l
        outs.append(jnp.where(row < HEAD_DIM, o[:, :tq], o[:, tq:]))
    return outs


def _split_heads(q_slab):
    lane = lax.broadcasted_iota(I32, q_slab.shape, 1)
    zero = jnp.zeros_like(q_slab)
    return jnp.where(lane < HEAD_DIM, q_slab, zero), jnp.where(lane >= HEAD_DIM, q_slab, zero)


def _kv_chunk(ref, c, tk, lanes):
    return ref[0, pl.ds(pl.multiple_of(c * tk, tk), tk), lanes]


def _per_batch_spec(a):
    nd = a.ndim
    return pl.BlockSpec((1,) + a.shape[1:], lambda bi, i: (bi,) + (0,) * (nd - 1),
                        pipeline_mode=pl.Buffered(1))


def _query_positions(row0, tq):
    pos = row0 + lax.broadcasted_iota(I32, (1, tq), 1)
    return jnp.concatenate([pos, pos], axis=1)


def _transpose_values(v, b, s, tk):
    return v.reshape(b, s // tk, tk, v.shape[1]).transpose(0, 1, 3, 2)


def _fox_attn_kernel(q_ref, k_ref, vt_ref, cq_ref, ck_ref, o_ref, *, tq, tk):
    i = pl.program_id(1)
    row0 = i * tq
    diag = row0 // tk
    qpos = _query_positions(row0, tq)
    cq = cq_ref[0]

    def slab(j):
        lanes = slice(j * LANES, (j + 1) * LANES)
        q2 = jnp.concatenate(_split_heads(q_ref[0, :, lanes]), axis=0)
        cq2 = jnp.concatenate([cq[2 * j:2 * j + 1, :], cq[2 * j + 1:2 * j + 2, :]], axis=1)

        def bias_fn(c):
            ck = ck_ref[0, pl.ds(pl.multiple_of(c * tk, tk), tk), :]
            ck2 = jnp.concatenate([jnp.broadcast_to(ck[:, 2 * j:2 * j + 1], (tk, tq)),
                                   jnp.broadcast_to(ck[:, 2 * j + 1:2 * j + 2], (tk, tq))], axis=1)
            return cq2 - ck2

        return (q2, lambda c: _kv_chunk(k_ref, c, tk, lanes), lambda c: vt_ref[0, c, lanes, :], bias_fn)

    for j0 in range(0, N_SLABS, SLAB_GROUP):
        outs = _attend_slabs([slab(j) for j in range(j0, j0 + SLAB_GROUP)], diag, diag, qpos, tk)
        for j, out in zip(range(j0, j0 + SLAB_GROUP), outs):
            o_ref[0, :, j * LANES:(j + 1) * LANES] = out.T.astype(o_ref.dtype)


def _fox_attn(q, k, vt, cum_q, cum_k):
    b, s, d = q.shape
    tq, tk = min(TQ, s), vt.shape[3]
    return pl.pallas_call(
        functools.partial(_fox_attn_kernel, tq=tq, tk=tk),
        grid=(b, s // tq),
        in_specs=[pl.BlockSpec((1, tq, d), lambda bi, i: (bi, i, 0)),
                  _per_batch_spec(k), _per_batch_spec(vt),
                  pl.BlockSpec((1, N_HEADS, tq), lambda bi, i: (bi, 0, i)),
                  _per_batch_spec(cum_k)],
        out_specs=pl.BlockSpec((1, tq, d), lambda bi, i: (bi, i, 0)),
        out_shape=jax.ShapeDtypeStruct((b, s, d), BF16),
        compiler_params=_params(2),
        name="fox_attn",
    )(q, k, vt, cum_q, cum_k)


def _mla_attn_kernel(q_ref, k_ref, vt_ref, o_ref, *, tq, tk):
    i = pl.program_id(1)
    row0 = i * tq
    diag = row0 // tk
    qpos = _query_positions(row0, tq)
    zero = jnp.zeros((tq, LANES), BF16)

    def slab(j):
        pair = slice(2 * j * LANES, (2 * j + 2) * LANES)
        qa = q_ref[0, :, 2 * j * LANES:(2 * j + 1) * LANES]
        qb = q_ref[0, :, (2 * j + 1) * LANES:(2 * j + 2) * LANES]
        q2 = jnp.concatenate([jnp.concatenate([qa, zero], axis=1),
                              jnp.concatenate([zero, qb], axis=1)], axis=0)
        lanes = slice(j * LANES, (j + 1) * LANES)
        return (q2, lambda c: _kv_chunk(k_ref, c, tk, pair), lambda c: vt_ref[0, c, lanes, :], None)

    for j0 in range(0, N_SLABS, SLAB_GROUP):
        outs = _attend_slabs([slab(j) for j in range(j0, j0 + SLAB_GROUP)], diag, diag, qpos, tk)
        for j, out in zip(range(j0, j0 + SLAB_GROUP), outs):
            o_ref[0, :, j * LANES:(j + 1) * LANES] = out.T.astype(o_ref.dtype)


def _mla_attn(q, k, vt):
    b, s, wide = q.shape
    d = vt.shape[2]
    tq, tk = min(TQ, s), vt.shape[3]
    return pl.pallas_call(
        functools.partial(_mla_attn_kernel, tq=tq, tk=tk),
        grid=(b, s // tq),
        in_specs=[pl.BlockSpec((1, tq, wide), lambda bi, i: (bi, i, 0)),
                  _per_batch_spec(k), _per_batch_spec(vt)],
        out_specs=pl.BlockSpec((1, tq, d), lambda bi, i: (bi, i, 0)),
        out_shape=jax.ShapeDtypeStruct((b, s, d), BF16),
        compiler_params=_params(2),
        name="mla_attn",
    )(q, k, vt)


def _sortable_key(score):
    bits = pltpu.bitcast(score + 0.0, I32)
    return jnp.where(bits < 0, bits ^ 0x7FFFFFFF, bits)


def _count_keys(ref, n_chunks, indicator, acc_rows):
    _, tk, tq = ref.shape

    def body(c, acc):
        ind = indicator(ref[c], c)
        for r in range(tk // acc_rows):
            acc = acc + ind[r * acc_rows:(r + 1) * acc_rows]
        return acc

    acc = lax.fori_loop(0, n_chunks, body, jnp.zeros((acc_rows, tq), ref.dtype))
    return jnp.sum(acc.astype(F32), axis=0, keepdims=True)


def _radix_select16(ref, n_chunks, target):
    tq = ref.shape[2]
    one, zero = jnp.int16(1), jnp.int16(0)

    def signed(u):
        return lax.shift_right_arithmetic(lax.shift_left(u ^ 0x8000, 16), 16)

    def bit_step(t, prefix):
        cand = prefix | lax.shift_left(jnp.int32(1), 15 - t)
        cand16 = signed(cand).astype(I16)
        cnt = _count_keys(ref, n_chunks, lambda blk, c: jnp.where(blk >= cand16, one, zero), 64)
        return jnp.where(cnt >= target, cand, prefix)

    return signed(lax.fori_loop(0, 16, bit_step, jnp.zeros((1, tq), I32)))


def _dsa_attn_kernel(q_ref, k_ref, vt_ref, qi_ref, ki_ref, wit_ref, o_ref,
                     qim_ref, key_ref, hi_ref, lo_ref, bias_ref, *, tq, tk, topk):
    i = pl.program_id(1)
    row0 = i * tq
    n_chunks = row0 // tk + 1
    qpos = row0 + lax.broadcasted_iota(I32, (1, tq), 1)
    kiota = lax.broadcasted_iota(I32, (tk, tq), 0)

    for h in range(IDX_HEADS):
        pair = _split_heads(qi_ref[0, :, (h // 2) * LANES:(h // 2 + 1) * LANES])
        qim_ref[h] = pair[h % 2]
    wit = wit_ref[0]

    def score_chunk(c, _):
        ki = _kv_chunk(ki_ref, c, tk, slice(None))
        acc = jnp.zeros((tk, tq), F32)
        for h in range(IDX_HEADS):
            acc = acc + wit[h:h + 1, :] * jnp.maximum(_dot_nt(ki, qim_ref[h]), 0.0)
        key = jnp.where(c * tk + kiota <= qpos, _sortable_key(acc), INT_MIN)
        key_ref[c] = key
        hi_ref[c] = lax.shift_right_arithmetic(key, 16).astype(I16)
        return 0

    lax.fori_loop(0, n_chunks, score_chunk, 0)

    k_f = float(topk)
    one16, zero16 = jnp.int16(1), jnp.int16(0)
    thr_hi = _radix_select16(hi_ref, n_chunks, k_f)
    thr_hi16 = thr_hi.astype(I16)
    n_gt_hi = _count_keys(hi_ref, n_chunks, lambda blk, c: jnp.where(blk > thr_hi16, one16, zero16), 64)

    def low_plane_chunk(c, _):
        key = key_ref[c]
        low = (key & 0xFFFF) - 0x8000
        in_bucket = lax.shift_right_arithmetic(key, 16) == thr_hi
        lo_ref[c] = jnp.where(in_bucket, low, -0x8000).astype(I16)
        return 0

    lax.fori_loop(0, n_chunks, low_plane_chunk, 0)
    thr_lo = _radix_select16(lo_ref, n_chunks, k_f - n_gt_hi)
    thr_lo16 = thr_lo.astype(I16)
    live = thr_hi != -0x8000
    thr = jnp.where(live, thr_hi * 0x10000 + (thr_lo + 0x8000), INT_MIN)
    n_gt = n_gt_hi + _count_keys(lo_ref, n_chunks,
                                 lambda blk, c: jnp.where(blk > thr_lo16, one16, zero16), 64)

    def count(indicator):
        return _count_keys(key_ref, n_chunks, indicator, 32)

    n_eq = count(lambda blk, c: jnp.where(blk == thr, 1, 0))
    need = k_f - n_gt

    def tie_step(t, jmax):
        cand = jmax | lax.shift_left(jnp.int32(1), 12 - t)
        cnt = count(lambda blk, c: jnp.where(
            blk == thr, jnp.where(c * tk + kiota < cand, 1, 0), 0))
        return jnp.where(cnt < need, cand, jmax)

    has_ties = jnp.max(jnp.where(live, n_eq - need, 0.0)) > 0.0
    jmax = lax.cond(has_ties,
                    lambda: lax.fori_loop(0, 13, tie_step, jnp.zeros((1, tq), I32)),
                    lambda: jnp.full((1, tq), 2 ** 13 - 1, I32))
    jmax = jnp.where(live, jmax, -1)

    def bias_chunk(c, _):
        blk = key_ref[c]
        tie_ok = jnp.where(c * tk + kiota <= jmax, 0.0, NEG)
        bias_ref[c] = jnp.where(blk > thr, 0.0, jnp.where(blk == thr, tie_ok, NEG))
        return 0

    lax.fori_loop(0, n_chunks, bias_chunk, 0)

    def bias_fn(c):
        b = bias_ref[c]
        return jnp.concatenate([b, b], axis=1)

    def slab(j):
        lanes = slice(j * LANES, (j + 1) * LANES)
        q2 = jnp.concatenate(_split_heads(q_ref[0, :, lanes]), axis=0)
        return (q2, lambda c: _kv_chunk(k_ref, c, tk, lanes), lambda c: vt_ref[0, c, lanes, :], bias_fn)

    for j0 in range(0, N_SLABS, SLAB_GROUP):
        outs = _attend_slabs([slab(j) for j in range(j0, j0 + SLAB_GROUP)], n_chunks, None, None, tk)
        for j, out in zip(range(j0, j0 + SLAB_GROUP), outs):
            o_ref[0, :, j * LANES:(j + 1) * LANES] = out.T.astype(o_ref.dtype)


def _dsa_attn(q, k, vt, qi, ki, wit):
    b, s, d = q.shape
    tq, tk = min(TQ, s), vt.shape[3]
    topk = min(TOPK_MAX, s // 4)
    per_q = lambda width: pl.BlockSpec((1, tq, width), lambda bi, i: (bi, i, 0))
    return pl.pallas_call(
        functools.partial(_dsa_attn_kernel, tq=tq, tk=tk, topk=topk),
        grid=(b, s // tq),
        in_specs=[per_q(d), _per_batch_spec(k), _per_batch_spec(vt),
                  per_q(qi.shape[2]), _per_batch_spec(ki),
                  pl.BlockSpec((1, IDX_HEADS, tq), lambda bi, i: (bi, 0, i))],
        out_specs=per_q(d),
        out_shape=jax.ShapeDtypeStruct((b, s, d), BF16),
        scratch_shapes=[pltpu.VMEM((IDX_HEADS, tq, LANES), BF16),
                        pltpu.VMEM((s // tk, tk, tq), I32),
                        pltpu.VMEM((s // tk, tk, tq), I16),
                        pltpu.VMEM((s // tk, tk, tq), I16),
                        pltpu.VMEM((s // tk, tk, tq), F32)],
        compiler_params=_params(2),
        name="dsa_attn",
    )(q, k, vt, qi, ki, wit)


def _dsa_mixer(x2, b, s, w_in, tabs):
    q, k, v, qi, ki, wi = _dsa_proj(x2, _prep_dsa_w(w_in), tabs, s)
    r3 = lambda a: a.reshape(b, s, a.shape[1])
    wit = r3(wi)[:, :, :IDX_HEADS].transpose(0, 2, 1)
    o = _dsa_attn(r3(q), r3(k), _transpose_values(v, b, s, min(TK, s)), r3(qi), r3(ki), wit)
    return o.reshape(b * s, MIX_WIDTH)


def _fox_mixer(x2, b, s, w_in, b_f):
    d = MIX_WIDTH
    wf = jnp.pad(w_in[:, 3 * d:], ((0, 0), (0, LANES - N_HEADS))).astype(BF16)
    bf = jnp.pad(b_f, (0, LANES - N_HEADS)).reshape(1, LANES)
    q, k, v, cum = _fox_proj(x2, w_in[:, :3 * d].astype(BF16), wf, bf, s)
    r3 = lambda a: a.reshape(b, s, a.shape[1])
    cum_q = r3(cum)[:, :, :N_HEADS].transpose(0, 2, 1)
    o = _fox_attn(r3(q), r3(k), _transpose_values(v, b, s, min(TK, s)), cum_q, r3(cum))
    return o.reshape(b * s, d)


def _mla_mixer(x2, b, s, w_dqkv, gq, w_uq, gkv, w_ukv, tabs):
    wd, uq, uk, uv = _prep_mla_w(w_dqkv, w_uq, w_ukv)
    q, k, v = _mla_proj(x2, wd, gq.reshape(1, -1), gkv.reshape(1, -1), uq, uk, uv, tabs, s)
    r3 = lambda a: a.reshape(b, s, a.shape[1])
    o = _mla_attn(r3(q), r3(k), _transpose_values(v, b, s, min(TK, s)))
    return o.reshape(b * s, MIX_WIDTH)


def kernel(x, ffn1_w13, ffn1_w2, ffn2_w13, ffn2_w2, ln_g, ln_b, w_out, dsa_w_in, fox_w_in, fox_b_f,
           mla_w_dqkv, mla_q_norm_g, mla_w_uq, mla_kv_norm_g, mla_w_ukv):
    b, s, dm = x.shape
    x2 = x.reshape(b * s, dm)
    tabs_p = _rope_tables(s, ROT_DIM, 0, HEAD_DIM)
    tabs_m = _rope_tables(s, MLA_ROPE, MLA_NOPE, LANES)
    for i in range(DEPTH):
        g = lambda r, i=i: ln_g[i, r].reshape(1, dm)
        be = lambda r, i=i: ln_b[i, r].reshape(1, dm)
        x2 = _ffn_ln(x2, *_prep_ffn(ffn1_w13[i], ffn1_w2[i]), g(0), be(0))
        kind, j = i % N_MIXERS, i // N_MIXERS
        if kind == 0:
            o = _dsa_mixer(x2, b, s, dsa_w_in[j], tabs_p)
        elif kind == 1:
            o = _fox_mixer(x2, b, s, fox_w_in[j], fox_b_f[j])
        else:
            o = _mla_mixer(x2, b, s, mla_w_dqkv[j], mla_q_norm_g[j], mla_w_uq[j],
                           mla_kv_norm_g[j], mla_w_ukv[j], tabs_m)
        x2 = _outproj_ln(x2, o, w_out[i].astype(BF16), g(1), be(1))
        x2 = _ffn_ln(x2, *_prep_ffn(ffn2_w13[i], ffn2_w2[i]), g(2), be(2))
    return x2.reshape(b, s, dm)
```

```python
import functools

import jax
import jax.numpy as jnp
from jax import lax
from jax.experimental import pallas as pl
from jax.experimental.pallas import tpu as pltpu

F32 = jnp.float32
BF16 = jnp.bfloat16
I32 = jnp.int32
I16 = jnp.int16

D_MODEL = 1024
DEPTH = 4
N_MIXERS = 3
HEAD_DIM = 64
N_HEADS = D_MODEL // HEAD_DIM
MIX_WIDTH = N_HEADS * HEAD_DIM
ROT_DIM = HEAD_DIM // 4
ROPE_THETA = 500000.0
IDX_HEADS = 8
IDX_DIM = 64
TOPK_MAX = 256
MLA_NOPE = 64
MLA_ROPE = 32
MLA_V = 64
Q_LORA = 384
KV_LORA = 256
D_FF = 2816
ALPHA = (2.0 * DEPTH) ** 0.25
LN_EPS = 1e-5
RMS_EPS = 1e-6

LANES = 128
N_SLABS = MIX_WIDTH // LANES
VT_ROWS = LANES + 16
LOG2E = 1.4426950408889634
NEG = -1e30
INT_MIN = -2147483648

TM = 512
TF = 256
TQ = 256
TK = 512
VMEM_LIMIT = 56 * 1024 * 1024


def _params(n_axes, vmem=VMEM_LIMIT):
    return pltpu.CompilerParams(dimension_semantics=("arbitrary",) * n_axes, vmem_limit_bytes=vmem)


def _const_spec(shape):
    nd = len(shape)
    return pl.BlockSpec(shape, lambda *_: (0,) * nd)


def _dot(a, b):
    return jnp.dot(a, b, preferred_element_type=F32)


def _dot_nt(a, b):
    return lax.dot_general(a, b, (((1,), (1,)), ((), ())), preferred_element_type=F32)


def _layer_norm(z, g, b):
    mu = jnp.mean(z, axis=-1, keepdims=True)
    d = z - mu
    var = jnp.mean(d * d, axis=-1, keepdims=True)
    return d * lax.rsqrt(var + LN_EPS) * g + b


def _rms_norm(z, g):
    ms = jnp.mean(z * z, axis=-1, keepdims=True)
    return z * lax.rsqrt(ms + RMS_EPS) * g


def _rope_slab(y, cos, sin_lo, sin_hi, half):
    return y * cos + pltpu.roll(y, half, 1) * sin_hi + pltpu.roll(y, LANES - half, 1) * sin_lo


def _ffn_ln_kernel(x_ref, wgu_ref, w2_ref, g_ref, b_ref, o_ref):
    x = x_ref[...]
    xb = x.astype(BF16)
    n_chunks, _, two_tf = wgu_ref.shape
    tf = two_tf // 2
    acc = jnp.zeros(x.shape, F32)
    for c in range(n_chunks):
        gu = _dot(xb, wgu_ref[c])
        gate, up = gu[:, :tf], gu[:, tf:]
        h = gate * jax.nn.sigmoid(gate) * up
        acc = acc + _dot(h.astype(BF16), w2_ref[c])
    o_ref[...] = _layer_norm(ALPHA * x + 0.5 * acc, g_ref[...], b_ref[...])


def _ffn_ln(x2, wgu, w2, g, b):
    n, d = x2.shape
    return pl.pallas_call(
        _ffn_ln_kernel,
        grid=(n // TM,),
        in_specs=[pl.BlockSpec((TM, d), lambda i: (i, 0)),
                  _const_spec(wgu.shape), _const_spec(w2.shape),
                  _const_spec(g.shape), _const_spec(b.shape)],
        out_specs=pl.BlockSpec((TM, d), lambda i: (i, 0)),
        out_shape=jax.ShapeDtypeStruct((n, d), F32),
        compiler_params=_params(1),
        name="ffn_ln",
    )(x2, wgu, w2, g, b)


def _prep_ffn(w13, w2):
    d, two_ff = w13.shape
    ff = two_ff // 2
    nc = ff // TF
    gate = w13[:, :ff].reshape(d, nc, 1, TF)
    up = w13[:, ff:].reshape(d, nc, 1, TF)
    wgu = jnp.concatenate([gate, up], axis=2).transpose(1, 0, 2, 3).reshape(nc, d, 2 * TF)
    return wgu.astype(BF16), w2.reshape(nc, TF, d).astype(BF16)


def _outproj_ln_kernel(x_ref, o_ref, w_ref, g_ref, b_ref, out_ref):
    y = _dot(o_ref[...], w_ref[...])
    out_ref[...] = _layer_norm(ALPHA * x_ref[...] + y, g_ref[...], b_ref[...])


def _outproj_ln(x2, o2, w, g, b):
    n, d = x2.shape
    return pl.pallas_call(
        _outproj_ln_kernel,
        grid=(n // TM,),
        in_specs=[pl.BlockSpec((TM, d), lambda i: (i, 0)),
                  pl.BlockSpec((TM, o2.shape[1]), lambda i: (i, 0)),
                  _const_spec(w.shape), _const_spec(g.shape), _const_spec(b.shape)],
        out_specs=pl.BlockSpec((TM, d), lambda i: (i, 0)),
        out_shape=jax.ShapeDtypeStruct((n, d), F32),
        compiler_params=_params(1),
        name="outproj_ln",
    )(x2, o2, w, g, b)


def _rope_tables(seq, rot, lane_of_rot0, period):
    half = rot // 2
    inv = ROPE_THETA ** (-jnp.arange(0, rot, 2, dtype=F32) / rot)
    ang = jnp.arange(seq, dtype=F32)[:, None] * inv[None, :]
    cos_h, sin_h = jnp.cos(ang), jnp.sin(ang)
    r = (jnp.arange(LANES) % period) - lane_of_rot0
    in_lo = (r >= 0) & (r < half)
    in_hi = (r >= half) & (r < rot)
    idx = jnp.clip(jnp.where(in_hi, r - half, r), 0, half - 1)
    cos = jnp.where((in_lo | in_hi)[None, :], cos_h[:, idx], 1.0)
    sin_lo = jnp.where(in_lo[None, :], -sin_h[:, idx], 0.0)
    sin_hi = jnp.where(in_hi[None, :], sin_h[:, idx], 0.0)
    return cos, sin_lo, sin_hi


def _dsa_proj_kernel(x_ref, w_ref, cos_ref, slo_ref, shi_ref,
                     q_ref, k_ref, v_ref, qi_ref, ki_ref, wi_ref):
    xb = x_ref[...].astype(BF16)
    cos, slo, shi = cos_ref[...], slo_ref[...], shi_ref[...]
    half = ROT_DIM // 2
    d = MIX_WIDTH
    di = IDX_HEADS * IDX_DIM

    def roped(col0, width, scale, out_ref):
        y = _dot(xb, w_ref[:, col0:col0 + width])
        for j in range(width // LANES):
            s = _rope_slab(y[:, j * LANES:(j + 1) * LANES], cos, slo, shi, half)
            if scale != 1.0:
                s = s * scale
            out_ref[:, j * LANES:(j + 1) * LANES] = s.astype(out_ref.dtype)

    roped(0, d, HEAD_DIM ** -0.5 * LOG2E, q_ref)
    roped(d, d, 1.0, k_ref)
    v_ref[...] = _dot(xb, w_ref[:, 2 * d:3 * d]).astype(BF16)
    roped(3 * d, di, IDX_DIM ** -0.5, qi_ref)
    roped(3 * d + di, LANES, 1.0, ki_ref)
    wi_ref[...] = _dot(xb, w_ref[:, 3 * d + di + LANES:]) * (IDX_HEADS ** -0.5)


def _dsa_proj(x2, w, tabs, seq):
    n, dm = x2.shape
    d, di = MIX_WIDTH, IDX_HEADS * IDX_DIM
    tiles_per_seq = seq // TM
    row = lambda width: pl.BlockSpec((TM, width), lambda i: (i, 0))
    tab = pl.BlockSpec((TM, LANES), lambda i: (i % tiles_per_seq, 0))
    return pl.pallas_call(
        _dsa_proj_kernel,
        grid=(n // TM,),
        in_specs=[row(dm), _const_spec(w.shape), tab, tab, tab],
        out_specs=[row(d), row(d), row(d), row(di), row(LANES), row(LANES)],
        out_shape=[jax.ShapeDtypeStruct((n, d), BF16), jax.ShapeDtypeStruct((n, d), BF16),
                   jax.ShapeDtypeStruct((n, d), BF16), jax.ShapeDtypeStruct((n, di), BF16),
                   jax.ShapeDtypeStruct((n, LANES), BF16), jax.ShapeDtypeStruct((n, LANES), F32)],
        compiler_params=_params(1),
        name="dsa_proj",
    )(x2, w, *tabs)


def _prep_dsa_w(w_in):
    d, di = MIX_WIDTH, IDX_HEADS * IDX_DIM
    ki = w_in[:, 3 * d + di:3 * d + di + IDX_DIM]
    wi = w_in[:, 3 * d + di + IDX_DIM:]
    pad = jnp.zeros((w_in.shape[0], LANES - IDX_HEADS), w_in.dtype)
    return jnp.concatenate([w_in[:, :3 * d + di], ki, ki, wi, pad], axis=1).astype(BF16)


def _fox_proj_kernel(x_ref, w_ref, wf_ref, bf_ref, q_ref, k_ref, v_ref, cum_ref, carry_ref,
                     *, tiles_per_seq):
    i = pl.program_id(0)
    xb = x_ref[...].astype(BF16)
    d = MIX_WIDTH
    q_ref[...] = (_dot(xb, w_ref[:, :d]) * (HEAD_DIM ** -0.5 * LOG2E)).astype(BF16)
    k_ref[...] = _dot(xb, w_ref[:, d:2 * d]).astype(BF16)
    v_ref[...] = _dot(xb, w_ref[:, 2 * d:]).astype(BF16)

    f = _dot(xb, wf_ref[...]) + bf_ref[...]
    log_f = (jnp.minimum(f, 0.0) - jnp.log1p(jnp.exp(-jnp.abs(f)))) * LOG2E
    tm = log_f.shape[0]
    tri = (lax.broadcasted_iota(I32, (tm, tm), 0) >= lax.broadcasted_iota(I32, (tm, tm), 1))
    tri = jnp.where(tri, 1.0, 0.0).astype(BF16)
    p0 = log_f.astype(BF16)
    r1 = log_f - p0.astype(F32)
    p1 = r1.astype(BF16)
    p2 = (r1 - p1.astype(F32)).astype(BF16)
    local = _dot(tri, p0) + _dot(tri, p1) + _dot(tri, p2)

    @pl.when(i % tiles_per_seq == 0)
    def _():
        carry_ref[...] = jnp.zeros_like(carry_ref)

    cum = local + carry_ref[...]
    cum_ref[...] = cum
    carry_ref[...] = cum[tm - 1:tm, :]


def _fox_proj(x2, w, wf, bf, seq):
    n, dm = x2.shape
    d = MIX_WIDTH
    row = lambda width: pl.BlockSpec((TM, width), lambda i: (i, 0))
    return pl.pallas_call(
        functools.partial(_fox_proj_kernel, tiles_per_seq=seq // TM),
        grid=(n // TM,),
        in_specs=[row(dm), _const_spec(w.shape), _const_spec(wf.shape), _const_spec(bf.shape)],
        out_specs=[row(d), row(d), row(d), row(LANES)],
        out_shape=[jax.ShapeDtypeStruct((n, d), BF16)] * 3 + [jax.ShapeDtypeStruct((n, LANES), F32)],
        scratch_shapes=[pltpu.VMEM((1, LANES), F32)],
        compiler_params=_params(1),
        name="fox_proj",
    )(x2, w, wf, bf)


def _mla_proj_kernel(x_ref, wd_ref, gq_ref, gkv_ref, wuq_ref, wuk_ref, wuv_ref,
                     cos_ref, slo_ref, shi_ref, q_ref, k_ref, v_ref):
    xb = x_ref[...].astype(BF16)
    cos, slo, shi = cos_ref[...], slo_ref[...], shi_ref[...]
    half = MLA_ROPE // 2
    c = _dot(xb, wd_ref[...])
    cq = _rms_norm(c[:, :Q_LORA], gq_ref[...]).astype(BF16)
    ckv = _rms_norm(c[:, Q_LORA:Q_LORA + KV_LORA], gkv_ref[...]).astype(BF16)
    k_rope = _rope_slab(c[:, Q_LORA + KV_LORA:], cos, slo, shi, half)
    scale = (MLA_NOPE + MLA_ROPE) ** -0.5 * LOG2E
    for h in range(N_HEADS):
        sl = slice(h * LANES, (h + 1) * LANES)
        q = _rope_slab(_dot(cq, wuq_ref[:, sl]), cos, slo, shi, half)
        q_ref[:, sl] = (q * scale).astype(BF16)
        k_ref[:, sl] = (_dot(ckv, wuk_ref[:, sl]) + k_rope).astype(BF16)
    v_ref[...] = _dot(ckv, wuv_ref[...]).astype(BF16)


def _mla_proj(x2, wd, gq, gkv, wuq, wuk, wuv, tabs, seq):
    n, dm = x2.shape
    tiles_per_seq = seq // TM
    row = lambda width: pl.BlockSpec((TM, width), lambda i: (i, 0))
    tab = pl.BlockSpec((TM, LANES), lambda i: (i % tiles_per_seq, 0))
    wide = N_HEADS * LANES
    return pl.pallas_call(
        _mla_proj_kernel,
        grid=(n // TM,),
        in_specs=[row(dm)] + [_const_spec(a.shape) for a in (wd, gq, gkv, wuq, wuk, wuv)] + [tab] * 3,
        out_specs=[row(wide), row(wide), row(MIX_WIDTH)],
        out_shape=[jax.ShapeDtypeStruct((n, wide), BF16), jax.ShapeDtypeStruct((n, wide), BF16),
                   jax.ShapeDtypeStruct((n, MIX_WIDTH), BF16)],
        compiler_params=_params(1),
        name="mla_proj",
    )(x2, wd, gq, gkv, wuq, wuk, wuv, *tabs)


def _prep_mla_w(w_dqkv, w_uq, w_ukv):
    dm = w_dqkv.shape[0]
    zeros = lambda r, c: jnp.zeros((r, c), w_dqkv.dtype)
    k_rope_w = w_dqkv[:, Q_LORA + KV_LORA:]
    wd = jnp.concatenate([w_dqkv[:, :Q_LORA + KV_LORA], zeros(dm, MLA_NOPE), k_rope_w,
                          zeros(dm, LANES - MLA_NOPE - MLA_ROPE)], axis=1)
    uq = w_uq.reshape(Q_LORA, N_HEADS, MLA_NOPE + MLA_ROPE)
    uq = jnp.pad(uq, ((0, 0), (0, 0), (0, LANES - MLA_NOPE - MLA_ROPE))).reshape(Q_LORA, N_HEADS * LANES)
    ukv = w_ukv.reshape(KV_LORA, N_HEADS, MLA_NOPE + MLA_V)
    uk = jnp.pad(ukv[:, :, :MLA_NOPE], ((0, 0), (0, 0), (0, LANES - MLA_NOPE))).reshape(KV_LORA, N_HEADS * LANES)
    uv = ukv[:, :, MLA_NOPE:].reshape(KV_LORA, N_HEADS * MLA_V)
    return wd.astype(BF16), uq.astype(BF16), uk.astype(BF16), uv.astype(BF16)


def _attend_slabs(slabs, n_plain, causal, qpos, tk, one_pass):
    m2 = slabs[0][0].shape[0]
    tq = m2 // 2
    kiota = lax.broadcasted_iota(I32, (tk, m2), 0)

    def step(c, carries, masked, stale):
        scores = [_dot_nt(k_load(c), q2) for q2, k_load, _, _ in slabs]
        mask = (c * tk + kiota <= qpos) if masked else None
        out = []
        for (_, _, vt_load, bias_fn), (m, acc), s in zip(slabs, carries, scores):
            if bias_fn is not None:
                s = bias_fn(c) + s
            if mask is not None:
                s = jnp.where(mask, s, NEG)
            m_new = jnp.maximum(m, jnp.max(s, axis=0, keepdims=True))
            alpha = jnp.exp2(m - m_new)
            if stale:
                acc = (acc + _dot(vt_load(c), jnp.exp2(s - m).astype(BF16))) * alpha
            else:
                acc = alpha * acc + _dot(vt_load(c), jnp.exp2(s - m_new).astype(BF16))
            out.append((m_new, acc))
        return tuple(out)

    carries = tuple((jnp.full((1, m2), NEG, F32), jnp.zeros((VT_ROWS, m2), F32)) for _ in slabs)
    if one_pass:
        carries = step(0, carries, causal, False)
        carries = lax.fori_loop(1, n_plain, lambda c, cr: step(c, cr, False, True), carries)
        if causal:
            last = jnp.where(n_plain > 0, n_plain + 1, n_plain)
            carries = lax.fori_loop(n_plain, last, lambda c, cr: step(c, cr, True, True), carries)
    else:
        carries = lax.fori_loop(0, n_plain, lambda c, cr: step(c, cr, False, False), carries)
        if causal:
            carries = step(n_plain, carries, True, False)
    row = lax.broadcasted_iota(I32, (LANES, tq), 0)
    outs = []
    for _, acc in carries:
        o = acc[:LANES] / acc[LANES:LANES + 1]
        outs.append(jnp.where(row < HEAD_DIM, o[:, :tq], o[:, tq:]))
    return outs


def _attend_to(o_ref, slabs, n_plain, causal, qpos, tk):
    def run(one_pass):
        outs = _attend_slabs(slabs, n_plain, causal, qpos, tk, one_pass)
        for j, out in enumerate(outs):
            o_ref[0, :, j * LANES:(j + 1) * LANES] = out.T.astype(o_ref.dtype)
        return outs

    outs = run(True)
    total = functools.reduce(lambda a, b: a + b, [jnp.sum(o, axis=0, keepdims=True) for o in outs])
    overflowed = jnp.max(jnp.where(jnp.isfinite(total), 0.0, 1.0)) > 0.0

    @pl.when(overflowed)
    def _():
        run(False)


def _split_heads(q_slab):
    lane = lax.broadcasted_iota(I32, q_slab.shape, 1)
    zero = jnp.zeros_like(q_slab)
    return jnp.where(lane < HEAD_DIM, q_slab, zero), jnp.where(lane >= HEAD_DIM, q_slab, zero)


def _kv_chunk(ref, c, tk, lanes):
    return ref[0, pl.ds(pl.multiple_of(c * tk, tk), tk), lanes]


def _per_batch_spec(a):
    nd = a.ndim
    return pl.BlockSpec((1,) + a.shape[1:], lambda bi, i: (bi,) + (0,) * (nd - 1),
                        pipeline_mode=pl.Buffered(1))


def _query_positions(row0, tq):
    pos = row0 + lax.broadcasted_iota(I32, (1, tq), 1)
    return jnp.concatenate([pos, pos], axis=1)


def _transpose_values(v, b, s, tk):
    vt = v.reshape(b, s // tk, tk, N_SLABS, LANES).transpose(0, 1, 3, 4, 2)
    extra = jnp.zeros((b, s // tk, N_SLABS, VT_ROWS - LANES, tk), v.dtype).at[:, :, :, 0, :].set(1)
    return jnp.concatenate([vt, extra], axis=3).reshape(b, s // tk, N_SLABS * VT_ROWS, tk)


def _fox_attn_kernel(q_ref, k_ref, vt_ref, cq_ref, ck_ref, o_ref, *, tq, tk):
    i = pl.program_id(1)
    row0 = i * tq
    diag = row0 // tk
    qpos = _query_positions(row0, tq)
    cq = cq_ref[0]

    def slab(j):
        lanes = slice(j * LANES, (j + 1) * LANES)
        q2 = jnp.concatenate(_split_heads(q_ref[0, :, lanes]), axis=0)
        cq2 = jnp.concatenate([cq[2 * j:2 * j + 1, :], cq[2 * j + 1:2 * j + 2, :]], axis=1)

        def bias_fn(c):
            ck = ck_ref[0, pl.ds(pl.multiple_of(c * tk, tk), tk), :]
            ck2 = jnp.concatenate([jnp.broadcast_to(ck[:, 2 * j:2 * j + 1], (tk, tq)),
                                   jnp.broadcast_to(ck[:, 2 * j + 1:2 * j + 2], (tk, tq))], axis=1)
            return cq2 - ck2

        return (q2, lambda c: _kv_chunk(k_ref, c, tk, lanes), lambda c: vt_ref[0, c, j * VT_ROWS:(j + 1) * VT_ROWS, :], bias_fn)

    _attend_to(o_ref, [slab(j) for j in range(N_SLABS)], diag, True, qpos, tk)


def _fox_attn(q, k, vt, cum_q, cum_k):
    b, s, d = q.shape
    tq, tk = min(TQ, s), vt.shape[3]
    return pl.pallas_call(
        functools.partial(_fox_attn_kernel, tq=tq, tk=tk),
        grid=(b, s // tq),
        in_specs=[pl.BlockSpec((1, tq, d), lambda bi, i: (bi, i, 0)),
                  _per_batch_spec(k), _per_batch_spec(vt),
                  pl.BlockSpec((1, N_HEADS, tq), lambda bi, i: (bi, 0, i)),
                  _per_batch_spec(cum_k)],
        out_specs=pl.BlockSpec((1, tq, d), lambda bi, i: (bi, i, 0)),
        out_shape=jax.ShapeDtypeStruct((b, s, d), BF16),
        compiler_params=_params(2),
        name="fox_attn",
    )(q, k, vt, cum_q, cum_k)


def _mla_attn_kernel(q_ref, k_ref, vt_ref, o_ref, *, tq, tk):
    i = pl.program_id(1)
    row0 = i * tq
    diag = row0 // tk
    qpos = _query_positions(row0, tq)
    zero = jnp.zeros((tq, LANES), BF16)

    def slab(j):
        pair = slice(2 * j * LANES, (2 * j + 2) * LANES)
        qa = q_ref[0, :, 2 * j * LANES:(2 * j + 1) * LANES]
        qb = q_ref[0, :, (2 * j + 1) * LANES:(2 * j + 2) * LANES]
        q2 = jnp.concatenate([jnp.concatenate([qa, zero], axis=1),
                              jnp.concatenate([zero, qb], axis=1)], axis=0)
        lanes = slice(j * LANES, (j + 1) * LANES)
        return (q2, lambda c: _kv_chunk(k_ref, c, tk, pair), lambda c: vt_ref[0, c, j * VT_ROWS:(j + 1) * VT_ROWS, :], None)

    _attend_to(o_ref, [slab(j) for j in range(N_SLABS)], diag, True, qpos, tk)


def _mla_attn(q, k, vt):
    b, s, wide = q.shape
    d = MIX_WIDTH
    tq, tk = min(TQ, s), vt.shape[3]
    return pl.pallas_call(
        functools.partial(_mla_attn_kernel, tq=tq, tk=tk),
        grid=(b, s // tq),
        in_specs=[pl.BlockSpec((1, tq, wide), lambda bi, i: (bi, i, 0)),
                  _per_batch_spec(k), _per_batch_spec(vt)],
        out_specs=pl.BlockSpec((1, tq, d), lambda bi, i: (bi, i, 0)),
        out_shape=jax.ShapeDtypeStruct((b, s, d), BF16),
        compiler_params=_params(2),
        name="mla_attn",
    )(q, k, vt)


def _sortable_key(score):
    bits = pltpu.bitcast(score + 0.0, I32)
    return jnp.where(bits < 0, bits ^ 0x7FFFFFFF, bits)


def _count_keys(ref, n_chunks, indicator, acc_rows):
    _, tk, tq = ref.shape

    def body(c, acc):
        ind = indicator(ref[c], c)
        for r in range(tk // acc_rows):
            acc = acc + ind[r * acc_rows:(r + 1) * acc_rows]
        return acc

    acc = lax.fori_loop(0, n_chunks, body, jnp.zeros((acc_rows, tq), ref.dtype))
    return jnp.sum(acc.astype(F32), axis=0, keepdims=True)


def _radix_select16(ref, n_chunks, target):
    tq = ref.shape[2]
    one, zero = jnp.int16(1), jnp.int16(0)

    def signed(u):
        return lax.shift_right_arithmetic(lax.shift_left(u ^ 0x8000, 16), 16)

    def bit_step(t, prefix):
        cand = prefix | lax.shift_left(jnp.int32(1), 15 - t)
        cand16 = signed(cand).astype(I16)
        cnt = _count_keys(ref, n_chunks, lambda blk, c: jnp.where(blk >= cand16, one, zero), 64)
        return jnp.where(cnt >= target, cand, prefix)

    return signed(lax.fori_loop(0, 16, bit_step, jnp.zeros((1, tq), I32)))


def _dsa_attn_kernel(q_ref, k_ref, vt_ref, qi_ref, ki_ref, wit_ref, o_ref,
                     qim_ref, key_ref, hi_ref, lo_ref, bias_ref, *, tq, tk, topk):
    i = pl.program_id(1)
    row0 = i * tq
    n_chunks = row0 // tk + 1
    qpos = row0 + lax.broadcasted_iota(I32, (1, tq), 1)
    kiota = lax.broadcasted_iota(I32, (tk, tq), 0)

    for h in range(IDX_HEADS):
        pair = _split_heads(qi_ref[0, :, (h // 2) * LANES:(h // 2 + 1) * LANES])
        qim_ref[h] = pair[h % 2]
    wit = wit_ref[0]

    def score_chunk(c, _):
        ki = _kv_chunk(ki_ref, c, tk, slice(None))
        acc = jnp.zeros((tk, tq), F32)
        for h in range(IDX_HEADS):
            acc = acc + wit[h:h + 1, :] * jnp.maximum(_dot_nt(ki, qim_ref[h]), 0.0)
        key = jnp.where(c * tk + kiota <= qpos, _sortable_key(acc), INT_MIN)
        key_ref[c] = key
        hi_ref[c] = lax.shift_right_arithmetic(key, 16).astype(I16)
        return 0

    lax.fori_loop(0, n_chunks, score_chunk, 0)

    k_f = float(topk)
    one16, zero16 = jnp.int16(1), jnp.int16(0)
    thr_hi = _radix_select16(hi_ref, n_chunks, k_f)
    thr_hi16 = thr_hi.astype(I16)
    n_gt_hi = _count_keys(hi_ref, n_chunks, lambda blk, c: jnp.where(blk > thr_hi16, one16, zero16), 64)

    def low_plane_chunk(c, _):
        key = key_ref[c]
        low = (key & 0xFFFF) - 0x8000
        in_bucket = lax.shift_right_arithmetic(key, 16) == thr_hi
        lo_ref[c] = jnp.where(in_bucket, low, -0x8000).astype(I16)
        return 0

    lax.fori_loop(0, n_chunks, low_plane_chunk, 0)
    thr_lo = _radix_select16(lo_ref, n_chunks, k_f - n_gt_hi)
    thr_lo16 = thr_lo.astype(I16)
    live = thr_hi != -0x8000
    thr = jnp.where(live, thr_hi * 0x10000 + (thr_lo + 0x8000), INT_MIN)
    n_gt = n_gt_hi + _count_keys(lo_ref, n_chunks,
                                 lambda blk, c: jnp.where(blk > thr_lo16, one16, zero16), 64)

    def count(indicator):
        return _count_keys(key_ref, n_chunks, indicator, 32)

    n_eq = count(lambda blk, c: jnp.where(blk == thr, 1, 0))
    need = k_f - n_gt

    def tie_step(t, jmax):
        cand = jmax | lax.shift_left(jnp.int32(1), 12 - t)
        cnt = count(lambda blk, c: jnp.where(
            blk == thr, jnp.where(c * tk + kiota < cand, 1, 0), 0))
        return jnp.where(cnt < need, cand, jmax)

    has_ties = jnp.max(jnp.where(live, n_eq - need, 0.0)) > 0.0
    jmax = lax.cond(has_ties,
                    lambda: lax.fori_loop(0, 13, tie_step, jnp.zeros((1, tq), I32)),
                    lambda: jnp.full((1, tq), 2 ** 13 - 1, I32))
    jmax = jnp.where(live, jmax, -1)

    def bias_chunk(c, _):
        blk = key_ref[c]
        tie_ok = jnp.where(c * tk + kiota <= jmax, 0.0, NEG)
        bias_ref[c] = jnp.where(blk > thr, 0.0, jnp.where(blk == thr, tie_ok, NEG))
        return 0

    lax.fori_loop(0, n_chunks, bias_chunk, 0)

    def bias_fn(c):
        b = bias_ref[c]
        return jnp.concatenate([b, b], axis=1)

    def slab(j):
        lanes = slice(j * LANES, (j + 1) * LANES)
        q2 = jnp.concatenate(_split_heads(q_ref[0, :, lanes]), axis=0)
        return (q2, lambda c: _kv_chunk(k_ref, c, tk, lanes), lambda c: vt_ref[0, c, j * VT_ROWS:(j + 1) * VT_ROWS, :], bias_fn)

    _attend_to(o_ref, [slab(j) for j in range(N_SLABS)], n_chunks, False, None, tk)


def _dsa_attn(q, k, vt, qi, ki, wit):
    b, s, d = q.shape
    tq, tk = min(TQ, s), vt.shape[3]
    topk = min(TOPK_MAX, s // 4)
    per_q = lambda width: pl.BlockSpec((1, tq, width), lambda bi, i: (bi, i, 0))
    return pl.pallas_call(
        functools.partial(_dsa_attn_kernel, tq=tq, tk=tk, topk=topk),
        grid=(b, s // tq),
        in_specs=[per_q(d), _per_batch_spec(k), _per_batch_spec(vt),
                  per_q(qi.shape[2]), _per_batch_spec(ki),
                  pl.BlockSpec((1, IDX_HEADS, tq), lambda bi, i: (bi, 0, i))],
        out_specs=per_q(d),
        out_shape=jax.ShapeDtypeStruct((b, s, d), BF16),
        scratch_shapes=[pltpu.VMEM((IDX_HEADS, tq, LANES), BF16),
                        pltpu.VMEM((s // tk, tk, tq), I32),
                        pltpu.VMEM((s // tk, tk, tq), I16),
                        pltpu.VMEM((s // tk, tk, tq), I16),
                        pltpu.VMEM((s // tk, tk, tq), F32)],
        compiler_params=_params(2),
        name="dsa_attn",
    )(q, k, vt, qi, ki, wit)


def _dsa_mixer(x2, b, s, w_in, tabs):
    q, k, v, qi, ki, wi = _dsa_proj(x2, _prep_dsa_w(w_in), tabs, s)
    r3 = lambda a: a.reshape(b, s, a.shape[1])
    wit = r3(wi)[:, :, :IDX_HEADS].transpose(0, 2, 1)
    o = _dsa_attn(r3(q), r3(k), _transpose_values(v, b, s, min(TK, s)), r3(qi), r3(ki), wit)
    return o.reshape(b * s, MIX_WIDTH)


def _fox_mixer(x2, b, s, w_in, b_f):
    d = MIX_WIDTH
    wf = jnp.pad(w_in[:, 3 * d:], ((0, 0), (0, LANES - N_HEADS))).astype(BF16)
    bf = jnp.pad(b_f, (0, LANES - N_HEADS)).reshape(1, LANES)
    q, k, v, cum = _fox_proj(x2, w_in[:, :3 * d].astype(BF16), wf, bf, s)
    r3 = lambda a: a.reshape(b, s, a.shape[1])
    cum_q = r3(cum)[:, :, :N_HEADS].transpose(0, 2, 1)
    o = _fox_attn(r3(q), r3(k), _transpose_values(v, b, s, min(TK, s)), cum_q, r3(cum))
    return o.reshape(b * s, d)


def _mla_mixer(x2, b, s, w_dqkv, gq, w_uq, gkv, w_ukv, tabs):
    wd, uq, uk, uv = _prep_mla_w(w_dqkv, w_uq, w_ukv)
    q, k, v = _mla_proj(x2, wd, gq.reshape(1, -1), gkv.reshape(1, -1), uq, uk, uv, tabs, s)
    r3 = lambda a: a.reshape(b, s, a.shape[1])
    o = _mla_attn(r3(q), r3(k), _transpose_values(v, b, s, min(TK, s)))
    return o.reshape(b * s, MIX_WIDTH)


def kernel(x, ffn1_w13, ffn1_w2, ffn2_w13, ffn2_w2, ln_g, ln_b, w_out, dsa_w_in, fox_w_in, fox_b_f,
           mla_w_dqkv, mla_q_norm_g, mla_w_uq, mla_kv_norm_g, mla_w_ukv):
    b, s, dm = x.shape
    x2 = x.reshape(b * s, dm)
    tabs_p = _rope_tables(s, ROT_DIM, 0, HEAD_DIM)
    tabs_m = _rope_tables(s, MLA_ROPE, MLA_NOPE, LANES)
    for i in range(DEPTH):
        g = lambda r, i=i: ln_g[i, r].reshape(1, dm)
        be = lambda r, i=i: ln_b[i, r].reshape(1, dm)
        x2 = _ffn_ln(x2, *_prep_ffn(ffn1_w13[i], ffn1_w2[i]), g(0), be(0))
        kind, j = i % N_MIXERS, i // N_MIXERS
        if kind == 0:
            o = _dsa_mixer(x2, b, s, dsa_w_in[j], tabs_p)
        elif kind == 1:
            o = _fox_mixer(x2, b, s, fox_w_in[j], fox_b_f[j])
        else:
            o = _mla_mixer(x2, b, s, mla_w_dqkv[j], mla_q_norm_g[j], mla_w_uq[j],
                           mla_kv_norm_g[j], mla_w_ukv[j], tabs_m)
        x2 = _outproj_ln(x2, o, w_out[i].astype(BF16), g(1), be(1))
        x2 = _ffn_ln(x2, *_prep_ffn(ffn2_w13[i], ffn2_w2[i]), g(2), be(2))
    return x2.reshape(b, s, dm)
```

```python
import functools

import jax
import jax.numpy as jnp
from jax import lax
from jax.experimental import pallas as pl
from jax.experimental.pallas import tpu as pltpu

F32 = jnp.float32
BF16 = jnp.bfloat16
I32 = jnp.int32
I16 = jnp.int16

D_MODEL = 1024
DEPTH = 4
N_MIXERS = 3
HEAD_DIM = 64
N_HEADS = D_MODEL // HEAD_DIM
MIX_WIDTH = N_HEADS * HEAD_DIM
ROT_DIM = HEAD_DIM // 4
ROPE_THETA = 500000.0
IDX_HEADS = 8
IDX_DIM = 64
TOPK_MAX = 256
MLA_NOPE = 64
MLA_ROPE = 32
MLA_V = 64
Q_LORA = 384
KV_LORA = 256
D_FF = 2816
ALPHA = (2.0 * DEPTH) ** 0.25
LN_EPS = 1e-5
RMS_EPS = 1e-6

LANES = 128
N_SLABS = MIX_WIDTH // LANES
VT_ROWS = LANES + 16
LOG2E = 1.4426950408889634
NEG = -1e30
INT_MIN = -2147483648

TM = 512
TF = 256
TQ = 256
TK = 512
VMEM_LIMIT = 56 * 1024 * 1024


def _params(n_axes, vmem=VMEM_LIMIT):
    return pltpu.CompilerParams(dimension_semantics=("arbitrary",) * n_axes, vmem_limit_bytes=vmem)


def _const_spec(shape):
    nd = len(shape)
    return pl.BlockSpec(shape, lambda *_: (0,) * nd)


def _dot(a, b):
    return jnp.dot(a, b, preferred_element_type=F32)


def _dot_nt(a, b):
    return lax.dot_general(a, b, (((1,), (1,)), ((), ())), preferred_element_type=F32)


def _layer_norm(z, g, b):
    mu = jnp.mean(z, axis=-1, keepdims=True)
    d = z - mu
    var = jnp.mean(d * d, axis=-1, keepdims=True)
    return d * lax.rsqrt(var + LN_EPS) * g + b


def _rms_norm(z, g):
    ms = jnp.mean(z * z, axis=-1, keepdims=True)
    return z * lax.rsqrt(ms + RMS_EPS) * g


def _rope_slab(y, cos, sin_lo, sin_hi, half):
    return y * cos + pltpu.roll(y, half, 1) * sin_hi + pltpu.roll(y, LANES - half, 1) * sin_lo


def _ffn_ln_kernel(x_ref, w13_ref, w2_ref, g_ref, b_ref, o_ref):
    x = x_ref[...]
    xb = x.astype(BF16)
    ff = w2_ref.shape[0]
    acc = jnp.zeros(x.shape, F32)
    for c0 in range(0, ff, TF):
        gate = _dot(xb, w13_ref[:, c0:c0 + TF])
        up = _dot(xb, w13_ref[:, ff + c0:ff + c0 + TF])
        h = gate * jax.nn.sigmoid(gate) * up
        acc = acc + _dot(h.astype(BF16), w2_ref[c0:c0 + TF, :])
    o_ref[...] = _layer_norm(ALPHA * x + 0.5 * acc, g_ref[...], b_ref[...])


def _ffn_ln(x2, w13, w2, g, b):
    n, d = x2.shape
    return pl.pallas_call(
        _ffn_ln_kernel,
        grid=(n // TM,),
        in_specs=[pl.BlockSpec((TM, d), lambda i: (i, 0)),
                  _const_spec(w13.shape), _const_spec(w2.shape),
                  _const_spec(g.shape), _const_spec(b.shape)],
        out_specs=pl.BlockSpec((TM, d), lambda i: (i, 0)),
        out_shape=jax.ShapeDtypeStruct((n, d), F32),
        compiler_params=_params(1),
        name="ffn_ln",
    )(x2, w13.astype(BF16), w2.astype(BF16), g, b)


def _outproj_ln_kernel(x_ref, o_ref, w_ref, g_ref, b_ref, out_ref):
    y = _dot(o_ref[...], w_ref[...])
    out_ref[...] = _layer_norm(ALPHA * x_ref[...] + y, g_ref[...], b_ref[...])


def _outproj_ln(x2, o2, w, g, b):
    n, d = x2.shape
    return pl.pallas_call(
        _outproj_ln_kernel,
        grid=(n // TM,),
        in_specs=[pl.BlockSpec((TM, d), lambda i: (i, 0)),
                  pl.BlockSpec((TM, o2.shape[1]), lambda i: (i, 0)),
                  _const_spec(w.shape), _const_spec(g.shape), _const_spec(b.shape)],
        out_specs=pl.BlockSpec((TM, d), lambda i: (i, 0)),
        out_shape=jax.ShapeDtypeStruct((n, d), F32),
        compiler_params=_params(1),
        name="outproj_ln",
    )(x2, o2, w, g, b)


def _rope_tables(seq, rot, lane_of_rot0, period):
    half = rot // 2
    inv = ROPE_THETA ** (-jnp.arange(0, rot, 2, dtype=F32) / rot)
    ang = jnp.arange(seq, dtype=F32)[:, None] * inv[None, :]
    cos_h, sin_h = jnp.cos(ang), jnp.sin(ang)
    r = (jnp.arange(LANES) % period) - lane_of_rot0
    in_lo = (r >= 0) & (r < half)
    in_hi = (r >= half) & (r < rot)
    idx = jnp.clip(jnp.where(in_hi, r - half, r), 0, half - 1)
    cos = jnp.where((in_lo | in_hi)[None, :], cos_h[:, idx], 1.0)
    sin_lo = jnp.where(in_lo[None, :], -sin_h[:, idx], 0.0)
    sin_hi = jnp.where(in_hi[None, :], sin_h[:, idx], 0.0)
    return cos, sin_lo, sin_hi


def _dsa_proj_kernel(x_ref, w_ref, cos_ref, slo_ref, shi_ref,
                     q_ref, k_ref, v_ref, qi_ref, ki_ref, wi_ref):
    xb = x_ref[...].astype(BF16)
    cos, slo, shi = cos_ref[...], slo_ref[...], shi_ref[...]
    half = ROT_DIM // 2
    d = MIX_WIDTH
    di = IDX_HEADS * IDX_DIM

    def roped(col0, width, scale, out_ref):
        y = _dot(xb, w_ref[:, col0:col0 + width])
        for j in range(width // LANES):
            s = _rope_slab(y[:, j * LANES:(j + 1) * LANES], cos, slo, shi, half)
            if scale != 1.0:
                s = s * scale
            out_ref[:, j * LANES:(j + 1) * LANES] = s.astype(out_ref.dtype)

    roped(0, d, HEAD_DIM ** -0.5 * LOG2E, q_ref)
    roped(d, d, 1.0, k_ref)
    v_ref[...] = _dot(xb, w_ref[:, 2 * d:3 * d]).astype(BF16)
    roped(3 * d, di, IDX_DIM ** -0.5, qi_ref)
    roped(3 * d + di, LANES, 1.0, ki_ref)
    wi_ref[...] = _dot(xb, w_ref[:, 3 * d + di + LANES:]) * (IDX_HEADS ** -0.5)


def _dsa_proj(x2, w, tabs, seq):
    n, dm = x2.shape
    d, di = MIX_WIDTH, IDX_HEADS * IDX_DIM
    tiles_per_seq = seq // TM
    row = lambda width: pl.BlockSpec((TM, width), lambda i: (i, 0))
    tab = pl.BlockSpec((TM, LANES), lambda i: (i % tiles_per_seq, 0))
    return pl.pallas_call(
        _dsa_proj_kernel,
        grid=(n // TM,),
        in_specs=[row(dm), _const_spec(w.shape), tab, tab, tab],
        out_specs=[row(d), row(d), row(d), row(di), row(LANES), row(LANES)],
        out_shape=[jax.ShapeDtypeStruct((n, d), BF16), jax.ShapeDtypeStruct((n, d), BF16),
                   jax.ShapeDtypeStruct((n, d), BF16), jax.ShapeDtypeStruct((n, di), BF16),
                   jax.ShapeDtypeStruct((n, LANES), BF16), jax.ShapeDtypeStruct((n, LANES), F32)],
        compiler_params=_params(1),
        name="dsa_proj",
    )(x2, w, *tabs)


def _prep_dsa_w(w_in):
    d, di = MIX_WIDTH, IDX_HEADS * IDX_DIM
    ki = w_in[:, 3 * d + di:3 * d + di + IDX_DIM]
    wi = w_in[:, 3 * d + di + IDX_DIM:]
    pad = jnp.zeros((w_in.shape[0], LANES - IDX_HEADS), w_in.dtype)
    return jnp.concatenate([w_in[:, :3 * d + di], ki, ki, wi, pad], axis=1).astype(BF16)


def _fox_proj_kernel(x_ref, w_ref, wf_ref, bf_ref, q_ref, k_ref, v_ref, cum_ref, carry_ref,
                     *, tiles_per_seq):
    i = pl.program_id(0)
    xb = x_ref[...].astype(BF16)
    d = MIX_WIDTH
    q_ref[...] = (_dot(xb, w_ref[:, :d]) * (HEAD_DIM ** -0.5 * LOG2E)).astype(BF16)
    k_ref[...] = _dot(xb, w_ref[:, d:2 * d]).astype(BF16)
    v_ref[...] = _dot(xb, w_ref[:, 2 * d:]).astype(BF16)

    f = _dot(xb, wf_ref[...]) + bf_ref[...]
    log_f = (jnp.minimum(f, 0.0) - jnp.log1p(jnp.exp(-jnp.abs(f)))) * LOG2E
    tm = log_f.shape[0]
    tri = (lax.broadcasted_iota(I32, (tm, tm), 0) >= lax.broadcasted_iota(I32, (tm, tm), 1))
    tri = jnp.where(tri, 1.0, 0.0).astype(BF16)
    p0 = log_f.astype(BF16)
    r1 = log_f - p0.astype(F32)
    p1 = r1.astype(BF16)
    p2 = (r1 - p1.astype(F32)).astype(BF16)
    local = _dot(tri, p0) + _dot(tri, p1) + _dot(tri, p2)

    @pl.when(i % tiles_per_seq == 0)
    def _():
        carry_ref[...] = jnp.zeros_like(carry_ref)

    cum = local + carry_ref[...]
    cum_ref[...] = cum
    carry_ref[...] = cum[tm - 1:tm, :]


def _fox_proj(x2, w, wf, bf, seq):
    n, dm = x2.shape
    d = MIX_WIDTH
    row = lambda width: pl.BlockSpec((TM, width), lambda i: (i, 0))
    return pl.pallas_call(
        functools.partial(_fox_proj_kernel, tiles_per_seq=seq // TM),
        grid=(n // TM,),
        in_specs=[row(dm), _const_spec(w.shape), _const_spec(wf.shape), _const_spec(bf.shape)],
        out_specs=[row(d), row(d), row(d), row(LANES)],
        out_shape=[jax.ShapeDtypeStruct((n, d), BF16)] * 3 + [jax.ShapeDtypeStruct((n, LANES), F32)],
        scratch_shapes=[pltpu.VMEM((1, LANES), F32)],
        compiler_params=_params(1),
        name="fox_proj",
    )(x2, w, wf, bf)


def _mla_proj_kernel(x_ref, wd_ref, gq_ref, gkv_ref, wuq_ref, wuk_ref, wuv_ref,
                     cos_ref, slo_ref, shi_ref, q_ref, k_ref, v_ref):
    xb = x_ref[...].astype(BF16)
    cos, slo, shi = cos_ref[...], slo_ref[...], shi_ref[...]
    half = MLA_ROPE // 2
    c = _dot(xb, wd_ref[...])
    cq = _rms_norm(c[:, :Q_LORA], gq_ref[...]).astype(BF16)
    ckv = _rms_norm(c[:, Q_LORA:Q_LORA + KV_LORA], gkv_ref[...]).astype(BF16)
    k_rope = _rope_slab(c[:, Q_LORA + KV_LORA:], cos, slo, shi, half)
    scale = (MLA_NOPE + MLA_ROPE) ** -0.5 * LOG2E
    for h in range(N_HEADS):
        sl = slice(h * LANES, (h + 1) * LANES)
        q = _rope_slab(_dot(cq, wuq_ref[:, sl]), cos, slo, shi, half)
        q_ref[:, sl] = (q * scale).astype(BF16)
        k_ref[:, sl] = (_dot(ckv, wuk_ref[:, sl]) + k_rope).astype(BF16)
    v_ref[...] = _dot(ckv, wuv_ref[...]).astype(BF16)


def _mla_proj(x2, wd, gq, gkv, wuq, wuk, wuv, tabs, seq):
    n, dm = x2.shape
    tiles_per_seq = seq // TM
    row = lambda width: pl.BlockSpec((TM, width), lambda i: (i, 0))
    tab = pl.BlockSpec((TM, LANES), lambda i: (i % tiles_per_seq, 0))
    wide = N_HEADS * LANES
    return pl.pallas_call(
        _mla_proj_kernel,
        grid=(n // TM,),
        in_specs=[row(dm)] + [_const_spec(a.shape) for a in (wd, gq, gkv, wuq, wuk, wuv)] + [tab] * 3,
        out_specs=[row(wide), row(wide), row(MIX_WIDTH)],
        out_shape=[jax.ShapeDtypeStruct((n, wide), BF16), jax.ShapeDtypeStruct((n, wide), BF16),
                   jax.ShapeDtypeStruct((n, MIX_WIDTH), BF16)],
        compiler_params=_params(1),
        name="mla_proj",
    )(x2, wd, gq, gkv, wuq, wuk, wuv, *tabs)


def _prep_mla_w(w_dqkv, w_uq, w_ukv):
    dm = w_dqkv.shape[0]
    zeros = lambda r, c: jnp.zeros((r, c), w_dqkv.dtype)
    k_rope_w = w_dqkv[:, Q_LORA + KV_LORA:]
    wd = jnp.concatenate([w_dqkv[:, :Q_LORA + KV_LORA], zeros(dm, MLA_NOPE), k_rope_w,
                          zeros(dm, LANES - MLA_NOPE - MLA_ROPE)], axis=1)
    uq = w_uq.reshape(Q_LORA, N_HEADS, MLA_NOPE + MLA_ROPE)
    uq = jnp.pad(uq, ((0, 0), (0, 0), (0, LANES - MLA_NOPE - MLA_ROPE))).reshape(Q_LORA, N_HEADS * LANES)
    ukv = w_ukv.reshape(KV_LORA, N_HEADS, MLA_NOPE + MLA_V)
    uk = jnp.pad(ukv[:, :, :MLA_NOPE], ((0, 0), (0, 0), (0, LANES - MLA_NOPE))).reshape(KV_LORA, N_HEADS * LANES)
    uv = ukv[:, :, MLA_NOPE:].reshape(KV_LORA, N_HEADS * MLA_V)
    return wd.astype(BF16), uq.astype(BF16), uk.astype(BF16), uv.astype(BF16)


def _attend_slabs(slabs, n_plain, causal, qpos, tk, one_pass):
    m2 = slabs[0][0].shape[0]
    tq = m2 // 2
    kiota = lax.broadcasted_iota(I32, (tk, m2), 0)

    def step(c, carries, masked, stale):
        scores = [_dot_nt(k_load(c), q2) for q2, k_load, _, _ in slabs]
        mask = (c * tk + kiota <= qpos) if masked else None
        out = []
        for (_, _, vt_load, bias_fn), (m, acc), s in zip(slabs, carries, scores):
            if bias_fn is not None:
                s = bias_fn(c) + s
            if mask is not None:
                s = jnp.where(mask, s, NEG)
            m_new = jnp.maximum(m, jnp.max(s, axis=0, keepdims=True))
            alpha = jnp.exp2(m - m_new)
            if stale:
                acc = (acc + _dot(vt_load(c), jnp.exp2(s - m).astype(BF16))) * alpha
            else:
                acc = alpha * acc + _dot(vt_load(c), jnp.exp2(s - m_new).astype(BF16))
            out.append((m_new, acc))
        return tuple(out)

    carries = tuple((jnp.full((1, m2), NEG, F32), jnp.zeros((VT_ROWS, m2), F32)) for _ in slabs)
    if one_pass:
        carries = step(0, carries, causal, False)
        carries = lax.fori_loop(1, n_plain, lambda c, cr: step(c, cr, False, True), carries)
        if causal:
            last = jnp.where(n_plain > 0, n_plain + 1, n_plain)
            carries = lax.fori_loop(n_plain, last, lambda c, cr: step(c, cr, True, True), carries)
    else:
        carries = lax.fori_loop(0, n_plain, lambda c, cr: step(c, cr, False, False), carries)
        if causal:
            carries = step(n_plain, carries, True, False)
    row = lax.broadcasted_iota(I32, (LANES, tq), 0)
    outs = []
    for _, acc in carries:
        o = acc[:LANES] / acc[LANES:LANES + 1]
        outs.append(jnp.where(row < HEAD_DIM, o[:, :tq], o[:, tq:]))
    return outs


def _attend_to(o_ref, slabs, n_plain, causal, qpos, tk, one_pass):
    def run(one_pass):
        outs = _attend_slabs(slabs, n_plain, causal, qpos, tk, one_pass)
        for j, out in enumerate(outs):
            o_ref[0, :, j * LANES:(j + 1) * LANES] = out.T.astype(o_ref.dtype)
        return outs

    if not one_pass:
        run(False)
        return
    outs = run(True)
    total = functools.reduce(lambda a, b: a + b, [jnp.sum(o, axis=0, keepdims=True) for o in outs])
    overflowed = jnp.max(jnp.where(jnp.isfinite(total), 0.0, 1.0)) > 0.0

    @pl.when(overflowed)
    def _():
        run(False)


def _split_heads(q_slab):
    lane = lax.broadcasted_iota(I32, q_slab.shape, 1)
    zero = jnp.zeros_like(q_slab)
    return jnp.where(lane < HEAD_DIM, q_slab, zero), jnp.where(lane >= HEAD_DIM, q_slab, zero)


def _kv_chunk(ref, c, tk, lanes):
    return ref[0, pl.ds(pl.multiple_of(c * tk, tk), tk), lanes]


def _per_batch_spec(a):
    nd = a.ndim
    return pl.BlockSpec((1,) + a.shape[1:], lambda bi, i: (bi,) + (0,) * (nd - 1),
                        pipeline_mode=pl.Buffered(1))


def _query_positions(row0, tq):
    pos = row0 + lax.broadcasted_iota(I32, (1, tq), 1)
    return jnp.concatenate([pos, pos], axis=1)


def _transpose_values(v, b, s, tk):
    vt = v.reshape(b, s // tk, tk, N_SLABS, LANES).transpose(0, 1, 3, 4, 2)
    extra = jnp.zeros((b, s // tk, N_SLABS, VT_ROWS - LANES, tk), v.dtype).at[:, :, :, 0, :].set(1)
    return jnp.concatenate([vt, extra], axis=3).reshape(b, s // tk, N_SLABS * VT_ROWS, tk)


def _fox_attn_kernel(q_ref, k_ref, vt_ref, cq_ref, ck_ref, o_ref, *, tq, tk):
    i = pl.program_id(1)
    row0 = i * tq
    diag = row0 // tk
    qpos = _query_positions(row0, tq)
    cq = cq_ref[0]

    def slab(j):
        lanes = slice(j * LANES, (j + 1) * LANES)
        q2 = jnp.concatenate(_split_heads(q_ref[0, :, lanes]), axis=0)
        cq2 = jnp.concatenate([cq[2 * j:2 * j + 1, :], cq[2 * j + 1:2 * j + 2, :]], axis=1)

        def bias_fn(c):
            ck = ck_ref[0, pl.ds(pl.multiple_of(c * tk, tk), tk), :]
            ck2 = jnp.concatenate([jnp.broadcast_to(ck[:, 2 * j:2 * j + 1], (tk, tq)),
                                   jnp.broadcast_to(ck[:, 2 * j + 1:2 * j + 2], (tk, tq))], axis=1)
            return cq2 - ck2

        return (q2, lambda c: _kv_chunk(k_ref, c, tk, lanes),
                lambda c: vt_ref[0, c, j * VT_ROWS:(j + 1) * VT_ROWS, :], bias_fn)

    _attend_to(o_ref, [slab(j) for j in range(N_SLABS)], diag, True, qpos, tk, False)


def _fox_attn(q, k, vt, cum_q, cum_k):
    b, s, d = q.shape
    tq, tk = min(TQ, s), vt.shape[3]
    return pl.pallas_call(
        functools.partial(_fox_attn_kernel, tq=tq, tk=tk),
        grid=(b, s // tq),
        in_specs=[pl.BlockSpec((1, tq, d), lambda bi, i: (bi, i, 0)),
                  _per_batch_spec(k), _per_batch_spec(vt),
                  pl.BlockSpec((1, N_HEADS, tq), lambda bi, i: (bi, 0, i)),
                  _per_batch_spec(cum_k)],
        out_specs=pl.BlockSpec((1, tq, d), lambda bi, i: (bi, i, 0)),
        out_shape=jax.ShapeDtypeStruct((b, s, d), BF16),
        compiler_params=_params(2),
        name="fox_attn",
    )(q, k, vt, cum_q, cum_k)


def _mla_attn_kernel(q_ref, k_ref, vt_ref, o_ref, *, tq, tk):
    i = pl.program_id(1)
    row0 = i * tq
    diag = row0 // tk
    qpos = _query_positions(row0, tq)
    zero = jnp.zeros((tq, LANES), BF16)

    def slab(j):
        pair = slice(2 * j * LANES, (2 * j + 2) * LANES)
        qa = q_ref[0, :, 2 * j * LANES:(2 * j + 1) * LANES]
        qb = q_ref[0, :, (2 * j + 1) * LANES:(2 * j + 2) * LANES]
        q2 = jnp.concatenate([jnp.concatenate([qa, zero], axis=1),
                              jnp.concatenate([zero, qb], axis=1)], axis=0)
        lanes = slice(j * LANES, (j + 1) * LANES)
        return (q2, lambda c: _kv_chunk(k_ref, c, tk, pair), lambda c: vt_ref[0, c, j * VT_ROWS:(j + 1) * VT_ROWS, :], None)

    _attend_to(o_ref, [slab(j) for j in range(N_SLABS)], diag, True, qpos, tk, True)


def _mla_attn(q, k, vt):
    b, s, wide = q.shape
    d = MIX_WIDTH
    tq, tk = min(TQ, s), vt.shape[3]
    return pl.pallas_call(
        functools.partial(_mla_attn_kernel, tq=tq, tk=tk),
        grid=(b, s // tq),
        in_specs=[pl.BlockSpec((1, tq, wide), lambda bi, i: (bi, i, 0)),
                  _per_batch_spec(k), _per_batch_spec(vt)],
        out_specs=pl.BlockSpec((1, tq, d), lambda bi, i: (bi, i, 0)),
        out_shape=jax.ShapeDtypeStruct((b, s, d), BF16),
        compiler_params=_params(2),
        name="mla_attn",
    )(q, k, vt)


def _sortable_key(score):
    bits = pltpu.bitcast(score + 0.0, I32)
    return jnp.where(bits < 0, bits ^ 0x7FFFFFFF, bits)


def _count_keys(ref, n_chunks, indicator, acc_rows):
    _, tk, tq = ref.shape

    def body(c, acc):
        ind = indicator(ref[c], c)
        for r in range(tk // acc_rows):
            acc = acc + ind[r * acc_rows:(r + 1) * acc_rows]
        return acc

    acc = lax.fori_loop(0, n_chunks, body, jnp.zeros((acc_rows, tq), ref.dtype))
    return jnp.sum(acc.astype(F32), axis=0, keepdims=True)


def _radix_select16(ref, n_chunks, target):
    tq = ref.shape[2]
    one, zero = jnp.int16(1), jnp.int16(0)

    def signed(u):
        return lax.shift_right_arithmetic(lax.shift_left(u ^ 0x8000, 16), 16)

    def bit_step(t, prefix):
        cand = prefix | lax.shift_left(jnp.int32(1), 15 - t)
        cand16 = signed(cand).astype(I16)
        cnt = _count_keys(ref, n_chunks, lambda blk, c: jnp.where(blk >= cand16, one, zero), 64)
        return jnp.where(cnt >= target, cand, prefix)

    return signed(lax.fori_loop(0, 16, bit_step, jnp.zeros((1, tq), I32)))


def _dsa_attn_kernel(q_ref, k_ref, vt_ref, qi_ref, ki_ref, wit_ref, o_ref,
                     qim_ref, key_ref, hi_ref, lo_ref, bias_ref, *, tq, tk, topk):
    i = pl.program_id(1)
    row0 = i * tq
    n_chunks = row0 // tk + 1
    qpos = row0 + lax.broadcasted_iota(I32, (1, tq), 1)
    kiota = lax.broadcasted_iota(I32, (tk, tq), 0)

    for h in range(IDX_HEADS):
        pair = _split_heads(qi_ref[0, :, (h // 2) * LANES:(h // 2 + 1) * LANES])
        qim_ref[h] = pair[h % 2]
    wit = wit_ref[0]

    def score_chunk(c, _):
        ki = _kv_chunk(ki_ref, c, tk, slice(None))
        acc = jnp.zeros((tk, tq), F32)
        for h in range(IDX_HEADS):
            acc = acc + wit[h:h + 1, :] * jnp.maximum(_dot_nt(ki, qim_ref[h]), 0.0)
        key = jnp.where(c * tk + kiota <= qpos, _sortable_key(acc), INT_MIN)
        key_ref[c] = key
        hi_ref[c] = lax.shift_right_arithmetic(key, 16).astype(I16)
        return 0

    lax.fori_loop(0, n_chunks, score_chunk, 0)

    k_f = float(topk)
    one16, zero16 = jnp.int16(1), jnp.int16(0)
    thr_hi = _radix_select16(hi_ref, n_chunks, k_f)
    thr_hi16 = thr_hi.astype(I16)
    n_gt_hi = _count_keys(hi_ref, n_chunks, lambda blk, c: jnp.where(blk > thr_hi16, one16, zero16), 64)

    def low_plane_chunk(c, _):
        key = key_ref[c]
        low = (key & 0xFFFF) - 0x8000
        in_bucket = lax.shift_right_arithmetic(key, 16) == thr_hi
        lo_ref[c] = jnp.where(in_bucket, low, -0x8000).astype(I16)
        return 0

    lax.fori_loop(0, n_chunks, low_plane_chunk, 0)
    thr_lo = _radix_select16(lo_ref, n_chunks, k_f - n_gt_hi)
    thr_lo16 = thr_lo.astype(I16)
    live = thr_hi != -0x8000
    thr = jnp.where(live, thr_hi * 0x10000 + (thr_lo + 0x8000), INT_MIN)
    n_gt = n_gt_hi + _count_keys(lo_ref, n_chunks,
                                 lambda blk, c: jnp.where(blk > thr_lo16, one16, zero16), 64)

    def count(indicator):
        return _count_keys(key_ref, n_chunks, indicator, 32)

    n_eq = count(lambda blk, c: jnp.where(blk == thr, 1, 0))
    need = k_f - n_gt

    def tie_step(t, jmax):
        cand = jmax | lax.shift_left(jnp.int32(1), 12 - t)
        cnt = count(lambda blk, c: jnp.where(
            blk == thr, jnp.where(c * tk + kiota < cand, 1, 0), 0))
        return jnp.where(cnt < need, cand, jmax)

    has_ties = jnp.max(jnp.where(live, n_eq - need, 0.0)) > 0.0
    jmax = lax.cond(has_ties,
                    lambda: lax.fori_loop(0, 13, tie_step, jnp.zeros((1, tq), I32)),
                    lambda: jnp.full((1, tq), 2 ** 13 - 1, I32))
    jmax = jnp.where(live, jmax, -1)

    def bias_chunk(c, _):
        blk = key_ref[c]
        tie_ok = jnp.where(c * tk + kiota <= jmax, 0.0, NEG)
        bias_ref[c] = jnp.where(blk > thr, 0.0, jnp.where(blk == thr, tie_ok, NEG))
        return 0

    lax.fori_loop(0, n_chunks, bias_chunk, 0)

    def bias_fn(c):
        b = bias_ref[c]
        return jnp.concatenate([b, b], axis=1)

    def slab(j):
        lanes = slice(j * LANES, (j + 1) * LANES)
        q2 = jnp.concatenate(_split_heads(q_ref[0, :, lanes]), axis=0)
        return (q2, lambda c: _kv_chunk(k_ref, c, tk, lanes), lambda c: vt_ref[0, c, j * VT_ROWS:(j + 1) * VT_ROWS, :], bias_fn)

    _attend_to(o_ref, [slab(j) for j in range(N_SLABS)], n_chunks, False, None, tk, True)


def _dsa_attn(q, k, vt, qi, ki, wit):
    b, s, d = q.shape
    tq, tk = min(TQ, s), vt.shape[3]
    topk = min(TOPK_MAX, s // 4)
    per_q = lambda width: pl.BlockSpec((1, tq, width), lambda bi, i: (bi, i, 0))
    return pl.pallas_call(
        functools.partial(_dsa_attn_kernel, tq=tq, tk=tk, topk=topk),
        grid=(b, s // tq),
        in_specs=[per_q(d), _per_batch_spec(k), _per_batch_spec(vt),
                  per_q(qi.shape[2]), _per_batch_spec(ki),
                  pl.BlockSpec((1, IDX_HEADS, tq), lambda bi, i: (bi, 0, i))],
        out_specs=per_q(d),
        out_shape=jax.ShapeDtypeStruct((b, s, d), BF16),
        scratch_shapes=[pltpu.VMEM((IDX_HEADS, tq, LANES), BF16),
                        pltpu.VMEM((s // tk, tk, tq), I32),
                        pltpu.VMEM((s // tk, tk, tq), I16),
                        pltpu.VMEM((s // tk, tk, tq), I16),
                        pltpu.VMEM((s // tk, tk, tq), F32)],
        compiler_params=_params(2),
        name="dsa_attn",
    )(q, k, vt, qi, ki, wit)


def _dsa_mixer(x2, b, s, w_in, tabs):
    q, k, v, qi, ki, wi = _dsa_proj(x2, _prep_dsa_w(w_in), tabs, s)
    r3 = lambda a: a.reshape(b, s, a.shape[1])
    wit = r3(wi)[:, :, :IDX_HEADS].transpose(0, 2, 1)
    o = _dsa_attn(r3(q), r3(k), _transpose_values(v, b, s, min(TK, s)), r3(qi), r3(ki), wit)
    return o.reshape(b * s, MIX_WIDTH)


def _fox_mixer(x2, b, s, w_in, b_f):
    d = MIX_WIDTH
    wf = jnp.pad(w_in[:, 3 * d:], ((0, 0), (0, LANES - N_HEADS))).astype(BF16)
    bf = jnp.pad(b_f, (0, LANES - N_HEADS)).reshape(1, LANES)
    q, k, v, cum = _fox_proj(x2, w_in[:, :3 * d].astype(BF16), wf, bf, s)
    r3 = lambda a: a.reshape(b, s, a.shape[1])
    cum_q = r3(cum)[:, :, :N_HEADS].transpose(0, 2, 1)
    o = _fox_attn(r3(q), r3(k), _transpose_values(v, b, s, min(TK, s)), cum_q, r3(cum))
    return o.reshape(b * s, d)


def _mla_mixer(x2, b, s, w_dqkv, gq, w_uq, gkv, w_ukv, tabs):
    wd, uq, uk, uv = _prep_mla_w(w_dqkv, w_uq, w_ukv)
    q, k, v = _mla_proj(x2, wd, gq.reshape(1, -1), gkv.reshape(1, -1), uq, uk, uv, tabs, s)
    r3 = lambda a: a.reshape(b, s, a.shape[1])
    o = _mla_attn(r3(q), r3(k), _transpose_values(v, b, s, min(TK, s)))
    return o.reshape(b * s, MIX_WIDTH)


def kernel(x, ffn1_w13, ffn1_w2, ffn2_w13, ffn2_w2, ln_g, ln_b, w_out, dsa_w_in, fox_w_in, fox_b_f,
           mla_w_dqkv, mla_q_norm_g, mla_w_uq, mla_kv_norm_g, mla_w_ukv):
    b, s, dm = x.shape
    x2 = x.reshape(b * s, dm)
    tabs_p = _rope_tables(s, ROT_DIM, 0, HEAD_DIM)
    tabs_m = _rope_tables(s, MLA_ROPE, MLA_NOPE, LANES)
    for i in range(DEPTH):
        g = lambda r, i=i: ln_g[i, r].reshape(1, dm)
        be = lambda r, i=i: ln_b[i, r].reshape(1, dm)
        x2 = _ffn_ln(x2, ffn1_w13[i], ffn1_w2[i], g(0), be(0))
        kind, j = i % N_MIXERS, i // N_MIXERS
        if kind == 0:
            o = _dsa_mixer(x2, b, s, dsa_w_in[j], tabs_p)
        elif kind == 1:
            o = _fox_mixer(x2, b, s, fox_w_in[j], fox_b_f[j])
        else:
            o = _mla_mixer(x2, b, s, mla_w_dqkv[j], mla_q_norm_g[j], mla_w_uq[j],
                           mla_kv_norm_g[j], mla_w_ukv[j], tabs_m)
        x2 = _outproj_ln(x2, o, w_out[i].astype(BF16), g(1), be(1))
        x2 = _ffn_ln(x2, ffn2_w13[i], ffn2_w2[i], g(2), be(2))
    return x2.reshape(b, s, dm)
```

```python
import functools

import jax
import jax.numpy as jnp
from jax import lax
from jax.experimental import pallas as pl
from jax.experimental.pallas import tpu as pltpu

F32 = jnp.float32
BF16 = jnp.bfloat16
I32 = jnp.int32
I16 = jnp.int16

D_MODEL = 1024
DEPTH = 4
N_MIXERS = 3
HEAD_DIM = 64
N_HEADS = D_MODEL // HEAD_DIM
MIX_WIDTH = N_HEADS * HEAD_DIM
ROT_DIM = HEAD_DIM // 4
ROPE_THETA = 500000.0
IDX_HEADS = 8
IDX_DIM = 64
TOPK_MAX = 256
MLA_NOPE = 64
MLA_ROPE = 32
MLA_V = 64
Q_LORA = 384
KV_LORA = 256
D_FF = 2816
ALPHA = (2.0 * DEPTH) ** 0.25
LN_EPS = 1e-5
RMS_EPS = 1e-6

LANES = 128
N_SLABS = MIX_WIDTH // LANES
VT_ROWS = LANES + 16
LOG2E = 1.4426950408889634
NEG = -1e30
INT_MIN = -2147483648

TM = 512
TF = 256
TQ = 256
VMEM_LIMIT = 56 * 1024 * 1024


def _params(n_axes, vmem=VMEM_LIMIT):
    return pltpu.CompilerParams(dimension_semantics=("arbitrary",) * n_axes, vmem_limit_bytes=vmem)


def _const_spec(shape):
    nd = len(shape)
    return pl.BlockSpec(shape, lambda *_: (0,) * nd)


def _dot(a, b):
    return jnp.dot(a, b, preferred_element_type=F32)


def _dot_nt(a, b):
    return lax.dot_general(a, b, (((1,), (1,)), ((), ())), preferred_element_type=F32)


def _layer_norm(z, g, b):
    mu = jnp.mean(z, axis=-1, keepdims=True)
    d = z - mu
    var = jnp.mean(d * d, axis=-1, keepdims=True)
    return d * lax.rsqrt(var + LN_EPS) * g + b


def _rms_norm(z, g):
    ms = jnp.mean(z * z, axis=-1, keepdims=True)
    return z * lax.rsqrt(ms + RMS_EPS) * g


def _rope_slab(y, cos, sin_lo, sin_hi, half):
    return y * cos + pltpu.roll(y, half, 1) * sin_hi + pltpu.roll(y, LANES - half, 1) * sin_lo


def _store_values_t(vt_ref, v):
    tm = v.shape[0]
    extra = jnp.where(lax.broadcasted_iota(I32, (VT_ROWS - LANES, tm), 0) == 0, 1.0, 0.0).astype(BF16)
    for j in range(N_SLABS):
        vt_ref[0, 0, j * VT_ROWS:j * VT_ROWS + LANES, :] = v[:, j * LANES:(j + 1) * LANES].T.astype(BF16)
        vt_ref[0, 0, j * VT_ROWS + LANES:(j + 1) * VT_ROWS, :] = extra


def _values_t_spec(tiles_per_seq):
    return pl.BlockSpec((1, 1, N_SLABS * VT_ROWS, TM), lambda i: (i // tiles_per_seq, i % tiles_per_seq, 0, 0))


def _values_t_shape(n, seq):
    return jax.ShapeDtypeStruct((n // seq, seq // TM, N_SLABS * VT_ROWS, TM), BF16)


def _ffn_ln_kernel(x_ref, w13_ref, w2_ref, g_ref, b_ref, o_ref):
    x = x_ref[...]
    xb = x.astype(BF16)
    ff = w2_ref.shape[0]
    acc = jnp.zeros(x.shape, F32)
    for c0 in range(0, ff, TF):
        gate = _dot(xb, w13_ref[:, c0:c0 + TF])
        up = _dot(xb, w13_ref[:, ff + c0:ff + c0 + TF])
        h = gate * jax.nn.sigmoid(gate) * up
        acc = acc + _dot(h.astype(BF16), w2_ref[c0:c0 + TF, :])
    o_ref[...] = _layer_norm(ALPHA * x + 0.5 * acc, g_ref[...], b_ref[...])


def _ffn_ln(x2, w13, w2, g, b):
    n, d = x2.shape
    return pl.pallas_call(
        _ffn_ln_kernel,
        grid=(n // TM,),
        in_specs=[pl.BlockSpec((TM, d), lambda i: (i, 0)),
                  _const_spec(w13.shape), _const_spec(w2.shape),
                  _const_spec(g.shape), _const_spec(b.shape)],
        out_specs=pl.BlockSpec((TM, d), lambda i: (i, 0)),
        out_shape=jax.ShapeDtypeStruct((n, d), F32),
        compiler_params=_params(1),
        name="ffn_ln",
    )(x2, w13.astype(BF16), w2.astype(BF16), g, b)


def _outproj_ln_kernel(x_ref, o_ref, w_ref, g_ref, b_ref, out_ref):
    y = _dot(o_ref[...], w_ref[...])
    out_ref[...] = _layer_norm(ALPHA * x_ref[...] + y, g_ref[...], b_ref[...])


def _outproj_ln(x2, o2, w, g, b):
    n, d = x2.shape
    return pl.pallas_call(
        _outproj_ln_kernel,
        grid=(n // TM,),
        in_specs=[pl.BlockSpec((TM, d), lambda i: (i, 0)),
                  pl.BlockSpec((TM, o2.shape[1]), lambda i: (i, 0)),
                  _const_spec(w.shape), _const_spec(g.shape), _const_spec(b.shape)],
        out_specs=pl.BlockSpec((TM, d), lambda i: (i, 0)),
        out_shape=jax.ShapeDtypeStruct((n, d), F32),
        compiler_params=_params(1),
        name="outproj_ln",
    )(x2, o2, w, g, b)


def _rope_tables(seq, rot, lane_of_rot0, period):
    half = rot // 2
    inv = ROPE_THETA ** (-jnp.arange(0, rot, 2, dtype=F32) / rot)
    ang = jnp.arange(seq, dtype=F32)[:, None] * inv[None, :]
    cos_h, sin_h = jnp.cos(ang), jnp.sin(ang)
    r = (jnp.arange(LANES) % period) - lane_of_rot0
    in_lo = (r >= 0) & (r < half)
    in_hi = (r >= half) & (r < rot)
    idx = jnp.clip(jnp.where(in_hi, r - half, r), 0, half - 1)
    cos = jnp.where((in_lo | in_hi)[None, :], cos_h[:, idx], 1.0)
    sin_lo = jnp.where(in_lo[None, :], -sin_h[:, idx], 0.0)
    sin_hi = jnp.where(in_hi[None, :], sin_h[:, idx], 0.0)
    return cos, sin_lo, sin_hi


def _dsa_proj_kernel(x_ref, w_ref, cos_ref, slo_ref, shi_ref,
                     q_ref, k_ref, vt_ref, qi_ref, ki_ref, wi_ref):
    xb = x_ref[...].astype(BF16)
    cos, slo, shi = cos_ref[...], slo_ref[...], shi_ref[...]
    half = ROT_DIM // 2
    d = MIX_WIDTH
    di = IDX_HEADS * IDX_DIM

    def roped(col0, width, scale, out_ref):
        y = _dot(xb, w_ref[:, col0:col0 + width])
        for j in range(width // LANES):
            s = _rope_slab(y[:, j * LANES:(j + 1) * LANES], cos, slo, shi, half)
            if scale != 1.0:
                s = s * scale
            out_ref[:, j * LANES:(j + 1) * LANES] = s.astype(out_ref.dtype)

    roped(0, d, HEAD_DIM ** -0.5 * LOG2E, q_ref)
    roped(d, d, 1.0, k_ref)
    _store_values_t(vt_ref, _dot(xb, w_ref[:, 2 * d:3 * d]))
    roped(3 * d, di, IDX_DIM ** -0.5, qi_ref)
    roped(3 * d + di, LANES, 1.0, ki_ref)
    wi_ref[...] = _dot(xb, w_ref[:, 3 * d + di + LANES:]) * (IDX_HEADS ** -0.5)


def _dsa_proj(x2, w, tabs, seq):
    n, dm = x2.shape
    d, di = MIX_WIDTH, IDX_HEADS * IDX_DIM
    tiles_per_seq = seq // TM
    row = lambda width: pl.BlockSpec((TM, width), lambda i: (i, 0))
    tab = pl.BlockSpec((TM, LANES), lambda i: (i % tiles_per_seq, 0))
    return pl.pallas_call(
        _dsa_proj_kernel,
        grid=(n // TM,),
        in_specs=[row(dm), _const_spec(w.shape), tab, tab, tab],
        out_specs=[row(d), row(d), _values_t_spec(tiles_per_seq), row(di), row(LANES), row(LANES)],
        out_shape=[jax.ShapeDtypeStruct((n, d), BF16), jax.ShapeDtypeStruct((n, d), BF16),
                   _values_t_shape(n, seq), jax.ShapeDtypeStruct((n, di), BF16),
                   jax.ShapeDtypeStruct((n, LANES), BF16), jax.ShapeDtypeStruct((n, LANES), F32)],
        compiler_params=_params(1),
        name="dsa_proj",
    )(x2, w, *tabs)


def _prep_dsa_w(w_in):
    d, di = MIX_WIDTH, IDX_HEADS * IDX_DIM
    ki = w_in[:, 3 * d + di:3 * d + di + IDX_DIM]
    wi = w_in[:, 3 * d + di + IDX_DIM:]
    pad = jnp.zeros((w_in.shape[0], LANES - IDX_HEADS), w_in.dtype)
    return jnp.concatenate([w_in[:, :3 * d + di], ki, ki, wi, pad], axis=1).astype(BF16)


def _fox_proj_kernel(x_ref, w_ref, wf_ref, bf_ref, q_ref, k_ref, vt_ref, cum_ref, carry_ref,
                     *, tiles_per_seq):
    i = pl.program_id(0)
    xb = x_ref[...].astype(BF16)
    d = MIX_WIDTH
    q_ref[...] = (_dot(xb, w_ref[:, :d]) * (HEAD_DIM ** -0.5 * LOG2E)).astype(BF16)
    k_ref[...] = _dot(xb, w_ref[:, d:2 * d]).astype(BF16)
    _store_values_t(vt_ref, _dot(xb, w_ref[:, 2 * d:]))

    f = _dot(xb, wf_ref[...]) + bf_ref[...]
    log_f = (jnp.minimum(f, 0.0) - jnp.log1p(jnp.exp(-jnp.abs(f)))) * LOG2E
    tm = log_f.shape[0]
    tri = (lax.broadcasted_iota(I32, (tm, tm), 0) >= lax.broadcasted_iota(I32, (tm, tm), 1))
    tri = jnp.where(tri, 1.0, 0.0).astype(BF16)
    p0 = log_f.astype(BF16)
    r1 = log_f - p0.astype(F32)
    p1 = r1.astype(BF16)
    p2 = (r1 - p1.astype(F32)).astype(BF16)
    local = _dot(tri, p0) + _dot(tri, p1) + _dot(tri, p2)

    @pl.when(i % tiles_per_seq == 0)
    def _():
        carry_ref[...] = jnp.zeros_like(carry_ref)

    cum = local + carry_ref[...]
    cum_ref[...] = cum
    carry_ref[...] = cum[tm - 1:tm, :]


def _fox_proj(x2, w, wf, bf, seq):
    n, dm = x2.shape
    d = MIX_WIDTH
    row = lambda width: pl.BlockSpec((TM, width), lambda i: (i, 0))
    return pl.pallas_call(
        functools.partial(_fox_proj_kernel, tiles_per_seq=seq // TM),
        grid=(n // TM,),
        in_specs=[row(dm), _const_spec(w.shape), _const_spec(wf.shape), _const_spec(bf.shape)],
        out_specs=[row(d), row(d), _values_t_spec(seq // TM), row(LANES)],
        out_shape=[jax.ShapeDtypeStruct((n, d), BF16), jax.ShapeDtypeStruct((n, d), BF16),
                   _values_t_shape(n, seq), jax.ShapeDtypeStruct((n, LANES), F32)],
        scratch_shapes=[pltpu.VMEM((1, LANES), F32)],
        compiler_params=_params(1),
        name="fox_proj",
    )(x2, w, wf, bf)


def _mla_proj_kernel(x_ref, wd_ref, gq_ref, gkv_ref, wuq_ref, wuk_ref, wuv_ref,
                     cos_ref, slo_ref, shi_ref, q_ref, k_ref, vt_ref):
    xb = x_ref[...].astype(BF16)
    cos, slo, shi = cos_ref[...], slo_ref[...], shi_ref[...]
    half = MLA_ROPE // 2
    c = _dot(xb, wd_ref[...])
    cq = _rms_norm(c[:, :Q_LORA], gq_ref[...]).astype(BF16)
    ckv = _rms_norm(c[:, Q_LORA:Q_LORA + KV_LORA], gkv_ref[...]).astype(BF16)
    k_rope = _rope_slab(c[:, Q_LORA + KV_LORA:], cos, slo, shi, half)
    scale = (MLA_NOPE + MLA_ROPE) ** -0.5 * LOG2E
    for h in range(N_HEADS):
        sl = slice(h * LANES, (h + 1) * LANES)
        q = _rope_slab(_dot(cq, wuq_ref[:, sl]), cos, slo, shi, half)
        q_ref[:, sl] = (q * scale).astype(BF16)
        k_ref[:, sl] = (_dot(ckv, wuk_ref[:, sl]) + k_rope).astype(BF16)
    _store_values_t(vt_ref, _dot(ckv, wuv_ref[...]))


def _mla_proj(x2, wd, gq, gkv, wuq, wuk, wuv, tabs, seq):
    n, dm = x2.shape
    tiles_per_seq = seq // TM
    row = lambda width: pl.BlockSpec((TM, width), lambda i: (i, 0))
    tab = pl.BlockSpec((TM, LANES), lambda i: (i % tiles_per_seq, 0))
    wide = N_HEADS * LANES
    return pl.pallas_call(
        _mla_proj_kernel,
        grid=(n // TM,),
        in_specs=[row(dm)] + [_const_spec(a.shape) for a in (wd, gq, gkv, wuq, wuk, wuv)] + [tab] * 3,
        out_specs=[row(wide), row(wide), _values_t_spec(tiles_per_seq)],
        out_shape=[jax.ShapeDtypeStruct((n, wide), BF16), jax.ShapeDtypeStruct((n, wide), BF16),
                   _values_t_shape(n, seq)],
        compiler_params=_params(1),
        name="mla_proj",
    )(x2, wd, gq, gkv, wuq, wuk, wuv, *tabs)


def _prep_mla_w(w_dqkv, w_uq, w_ukv):
    dm = w_dqkv.shape[0]
    zeros = lambda r, c: jnp.zeros((r, c), w_dqkv.dtype)
    k_rope_w = w_dqkv[:, Q_LORA + KV_LORA:]
    wd = jnp.concatenate([w_dqkv[:, :Q_LORA + KV_LORA], zeros(dm, MLA_NOPE), k_rope_w,
                          zeros(dm, LANES - MLA_NOPE - MLA_ROPE)], axis=1)
    uq = w_uq.reshape(Q_LORA, N_HEADS, MLA_NOPE + MLA_ROPE)
    uq = jnp.pad(uq, ((0, 0), (0, 0), (0, LANES - MLA_NOPE - MLA_ROPE))).reshape(Q_LORA, N_HEADS * LANES)
    ukv = w_ukv.reshape(KV_LORA, N_HEADS, MLA_NOPE + MLA_V)
    uk = jnp.pad(ukv[:, :, :MLA_NOPE], ((0, 0), (0, 0), (0, LANES - MLA_NOPE))).reshape(KV_LORA, N_HEADS * LANES)
    uv = ukv[:, :, MLA_NOPE:].reshape(KV_LORA, N_HEADS * MLA_V)
    return wd.astype(BF16), uq.astype(BF16), uk.astype(BF16), uv.astype(BF16)


def _attend_slabs(slabs, n_plain, causal, qpos, tk, one_pass):
    m2 = slabs[0][0].shape[0]
    tq = m2 // 2
    kiota = lax.broadcasted_iota(I32, (tk, m2), 0)

    def step(c, carries, masked, stale):
        scores = [_dot_nt(k_load(c), q2) for q2, k_load, _, _ in slabs]
        mask = (c * tk + kiota <= qpos) if masked else None
        out = []
        for (_, _, vt_load, bias_fn), (m, acc), s in zip(slabs, carries, scores):
            if bias_fn is not None:
                s = bias_fn(c) + s
            if mask is not None:
                s = jnp.where(mask, s, NEG)
            m_new = jnp.maximum(m, jnp.max(s, axis=0, keepdims=True))
            alpha = jnp.exp2(m - m_new)
            if stale:
                acc = (acc + _dot(vt_load(c), jnp.exp2(s - m).astype(BF16))) * alpha
            else:
                acc = alpha * acc + _dot(vt_load(c), jnp.exp2(s - m_new).astype(BF16))
            out.append((m_new, acc))
        return tuple(out)

    carries = tuple((jnp.full((1, m2), NEG, F32), jnp.zeros((VT_ROWS, m2), F32)) for _ in slabs)
    if one_pass and causal:
        carries = step(n_plain, carries, True, False)
        carries = lax.fori_loop(0, n_plain, lambda t, cr: step(n_plain - 1 - t, cr, False, True), carries)
    elif one_pass:
        carries = step(0, carries, False, False)
        carries = lax.fori_loop(1, n_plain, lambda c, cr: step(c, cr, False, True), carries)
    else:
        carries = lax.fori_loop(0, n_plain, lambda c, cr: step(c, cr, False, False), carries)
        if causal:
            carries = step(n_plain, carries, True, False)
    row = lax.broadcasted_iota(I32, (LANES, tq), 0)
    outs = []
    for _, acc in carries:
        o = acc[:LANES] / acc[LANES:LANES + 1]
        outs.append(jnp.where(row < HEAD_DIM, o[:, :tq], o[:, tq:]))
    return outs


def _attend_to(o_ref, slabs, n_plain, causal, qpos, tk, one_pass):
    def run(one_pass):
        outs = _attend_slabs(slabs, n_plain, causal, qpos, tk, one_pass)
        for j, out in enumerate(outs):
            o_ref[0, :, j * LANES:(j + 1) * LANES] = out.T.astype(o_ref.dtype)
        return outs

    if not one_pass:
        run(False)
        return
    outs = run(True)
    total = functools.reduce(lambda a, b: a + b, [jnp.sum(o, axis=0, keepdims=True) for o in outs])
    overflowed = jnp.max(jnp.where(jnp.isfinite(total), 0.0, 1.0)) > 0.0

    @pl.when(overflowed)
    def _():
        run(False)


def _split_heads(q_slab):
    lane = lax.broadcasted_iota(I32, q_slab.shape, 1)
    zero = jnp.zeros_like(q_slab)
    return jnp.where(lane < HEAD_DIM, q_slab, zero), jnp.where(lane >= HEAD_DIM, q_slab, zero)


def _kv_chunk(ref, c, tk, lanes):
    return ref[0, pl.ds(pl.multiple_of(c * tk, tk), tk), lanes]


def _per_batch_spec(a):
    nd = a.ndim
    return pl.BlockSpec((1,) + a.shape[1:], lambda bi, i: (bi,) + (0,) * (nd - 1),
                        pipeline_mode=pl.Buffered(1))


def _query_positions(row0, tq):
    pos = row0 + lax.broadcasted_iota(I32, (1, tq), 1)
    return jnp.concatenate([pos, pos], axis=1)


def _fox_attn_kernel(q_ref, k_ref, vt_ref, cq_ref, ck_ref, o_ref, *, tq, tk):
    i = pl.program_id(1)
    row0 = i * tq
    diag = row0 // tk
    qpos = _query_positions(row0, tq)
    cq = cq_ref[0]

    def slab(j):
        lanes = slice(j * LANES, (j + 1) * LANES)
        q2 = jnp.concatenate(_split_heads(q_ref[0, :, lanes]), axis=0)
        cq2 = jnp.concatenate([cq[2 * j:2 * j + 1, :], cq[2 * j + 1:2 * j + 2, :]], axis=1)

        def bias_fn(c):
            ck = ck_ref[0, pl.ds(pl.multiple_of(c * tk, tk), tk), :]
            ck2 = jnp.concatenate([jnp.broadcast_to(ck[:, 2 * j:2 * j + 1], (tk, tq)),
                                   jnp.broadcast_to(ck[:, 2 * j + 1:2 * j + 2], (tk, tq))], axis=1)
            return cq2 - ck2

        return (q2, lambda c: _kv_chunk(k_ref, c, tk, lanes),
                lambda c: vt_ref[0, c, j * VT_ROWS:(j + 1) * VT_ROWS, :], bias_fn)

    _attend_to(o_ref, [slab(j) for j in range(N_SLABS)], diag, True, qpos, tk, True)


def _fox_attn(q, k, vt, cum_q, cum_k):
    b, s, d = q.shape
    tq, tk = min(TQ, s), vt.shape[3]
    return pl.pallas_call(
        functools.partial(_fox_attn_kernel, tq=tq, tk=tk),
        grid=(b, s // tq),
        in_specs=[pl.BlockSpec((1, tq, d), lambda bi, i: (bi, i, 0)),
                  _per_batch_spec(k), _per_batch_spec(vt),
                  pl.BlockSpec((1, N_HEADS, tq), lambda bi, i: (bi, 0, i)),
                  _per_batch_spec(cum_k)],
        out_specs=pl.BlockSpec((1, tq, d), lambda bi, i: (bi, i, 0)),
        out_shape=jax.ShapeDtypeStruct((b, s, d), BF16),
        compiler_params=_params(2),
        name="fox_attn",
    )(q, k, vt, cum_q, cum_k)


def _mla_attn_kernel(q_ref, k_ref, vt_ref, o_ref, *, tq, tk):
    i = pl.program_id(1)
    row0 = i * tq
    diag = row0 // tk
    qpos = _query_positions(row0, tq)
    zero = jnp.zeros((tq, LANES), BF16)

    def slab(j):
        pair = slice(2 * j * LANES, (2 * j + 2) * LANES)
        qa = q_ref[0, :, 2 * j * LANES:(2 * j + 1) * LANES]
        qb = q_ref[0, :, (2 * j + 1) * LANES:(2 * j + 2) * LANES]
        q2 = jnp.concatenate([jnp.concatenate([qa, zero], axis=1),
                              jnp.concatenate([zero, qb], axis=1)], axis=0)
        lanes = slice(j * LANES, (j + 1) * LANES)
        return (q2, lambda c: _kv_chunk(k_ref, c, tk, pair), lambda c: vt_ref[0, c, j * VT_ROWS:(j + 1) * VT_ROWS, :], None)

    _attend_to(o_ref, [slab(j) for j in range(N_SLABS)], diag, True, qpos, tk, True)


def _mla_attn(q, k, vt):
    b, s, wide = q.shape
    d = MIX_WIDTH
    tq, tk = min(TQ, s), vt.shape[3]
    return pl.pallas_call(
        functools.partial(_mla_attn_kernel, tq=tq, tk=tk),
        grid=(b, s // tq),
        in_specs=[pl.BlockSpec((1, tq, wide), lambda bi, i: (bi, i, 0)),
                  _per_batch_spec(k), _per_batch_spec(vt)],
        out_specs=pl.BlockSpec((1, tq, d), lambda bi, i: (bi, i, 0)),
        out_shape=jax.ShapeDtypeStruct((b, s, d), BF16),
        compiler_params=_params(2),
        name="mla_attn",
    )(q, k, vt)


def _sortable_key(score):
    bits = pltpu.bitcast(score + 0.0, I32)
    return jnp.where(bits < 0, bits ^ 0x7FFFFFFF, bits)


def _count_keys(ref, n_chunks, indicator, acc_rows):
    _, tk, tq = ref.shape

    def body(c, acc):
        ind = indicator(ref[c], c)
        for r in range(tk // acc_rows):
            acc = acc + ind[r * acc_rows:(r + 1) * acc_rows]
        return acc

    acc = lax.fori_loop(0, n_chunks, body, jnp.zeros((acc_rows, tq), ref.dtype))
    return jnp.sum(acc.astype(F32), axis=0, keepdims=True)


def _radix_select16(ref, n_chunks, target):
    tq = ref.shape[2]
    one, zero = jnp.int16(1), jnp.int16(0)

    def signed(u):
        return lax.shift_right_arithmetic(lax.shift_left(u ^ 0x8000, 16), 16)

    def bit_step(t, prefix):
        cand = prefix | lax.shift_left(jnp.int32(1), 15 - t)
        cand16 = signed(cand).astype(I16)
        cnt = _count_keys(ref, n_chunks, lambda blk, c: jnp.where(blk >= cand16, one, zero), 64)
        return jnp.where(cnt >= target, cand, prefix)

    return signed(lax.fori_loop(0, 16, bit_step, jnp.zeros((1, tq), I32)))


def _dsa_attn_kernel(q_ref, k_ref, vt_ref, qi_ref, ki_ref, wit_ref, o_ref,
                     qim_ref, key_ref, hi_ref, lo_ref, bias_ref, *, tq, tk, topk):
    i = pl.program_id(1)
    row0 = i * tq
    n_chunks = row0 // tk + 1
    qpos = row0 + lax.broadcasted_iota(I32, (1, tq), 1)
    kiota = lax.broadcasted_iota(I32, (tk, tq), 0)

    for h in range(IDX_HEADS):
        pair = _split_heads(qi_ref[0, :, (h // 2) * LANES:(h // 2 + 1) * LANES])
        qim_ref[h] = pair[h % 2]
    wit = wit_ref[0]

    def score_chunk(c, _):
        ki = _kv_chunk(ki_ref, c, tk, slice(None))
        acc = jnp.zeros((tk, tq), F32)
        for h in range(IDX_HEADS):
            acc = acc + wit[h:h + 1, :] * jnp.maximum(_dot_nt(ki, qim_ref[h]), 0.0)
        key = jnp.where(c * tk + kiota <= qpos, _sortable_key(acc), INT_MIN)
        key_ref[c] = key
        hi_ref[c] = lax.shift_right_arithmetic(key, 16).astype(I16)
        return 0

    lax.fori_loop(0, n_chunks, score_chunk, 0)

    k_f = float(topk)
    one16, zero16 = jnp.int16(1), jnp.int16(0)
    thr_hi = _radix_select16(hi_ref, n_chunks, k_f)
    thr_hi16 = thr_hi.astype(I16)
    n_gt_hi = _count_keys(hi_ref, n_chunks, lambda blk, c: jnp.where(blk > thr_hi16, one16, zero16), 64)

    def low_plane_chunk(c, _):
        key = key_ref[c]
        low = (key & 0xFFFF) - 0x8000
        in_bucket = lax.shift_right_arithmetic(key, 16) == thr_hi
        lo_ref[c] = jnp.where(in_bucket, low, -0x8000).astype(I16)
        return 0

    lax.fori_loop(0, n_chunks, low_plane_chunk, 0)
    thr_lo = _radix_select16(lo_ref, n_chunks, k_f - n_gt_hi)
    thr_lo16 = thr_lo.astype(I16)
    live = thr_hi != -0x8000
    thr = jnp.where(live, thr_hi * 0x10000 + (thr_lo + 0x8000), INT_MIN)
    n_gt = n_gt_hi + _count_keys(lo_ref, n_chunks,
                                 lambda blk, c: jnp.where(blk > thr_lo16, one16, zero16), 64)

    def count(indicator):
        return _count_keys(key_ref, n_chunks, indicator, 32)

    n_eq = count(lambda blk, c: jnp.where(blk == thr, 1, 0))
    need = k_f - n_gt

    def tie_step(t, jmax):
        cand = jmax | lax.shift_left(jnp.int32(1), 12 - t)
        cnt = count(lambda blk, c: jnp.where(
            blk == thr, jnp.where(c * tk + kiota < cand, 1, 0), 0))
        return jnp.where(cnt < need, cand, jmax)

    has_ties = jnp.max(jnp.where(live, n_eq - need, 0.0)) > 0.0
    jmax = lax.cond(has_ties,
                    lambda: lax.fori_loop(0, 13, tie_step, jnp.zeros((1, tq), I32)),
                    lambda: jnp.full((1, tq), 2 ** 13 - 1, I32))
    jmax = jnp.where(live, jmax, -1)

    def bias_chunk(c, _):
        blk = key_ref[c]
        tie_ok = jnp.where(c * tk + kiota <= jmax, 0.0, NEG)
        bias_ref[c] = jnp.where(blk > thr, 0.0, jnp.where(blk == thr, tie_ok, NEG))
        return 0

    lax.fori_loop(0, n_chunks, bias_chunk, 0)

    def bias_fn(c):
        b = bias_ref[c]
        return jnp.concatenate([b, b], axis=1)

    def slab(j):
        lanes = slice(j * LANES, (j + 1) * LANES)
        q2 = jnp.concatenate(_split_heads(q_ref[0, :, lanes]), axis=0)
        return (q2, lambda c: _kv_chunk(k_ref, c, tk, lanes), lambda c: vt_ref[0, c, j * VT_ROWS:(j + 1) * VT_ROWS, :], bias_fn)

    _attend_to(o_ref, [slab(j) for j in range(N_SLABS)], n_chunks, False, None, tk, True)


def _dsa_attn(q, k, vt, qi, ki, wit):
    b, s, d = q.shape
    tq, tk = min(TQ, s), vt.shape[3]
    topk = min(TOPK_MAX, s // 4)
    per_q = lambda width: pl.BlockSpec((1, tq, width), lambda bi, i: (bi, i, 0))
    return pl.pallas_call(
        functools.partial(_dsa_attn_kernel, tq=tq, tk=tk, topk=topk),
        grid=(b, s // tq),
        in_specs=[per_q(d), _per_batch_spec(k), _per_batch_spec(vt),
                  per_q(qi.shape[2]), _per_batch_spec(ki),
                  pl.BlockSpec((1, IDX_HEADS, tq), lambda bi, i: (bi, 0, i))],
        out_specs=per_q(d),
        out_shape=jax.ShapeDtypeStruct((b, s, d), BF16),
        scratch_shapes=[pltpu.VMEM((IDX_HEADS, tq, LANES), BF16),
                        pltpu.VMEM((s // tk, tk, tq), I32),
                        pltpu.VMEM((s // tk, tk, tq), I16),
                        pltpu.VMEM((s // tk, tk, tq), I16),
                        pltpu.VMEM((s // tk, tk, tq), F32)],
        compiler_params=_params(2),
        name="dsa_attn",
    )(q, k, vt, qi, ki, wit)


def _dsa_mixer(x2, b, s, w_in, tabs):
    q, k, vt, qi, ki, wi = _dsa_proj(x2, _prep_dsa_w(w_in), tabs, s)
    r3 = lambda a: a.reshape(b, s, a.shape[1])
    wit = r3(wi)[:, :, :IDX_HEADS].transpose(0, 2, 1)
    o = _dsa_attn(r3(q), r3(k), vt, r3(qi), r3(ki), wit)
    return o.reshape(b * s, MIX_WIDTH)


def _fox_mixer(x2, b, s, w_in, b_f):
    d = MIX_WIDTH
    wf = jnp.pad(w_in[:, 3 * d:], ((0, 0), (0, LANES - N_HEADS))).astype(BF16)
    bf = jnp.pad(b_f, (0, LANES - N_HEADS)).reshape(1, LANES)
    q, k, vt, cum = _fox_proj(x2, w_in[:, :3 * d].astype(BF16), wf, bf, s)
    r3 = lambda a: a.reshape(b, s, a.shape[1])
    cum_q = r3(cum)[:, :, :N_HEADS].transpose(0, 2, 1)
    o = _fox_attn(r3(q), r3(k), vt, cum_q, r3(cum))
    return o.reshape(b * s, d)


def _mla_mixer(x2, b, s, w_dqkv, gq, w_uq, gkv, w_ukv, tabs):
    wd, uq, uk, uv = _prep_mla_w(w_dqkv, w_uq, w_ukv)
    q, k, vt = _mla_proj(x2, wd, gq.reshape(1, -1), gkv.reshape(1, -1), uq, uk, uv, tabs, s)
    r3 = lambda a: a.reshape(b, s, a.shape[1])
    o = _mla_attn(r3(q), r3(k), vt)
    return o.reshape(b * s, MIX_WIDTH)


def kernel(x, ffn1_w13, ffn1_w2, ffn2_w13, ffn2_w2, ln_g, ln_b, w_out, dsa_w_in, fox_w_in, fox_b_f,
           mla_w_dqkv, mla_q_norm_g, mla_w_uq, mla_kv_norm_g, mla_w_ukv):
    b, s, dm = x.shape
    x2 = x.reshape(b * s, dm)
    tabs_p = _rope_tables(s, ROT_DIM, 0, HEAD_DIM)
    tabs_m = _rope_tables(s, MLA_ROPE, MLA_NOPE, LANES)
    for i in range(DEPTH):
        g = lambda r, i=i: ln_g[i, r].reshape(1, dm)
        be = lambda r, i=i: ln_b[i, r].reshape(1, dm)
        x2 = _ffn_ln(x2, ffn1_w13[i], ffn1_w2[i], g(0), be(0))
        kind, j = i % N_MIXERS, i // N_MIXERS
        if kind == 0:
            o = _dsa_mixer(x2, b, s, dsa_w_in[j], tabs_p)
        elif kind == 1:
            o = _fox_mixer(x2, b, s, fox_w_in[j], fox_b_f[j])
        else:
            o = _mla_mixer(x2, b, s, mla_w_dqkv[j], mla_q_norm_g[j], mla_w_uq[j],
                           mla_kv_norm_g[j], mla_w_ukv[j], tabs_m)
        x2 = _outproj_ln(x2, o, w_out[i].astype(BF16), g(1), be(1))
        x2 = _ffn_ln(x2, ffn2_w13[i], ffn2_w2[i], g(2), be(2))
    return x2.reshape(b, s, dm)
```

```python
import functools

import jax
import jax.numpy as jnp
from jax import lax
from jax.experimental import pallas as pl
from jax.experimental.pallas import tpu as pltpu

F32 = jnp.float32
BF16 = jnp.bfloat16
I32 = jnp.int32
I16 = jnp.int16

D_MODEL = 1024
DEPTH = 4
N_MIXERS = 3
HEAD_DIM = 64
N_HEADS = D_MODEL // HEAD_DIM
MIX_WIDTH = N_HEADS * HEAD_DIM
ROT_DIM = HEAD_DIM // 4
ROPE_THETA = 500000.0
IDX_HEADS = 8
IDX_DIM = 64
TOPK_MAX = 256
MLA_NOPE = 64
MLA_ROPE = 32
MLA_V = 64
Q_LORA = 384
KV_LORA = 256
D_FF = 2816
ALPHA = (2.0 * DEPTH) ** 0.25
LN_EPS = 1e-5
RMS_EPS = 1e-6

LANES = 128
N_SLABS = MIX_WIDTH // LANES
VT_ROWS = LANES + 16
LOG2E = 1.4426950408889634
NEG = -1e30
INT_MIN = -2147483648

TM = 512
TF = 256
TQ = 256
VMEM_LIMIT = 56 * 1024 * 1024


def _params(n_axes, vmem=VMEM_LIMIT):
    return pltpu.CompilerParams(dimension_semantics=("arbitrary",) * n_axes, vmem_limit_bytes=vmem)


def _const_spec(shape):
    nd = len(shape)
    return pl.BlockSpec(shape, lambda *_: (0,) * nd)


def _dot(a, b):
    return jnp.dot(a, b, preferred_element_type=F32)


def _dot_nt(a, b):
    return lax.dot_general(a, b, (((1,), (1,)), ((), ())), preferred_element_type=F32)


def _layer_norm(z, g, b):
    mu = jnp.mean(z, axis=-1, keepdims=True)
    d = z - mu
    var = jnp.mean(d * d, axis=-1, keepdims=True)
    return d * lax.rsqrt(var + LN_EPS) * g + b


def _rms_norm(z, g):
    ms = jnp.mean(z * z, axis=-1, keepdims=True)
    return z * lax.rsqrt(ms + RMS_EPS) * g


def _rope_slab(y, cos, sin_lo, sin_hi, half):
    return y * cos + pltpu.roll(y, half, 1) * sin_hi + pltpu.roll(y, LANES - half, 1) * sin_lo


def _store_values_t(vt_ref, v):
    tm = v.shape[0]
    extra = jnp.where(lax.broadcasted_iota(I32, (VT_ROWS - LANES, tm), 0) == 0, 1.0, 0.0).astype(BF16)
    for j in range(N_SLABS):
        vt_ref[0, 0, j * VT_ROWS:j * VT_ROWS + LANES, :] = v[:, j * LANES:(j + 1) * LANES].T.astype(BF16)
        vt_ref[0, 0, j * VT_ROWS + LANES:(j + 1) * VT_ROWS, :] = extra


def _values_t_spec(tiles_per_seq):
    return pl.BlockSpec((1, 1, N_SLABS * VT_ROWS, TM), lambda i: (i // tiles_per_seq, i % tiles_per_seq, 0, 0))


def _values_t_shape(n, seq):
    return jax.ShapeDtypeStruct((n // seq, seq // TM, N_SLABS * VT_ROWS, TM), BF16)


def _ffn_sublayer(x, w13_ref, w2_ref, g, b):
    xb = x.astype(BF16)
    ff = w2_ref.shape[0]
    acc = jnp.zeros(x.shape, F32)
    for c0 in range(0, ff, TF):
        gate = _dot(xb, w13_ref[:, c0:c0 + TF])
        up = _dot(xb, w13_ref[:, ff + c0:ff + c0 + TF])
        h = gate * jax.nn.sigmoid(gate) * up
        acc = acc + _dot(h.astype(BF16), w2_ref[c0:c0 + TF, :])
    return _layer_norm(ALPHA * x + 0.5 * acc, g, b)


def _ffn_ln_kernel(x_ref, w13_ref, w2_ref, g_ref, b_ref, o_ref):
    o_ref[...] = _ffn_sublayer(x_ref[...], w13_ref, w2_ref, g_ref[...], b_ref[...])


def _ffn_ln(x2, w13, w2, g, b):
    n, d = x2.shape
    return pl.pallas_call(
        _ffn_ln_kernel,
        grid=(n // TM,),
        in_specs=[pl.BlockSpec((TM, d), lambda i: (i, 0)),
                  _const_spec(w13.shape), _const_spec(w2.shape),
                  _const_spec(g.shape), _const_spec(b.shape)],
        out_specs=pl.BlockSpec((TM, d), lambda i: (i, 0)),
        out_shape=jax.ShapeDtypeStruct((n, d), F32),
        compiler_params=_params(1),
        name="ffn_ln",
    )(x2, w13.astype(BF16), w2.astype(BF16), g, b)


def _mix_ffn_ln_kernel(x_ref, o_ref, wo_ref, g1_ref, b1_ref, w13_ref, w2_ref, g2_ref, b2_ref, out_ref):
    y = _dot(o_ref[...], wo_ref[...])
    x1 = _layer_norm(ALPHA * x_ref[...] + y, g1_ref[...], b1_ref[...])
    out_ref[...] = _ffn_sublayer(x1, w13_ref, w2_ref, g2_ref[...], b2_ref[...])


def _mix_ffn_ln(x2, o2, wo, g1, b1, w13, w2, g2, b2):
    n, d = x2.shape
    consts = (wo.astype(BF16), g1, b1, w13.astype(BF16), w2.astype(BF16), g2, b2)
    return pl.pallas_call(
        _mix_ffn_ln_kernel,
        grid=(n // TM,),
        in_specs=[pl.BlockSpec((TM, d), lambda i: (i, 0)),
                  pl.BlockSpec((TM, o2.shape[1]), lambda i: (i, 0))] + [_const_spec(a.shape) for a in consts],
        out_specs=pl.BlockSpec((TM, d), lambda i: (i, 0)),
        out_shape=jax.ShapeDtypeStruct((n, d), F32),
        compiler_params=_params(1),
        name="mix_ffn_ln",
    )(x2, o2, *consts)


def _rope_tables(seq, rot, lane_of_rot0, period):
    half = rot // 2
    inv = ROPE_THETA ** (-jnp.arange(0, rot, 2, dtype=F32) / rot)
    ang = jnp.arange(seq, dtype=F32)[:, None] * inv[None, :]
    cos_h, sin_h = jnp.cos(ang), jnp.sin(ang)
    r = (jnp.arange(LANES) % period) - lane_of_rot0
    in_lo = (r >= 0) & (r < half)
    in_hi = (r >= half) & (r < rot)
    idx = jnp.clip(jnp.where(in_hi, r - half, r), 0, half - 1)
    cos = jnp.where((in_lo | in_hi)[None, :], cos_h[:, idx], 1.0)
    sin_lo = jnp.where(in_lo[None, :], -sin_h[:, idx], 0.0)
    sin_hi = jnp.where(in_hi[None, :], sin_h[:, idx], 0.0)
    return cos, sin_lo, sin_hi


def _dsa_proj_kernel(x_ref, w_ref, cos_ref, slo_ref, shi_ref,
                     q_ref, k_ref, vt_ref, qi_ref, ki_ref, wi_ref):
    xb = x_ref[...].astype(BF16)
    cos, slo, shi = cos_ref[...], slo_ref[...], shi_ref[...]
    half = ROT_DIM // 2
    d = MIX_WIDTH
    di = IDX_HEADS * IDX_DIM

    def roped(col0, width, scale, out_ref):
        y = _dot(xb, w_ref[:, col0:col0 + width])
        for j in range(width // LANES):
            s = _rope_slab(y[:, j * LANES:(j + 1) * LANES], cos, slo, shi, half)
            if scale != 1.0:
                s = s * scale
            out_ref[:, j * LANES:(j + 1) * LANES] = s.astype(out_ref.dtype)

    roped(0, d, HEAD_DIM ** -0.5 * LOG2E, q_ref)
    roped(d, d, 1.0, k_ref)
    _store_values_t(vt_ref, _dot(xb, w_ref[:, 2 * d:3 * d]))
    roped(3 * d, di, IDX_DIM ** -0.5, qi_ref)
    roped(3 * d + di, LANES, 1.0, ki_ref)
    wi_ref[...] = _dot(xb, w_ref[:, 3 * d + di + LANES:]) * (IDX_HEADS ** -0.5)


def _dsa_proj(x2, w, tabs, seq):
    n, dm = x2.shape
    d, di = MIX_WIDTH, IDX_HEADS * IDX_DIM
    tiles_per_seq = seq // TM
    row = lambda width: pl.BlockSpec((TM, width), lambda i: (i, 0))
    tab = pl.BlockSpec((TM, LANES), lambda i: (i % tiles_per_seq, 0))
    return pl.pallas_call(
        _dsa_proj_kernel,
        grid=(n // TM,),
        in_specs=[row(dm), _const_spec(w.shape), tab, tab, tab],
        out_specs=[row(d), row(d), _values_t_spec(tiles_per_seq), row(di), row(LANES), row(LANES)],
        out_shape=[jax.ShapeDtypeStruct((n, d), BF16), jax.ShapeDtypeStruct((n, d), BF16),
                   _values_t_shape(n, seq), jax.ShapeDtypeStruct((n, di), BF16),
                   jax.ShapeDtypeStruct((n, LANES), BF16), jax.ShapeDtypeStruct((n, LANES), F32)],
        compiler_params=_params(1),
        name="dsa_proj",
    )(x2, w, *tabs)


def _prep_dsa_w(w_in):
    d, di = MIX_WIDTH, IDX_HEADS * IDX_DIM
    ki = w_in[:, 3 * d + di:3 * d + di + IDX_DIM]
    wi = w_in[:, 3 * d + di + IDX_DIM:]
    pad = jnp.zeros((w_in.shape[0], LANES - IDX_HEADS), w_in.dtype)
    return jnp.concatenate([w_in[:, :3 * d + di], ki, ki, wi, pad], axis=1).astype(BF16)


def _fox_proj_kernel(x_ref, w_ref, wf_ref, bf_ref, q_ref, k_ref, vt_ref, cum_ref, carry_ref,
                     *, tiles_per_seq):
    i = pl.program_id(0)
    xb = x_ref[...].astype(BF16)
    d = MIX_WIDTH
    q_ref[...] = (_dot(xb, w_ref[:, :d]) * (HEAD_DIM ** -0.5 * LOG2E)).astype(BF16)
    k_ref[...] = _dot(xb, w_ref[:, d:2 * d]).astype(BF16)
    _store_values_t(vt_ref, _dot(xb, w_ref[:, 2 * d:]))

    f = _dot(xb, wf_ref[...]) + bf_ref[...]
    log_f = (jnp.minimum(f, 0.0) - jnp.log1p(jnp.exp(-jnp.abs(f)))) * LOG2E
    tm = log_f.shape[0]
    tri = (lax.broadcasted_iota(I32, (tm, tm), 0) >= lax.broadcasted_iota(I32, (tm, tm), 1))
    tri = jnp.where(tri, 1.0, 0.0).astype(BF16)
    p0 = log_f.astype(BF16)
    r1 = log_f - p0.astype(F32)
    p1 = r1.astype(BF16)
    p2 = (r1 - p1.astype(F32)).astype(BF16)
    local = _dot(tri, p0) + _dot(tri, p1) + _dot(tri, p2)

    @pl.when(i % tiles_per_seq == 0)
    def _():
        carry_ref[...] = jnp.zeros_like(carry_ref)

    cum = local + carry_ref[...]
    cum_ref[...] = cum
    carry_ref[...] = cum[tm - 1:tm, :]


def _fox_proj(x2, w, wf, bf, seq):
    n, dm = x2.shape
    d = MIX_WIDTH
    row = lambda width: pl.BlockSpec((TM, width), lambda i: (i, 0))
    return pl.pallas_call(
        functools.partial(_fox_proj_kernel, tiles_per_seq=seq // TM),
        grid=(n // TM,),
        in_specs=[row(dm), _const_spec(w.shape), _const_spec(wf.shape), _const_spec(bf.shape)],
        out_specs=[row(d), row(d), _values_t_spec(seq // TM), row(LANES)],
        out_shape=[jax.ShapeDtypeStruct((n, d), BF16), jax.ShapeDtypeStruct((n, d), BF16),
                   _values_t_shape(n, seq), jax.ShapeDtypeStruct((n, LANES), F32)],
        scratch_shapes=[pltpu.VMEM((1, LANES), F32)],
        compiler_params=_params(1),
        name="fox_proj",
    )(x2, w, wf, bf)


def _mla_proj_kernel(x_ref, wd_ref, gq_ref, gkv_ref, wuq_ref, wuk_ref, wuv_ref,
                     cos_ref, slo_ref, shi_ref, q_ref, k_ref, vt_ref):
    xb = x_ref[...].astype(BF16)
    cos, slo, shi = cos_ref[...], slo_ref[...], shi_ref[...]
    half = MLA_ROPE // 2
    c = _dot(xb, wd_ref[...])
    cq = _rms_norm(c[:, :Q_LORA], gq_ref[...]).astype(BF16)
    ckv = _rms_norm(c[:, Q_LORA:Q_LORA + KV_LORA], gkv_ref[...]).astype(BF16)
    k_rope = _rope_slab(c[:, Q_LORA + KV_LORA:], cos, slo, shi, half)
    scale = (MLA_NOPE + MLA_ROPE) ** -0.5 * LOG2E
    for h in range(N_HEADS):
        sl = slice(h * LANES, (h + 1) * LANES)
        q = _rope_slab(_dot(cq, wuq_ref[:, sl]), cos, slo, shi, half)
        q_ref[:, sl] = (q * scale).astype(BF16)
        k_ref[:, sl] = (_dot(ckv, wuk_ref[:, sl]) + k_rope).astype(BF16)
    _store_values_t(vt_ref, _dot(ckv, wuv_ref[...]))


def _mla_proj(x2, wd, gq, gkv, wuq, wuk, wuv, tabs, seq):
    n, dm = x2.shape
    tiles_per_seq = seq // TM
    row = lambda width: pl.BlockSpec((TM, width), lambda i: (i, 0))
    tab = pl.BlockSpec((TM, LANES), lambda i: (i % tiles_per_seq, 0))
    wide = N_HEADS * LANES
    return pl.pallas_call(
        _mla_proj_kernel,
        grid=(n // TM,),
        in_specs=[row(dm)] + [_const_spec(a.shape) for a in (wd, gq, gkv, wuq, wuk, wuv)] + [tab] * 3,
        out_specs=[row(wide), row(wide), _values_t_spec(tiles_per_seq)],
        out_shape=[jax.ShapeDtypeStruct((n, wide), BF16), jax.ShapeDtypeStruct((n, wide), BF16),
                   _values_t_shape(n, seq)],
        compiler_params=_params(1),
        name="mla_proj",
    )(x2, wd, gq, gkv, wuq, wuk, wuv, *tabs)


def _prep_mla_w(w_dqkv, w_uq, w_ukv):
    dm = w_dqkv.shape[0]
    zeros = lambda r, c: jnp.zeros((r, c), w_dqkv.dtype)
    k_rope_w = w_dqkv[:, Q_LORA + KV_LORA:]
    wd = jnp.concatenate([w_dqkv[:, :Q_LORA + KV_LORA], zeros(dm, MLA_NOPE), k_rope_w,
                          zeros(dm, LANES - MLA_NOPE - MLA_ROPE)], axis=1)
    uq = w_uq.reshape(Q_LORA, N_HEADS, MLA_NOPE + MLA_ROPE)
    uq = jnp.pad(uq, ((0, 0), (0, 0), (0, LANES - MLA_NOPE - MLA_ROPE))).reshape(Q_LORA, N_HEADS * LANES)
    ukv = w_ukv.reshape(KV_LORA, N_HEADS, MLA_NOPE + MLA_V)
    uk = jnp.pad(ukv[:, :, :MLA_NOPE], ((0, 0), (0, 0), (0, LANES - MLA_NOPE))).reshape(KV_LORA, N_HEADS * LANES)
    uv = ukv[:, :, MLA_NOPE:].reshape(KV_LORA, N_HEADS * MLA_V)
    return wd.astype(BF16), uq.astype(BF16), uk.astype(BF16), uv.astype(BF16)


def _attend_slabs(slabs, n_plain, causal, qpos, tk, one_pass):
    m2 = slabs[0][0].shape[0]
    tq = m2 // 2
    kiota = lax.broadcasted_iota(I32, (tk, m2), 0)

    def step(c, carries, masked, stale):
        scores = [_dot_nt(slab[1](c), slab[0]) for slab in slabs]
        mask = (c * tk + kiota <= qpos) if masked else None
        out = []
        for (_, _, vt_load, bias_fn, qbias), (m, acc), s in zip(slabs, carries, scores):
            if bias_fn is not None:
                s = bias_fn(c) + s
            if mask is not None:
                s = jnp.where(mask, s, NEG)
            smax = jnp.max(s, axis=0, keepdims=True)
            m_new = jnp.maximum(m, smax if qbias is None else smax + qbias)
            alpha = jnp.exp2(m - m_new)
            shift = m if stale else m_new
            p = jnp.exp2(s - (shift if qbias is None else shift - qbias)).astype(BF16)
            if stale:
                acc = (acc + _dot(vt_load(c), p)) * alpha
            else:
                acc = alpha * acc + _dot(vt_load(c), p)
            out.append((m_new, acc))
        return tuple(out)

    carries = tuple((jnp.full((1, m2), NEG, F32), jnp.zeros((VT_ROWS, m2), F32)) for _ in slabs)
    if one_pass and causal:
        carries = step(n_plain, carries, True, False)
        carries = lax.fori_loop(0, n_plain, lambda t, cr: step(n_plain - 1 - t, cr, False, True), carries)
    elif one_pass:
        carries = step(0, carries, False, False)
        carries = lax.fori_loop(1, n_plain, lambda c, cr: step(c, cr, False, True), carries)
    else:
        carries = lax.fori_loop(0, n_plain, lambda c, cr: step(c, cr, False, False), carries)
        if causal:
            carries = step(n_plain, carries, True, False)
    row = lax.broadcasted_iota(I32, (LANES, tq), 0)
    outs = []
    for _, acc in carries:
        o = acc[:LANES] / acc[LANES:LANES + 1]
        outs.append(jnp.where(row < HEAD_DIM, o[:, :tq], o[:, tq:]))
    return outs


def _attend_to(o_ref, slabs, n_plain, causal, qpos, tk, one_pass):
    def run(one_pass):
        outs = _attend_slabs(slabs, n_plain, causal, qpos, tk, one_pass)
        for j, out in enumerate(outs):
            o_ref[0, :, j * LANES:(j + 1) * LANES] = out.T.astype(o_ref.dtype)
        return outs

    if not one_pass:
        run(False)
        return
    outs = run(True)
    total = functools.reduce(lambda a, b: a + b, [jnp.sum(o, axis=0, keepdims=True) for o in outs])
    overflowed = jnp.max(jnp.where(jnp.isfinite(total), 0.0, 1.0)) > 0.0

    @pl.when(overflowed)
    def _():
        run(False)


def _split_heads(q_slab):
    lane = lax.broadcasted_iota(I32, q_slab.shape, 1)
    zero = jnp.zeros_like(q_slab)
    return jnp.where(lane < HEAD_DIM, q_slab, zero), jnp.where(lane >= HEAD_DIM, q_slab, zero)


def _kv_chunk(ref, c, tk, lanes):
    return ref[0, pl.ds(pl.multiple_of(c * tk, tk), tk), lanes]


def _per_batch_spec(a):
    nd = a.ndim
    return pl.BlockSpec((1,) + a.shape[1:], lambda bi, i: (bi,) + (0,) * (nd - 1),
                        pipeline_mode=pl.Buffered(1))


def _query_positions(row0, tq):
    pos = row0 + lax.broadcasted_iota(I32, (1, tq), 1)
    return jnp.concatenate([pos, pos], axis=1)


def _fox_attn_kernel(q_ref, k_ref, vt_ref, cq_ref, ck_ref, o_ref, *, tq, tk):
    i = pl.program_id(1)
    row0 = i * tq
    diag = row0 // tk
    qpos = _query_positions(row0, tq)
    cq = cq_ref[0]

    def slab(j):
        lanes = slice(j * LANES, (j + 1) * LANES)
        q2 = jnp.concatenate(_split_heads(q_ref[0, :, lanes]), axis=0)
        cq2 = jnp.concatenate([cq[2 * j:2 * j + 1, :], cq[2 * j + 1:2 * j + 2, :]], axis=1)

        def bias_fn(c):
            ck = -ck_ref[0, pl.ds(pl.multiple_of(c * tk, tk), tk), :]
            return jnp.concatenate([jnp.broadcast_to(ck[:, 2 * j:2 * j + 1], (tk, tq)),
                                    jnp.broadcast_to(ck[:, 2 * j + 1:2 * j + 2], (tk, tq))], axis=1)

        return (q2, lambda c: _kv_chunk(k_ref, c, tk, lanes),
                lambda c: vt_ref[0, c, j * VT_ROWS:(j + 1) * VT_ROWS, :], bias_fn, cq2)

    _attend_to(o_ref, [slab(j) for j in range(N_SLABS)], diag, True, qpos, tk, True)


def _fox_attn(q, k, vt, cum_q, cum_k):
    b, s, d = q.shape
    tq, tk = min(TQ, s), vt.shape[3]
    return pl.pallas_call(
        functools.partial(_fox_attn_kernel, tq=tq, tk=tk),
        grid=(b, s // tq),
        in_specs=[pl.BlockSpec((1, tq, d), lambda bi, i: (bi, i, 0)),
                  _per_batch_spec(k), _per_batch_spec(vt),
                  pl.BlockSpec((1, N_HEADS, tq), lambda bi, i: (bi, 0, i)),
                  _per_batch_spec(cum_k)],
        out_specs=pl.BlockSpec((1, tq, d), lambda bi, i: (bi, i, 0)),
        out_shape=jax.ShapeDtypeStruct((b, s, d), BF16),
        compiler_params=_params(2),
        name="fox_attn",
    )(q, k, vt, cum_q, cum_k)


def _mla_attn_kernel(q_ref, k_ref, vt_ref, o_ref, *, tq, tk):
    i = pl.program_id(1)
    row0 = i * tq
    diag = row0 // tk
    qpos = _query_positions(row0, tq)
    zero = jnp.zeros((tq, LANES), BF16)

    def slab(j):
        pair = slice(2 * j * LANES, (2 * j + 2) * LANES)
        qa = q_ref[0, :, 2 * j * LANES:(2 * j + 1) * LANES]
        qb = q_ref[0, :, (2 * j + 1) * LANES:(2 * j + 2) * LANES]
        q2 = jnp.concatenate([jnp.concatenate([qa, zero], axis=1),
                              jnp.concatenate([zero, qb], axis=1)], axis=0)
        lanes = slice(j * LANES, (j + 1) * LANES)
        return (q2, lambda c: _kv_chunk(k_ref, c, tk, pair),
                lambda c: vt_ref[0, c, j * VT_ROWS:(j + 1) * VT_ROWS, :], None, None)

    _attend_to(o_ref, [slab(j) for j in range(N_SLABS)], diag, True, qpos, tk, True)


def _mla_attn(q, k, vt):
    b, s, wide = q.shape
    d = MIX_WIDTH
    tq, tk = min(TQ, s), vt.shape[3]
    return pl.pallas_call(
        functools.partial(_mla_attn_kernel, tq=tq, tk=tk),
        grid=(b, s // tq),
        in_specs=[pl.BlockSpec((1, tq, wide), lambda bi, i: (bi, i, 0)),
                  _per_batch_spec(k), _per_batch_spec(vt)],
        out_specs=pl.BlockSpec((1, tq, d), lambda bi, i: (bi, i, 0)),
        out_shape=jax.ShapeDtypeStruct((b, s, d), BF16),
        compiler_params=_params(2),
        name="mla_attn",
    )(q, k, vt)


def _sortable_key(score):
    bits = pltpu.bitcast(score + 0.0, I32)
    return jnp.where(bits < 0, bits ^ 0x7FFFFFFF, bits)


def _count_keys(ref, n_chunks, indicator, acc_rows):
    _, tk, tq = ref.shape

    def body(c, acc):
        ind = indicator(ref[c], c)
        for r in range(tk // acc_rows):
            acc = acc + ind[r * acc_rows:(r + 1) * acc_rows]
        return acc

    acc = lax.fori_loop(0, n_chunks, body, jnp.zeros((acc_rows, tq), ref.dtype))
    return jnp.sum(acc.astype(F32), axis=0, keepdims=True)


def _radix_select16(ref, n_chunks, target):
    tq = ref.shape[2]
    one, zero = jnp.int16(1), jnp.int16(0)

    def signed(u):
        return lax.shift_right_arithmetic(lax.shift_left(u ^ 0x8000, 16), 16)

    def bit_step(t, prefix):
        cand = prefix | lax.shift_left(jnp.int32(1), 15 - t)
        cand16 = signed(cand).astype(I16)
        cnt = _count_keys(ref, n_chunks, lambda blk, c: jnp.where(blk >= cand16, one, zero), 64)
        return jnp.where(cnt >= target, cand, prefix)

    return signed(lax.fori_loop(0, 16, bit_step, jnp.zeros((1, tq), I32)))


def _dsa_attn_kernel(q_ref, k_ref, vt_ref, qi_ref, ki_ref, wit_ref, o_ref,
                     qim_ref, key_ref, hi_ref, lo_ref, bias_ref, *, tq, tk, topk):
    i = pl.program_id(1)
    row0 = i * tq
    n_chunks = row0 // tk + 1
    qpos = row0 + lax.broadcasted_iota(I32, (1, tq), 1)
    kiota = lax.broadcasted_iota(I32, (tk, tq), 0)

    for h in range(IDX_HEADS):
        pair = _split_heads(qi_ref[0, :, (h // 2) * LANES:(h // 2 + 1) * LANES])
        qim_ref[h] = pair[h % 2]
    wit = wit_ref[0]

    def score_chunk(c, _):
        ki = _kv_chunk(ki_ref, c, tk, slice(None))
        acc = jnp.zeros((tk, tq), F32)
        for h in range(IDX_HEADS):
            acc = acc + wit[h:h + 1, :] * jnp.maximum(_dot_nt(ki, qim_ref[h]), 0.0)
        key = jnp.where(c * tk + kiota <= qpos, _sortable_key(acc), INT_MIN)
        key_ref[c] = key
        hi_ref[c] = lax.shift_right_arithmetic(key, 16).astype(I16)
        return 0

    lax.fori_loop(0, n_chunks, score_chunk, 0)

    k_f = float(topk)
    one16, zero16 = jnp.int16(1), jnp.int16(0)
    thr_hi = _radix_select16(hi_ref, n_chunks, k_f)
    thr_hi16 = thr_hi.astype(I16)
    n_gt_hi = _count_keys(hi_ref, n_chunks, lambda blk, c: jnp.where(blk > thr_hi16, one16, zero16), 64)

    def low_plane_chunk(c, _):
        key = key_ref[c]
        low = (key & 0xFFFF) - 0x8000
        in_bucket = lax.shift_right_arithmetic(key, 16) == thr_hi
        lo_ref[c] = jnp.where(in_bucket, low, -0x8000).astype(I16)
        return 0

    lax.fori_loop(0, n_chunks, low_plane_chunk, 0)
    thr_lo = _radix_select16(lo_ref, n_chunks, k_f - n_gt_hi)
    thr_lo16 = thr_lo.astype(I16)
    live = thr_hi != -0x8000
    thr = jnp.where(live, thr_hi * 0x10000 + (thr_lo + 0x8000), INT_MIN)
    n_gt = n_gt_hi + _count_keys(lo_ref, n_chunks,
                                 lambda blk, c: jnp.where(blk > thr_lo16, one16, zero16), 64)

    def count(indicator):
        return _count_keys(key_ref, n_chunks, indicator, 32)

    n_eq = count(lambda blk, c: jnp.where(blk == thr, 1, 0))
    need = k_f - n_gt

    def tie_step(t, jmax):
        cand = jmax | lax.shift_left(jnp.int32(1), 12 - t)
        cnt = count(lambda blk, c: jnp.where(
            blk == thr, jnp.where(c * tk + kiota < cand, 1, 0), 0))
        return jnp.where(cnt < need, cand, jmax)

    has_ties = jnp.max(jnp.where(live, n_eq - need, 0.0)) > 0.0
    jmax = lax.cond(has_ties,
                    lambda: lax.fori_loop(0, 13, tie_step, jnp.zeros((1, tq), I32)),
                    lambda: jnp.full((1, tq), 2 ** 13 - 1, I32))
    jmax = jnp.where(live, jmax, -1)

    def bias_chunk(c, _):
        blk = key_ref[c]
        tie_ok = jnp.where(c * tk + kiota <= jmax, 0.0, NEG)
        bias_ref[c] = jnp.where(blk > thr, 0.0, jnp.where(blk == thr, tie_ok, NEG))
        return 0

    lax.fori_loop(0, n_chunks, bias_chunk, 0)

    def bias_fn(c):
        b = bias_ref[c]
        return jnp.concatenate([b, b], axis=1)

    def slab(j):
        lanes = slice(j * LANES, (j + 1) * LANES)
        q2 = jnp.concatenate(_split_heads(q_ref[0, :, lanes]), axis=0)
        return (q2, lambda c: _kv_chunk(k_ref, c, tk, lanes),
                lambda c: vt_ref[0, c, j * VT_ROWS:(j + 1) * VT_ROWS, :], bias_fn, None)

    _attend_to(o_ref, [slab(j) for j in range(N_SLABS)], n_chunks, False, None, tk, True)


def _dsa_attn(q, k, vt, qi, ki, wit):
    b, s, d = q.shape
    tq, tk = min(TQ, s), vt.shape[3]
    topk = min(TOPK_MAX, s // 4)
    per_q = lambda width: pl.BlockSpec((1, tq, width), lambda bi, i: (bi, i, 0))
    return pl.pallas_call(
        functools.partial(_dsa_attn_kernel, tq=tq, tk=tk, topk=topk),
        grid=(b, s // tq),
        in_specs=[per_q(d), _per_batch_spec(k), _per_batch_spec(vt),
                  per_q(qi.shape[2]), _per_batch_spec(ki),
                  pl.BlockSpec((1, IDX_HEADS, tq), lambda bi, i: (bi, 0, i))],
        out_specs=per_q(d),
        out_shape=jax.ShapeDtypeStruct((b, s, d), BF16),
        scratch_shapes=[pltpu.VMEM((IDX_HEADS, tq, LANES), BF16),
                        pltpu.VMEM((s // tk, tk, tq), I32),
                        pltpu.VMEM((s // tk, tk, tq), I16),
                        pltpu.VMEM((s // tk, tk, tq), I16),
                        pltpu.VMEM((s // tk, tk, tq), F32)],
        compiler_params=_params(2),
        name="dsa_attn",
    )(q, k, vt, qi, ki, wit)


def _dsa_mixer(x2, b, s, w_in, tabs):
    q, k, vt, qi, ki, wi = _dsa_proj(x2, _prep_dsa_w(w_in), tabs, s)
    r3 = lambda a: a.reshape(b, s, a.shape[1])
    wit = r3(wi)[:, :, :IDX_HEADS].transpose(0, 2, 1)
    o = _dsa_attn(r3(q), r3(k), vt, r3(qi), r3(ki), wit)
    return o.reshape(b * s, MIX_WIDTH)


def _fox_mixer(x2, b, s, w_in, b_f):
    d = MIX_WIDTH
    wf = jnp.pad(w_in[:, 3 * d:], ((0, 0), (0, LANES - N_HEADS))).astype(BF16)
    bf = jnp.pad(b_f, (0, LANES - N_HEADS)).reshape(1, LANES)
    q, k, vt, cum = _fox_proj(x2, w_in[:, :3 * d].astype(BF16), wf, bf, s)
    r3 = lambda a: a.reshape(b, s, a.shape[1])
    cum_q = r3(cum)[:, :, :N_HEADS].transpose(0, 2, 1)
    o = _fox_attn(r3(q), r3(k), vt, cum_q, r3(cum))
    return o.reshape(b * s, d)


def _mla_mixer(x2, b, s, w_dqkv, gq, w_uq, gkv, w_ukv, tabs):
    wd, uq, uk, uv = _prep_mla_w(w_dqkv, w_uq, w_ukv)
    q, k, vt = _mla_proj(x2, wd, gq.reshape(1, -1), gkv.reshape(1, -1), uq, uk, uv, tabs, s)
    r3 = lambda a: a.reshape(b, s, a.shape[1])
    o = _mla_attn(r3(q), r3(k), vt)
    return o.reshape(b * s, MIX_WIDTH)


def kernel(x, ffn1_w13, ffn1_w2, ffn2_w13, ffn2_w2, ln_g, ln_b, w_out, dsa_w_in, fox_w_in, fox_b_f,
           mla_w_dqkv, mla_q_norm_g, mla_w_uq, mla_kv_norm_g, mla_w_ukv):
    b, s, dm = x.shape
    x2 = x.reshape(b * s, dm)
    tabs_p = _rope_tables(s, ROT_DIM, 0, HEAD_DIM)
    tabs_m = _rope_tables(s, MLA_ROPE, MLA_NOPE, LANES)
    for i in range(DEPTH):
        g = lambda r, i=i: ln_g[i, r].reshape(1, dm)
        be = lambda r, i=i: ln_b[i, r].reshape(1, dm)
        x2 = _ffn_ln(x2, ffn1_w13[i], ffn1_w2[i], g(0), be(0))
        kind, j = i % N_MIXERS, i // N_MIXERS
        if kind == 0:
            o = _dsa_mixer(x2, b, s, dsa_w_in[j], tabs_p)
        elif kind == 1:
            o = _fox_mixer(x2, b, s, fox_w_in[j], fox_b_f[j])
        else:
            o = _mla_mixer(x2, b, s, mla_w_dqkv[j], mla_q_norm_g[j], mla_w_uq[j],
                           mla_kv_norm_g[j], mla_w_ukv[j], tabs_m)
        x2 = _mix_ffn_ln(x2, o, w_out[i], g(1), be(1), ffn2_w13[i], ffn2_w2[i], g(2), be(2))
    return x2.reshape(b, s, dm)
```

```python
import functools

import jax
import jax.numpy as jnp
from jax import lax
from jax.experimental import pallas as pl
from jax.experimental.pallas import tpu as pltpu

F32 = jnp.float32
BF16 = jnp.bfloat16
I32 = jnp.int32
I16 = jnp.int16

D_MODEL = 1024
DEPTH = 4
N_MIXERS = 3
HEAD_DIM = 64
N_HEADS = D_MODEL // HEAD_DIM
MIX_WIDTH = N_HEADS * HEAD_DIM
ROT_DIM = HEAD_DIM // 4
ROPE_THETA = 500000.0
IDX_HEADS = 8
IDX_DIM = 64
TOPK_MAX = 256
MLA_NOPE = 64
MLA_ROPE = 32
MLA_V = 64
Q_LORA = 384
KV_LORA = 256
D_FF = 2816
ALPHA = (2.0 * DEPTH) ** 0.25
LN_EPS = 1e-5
RMS_EPS = 1e-6

LANES = 128
N_SLABS = MIX_WIDTH // LANES
VT_ROWS = LANES + 16
LOG2E = 1.4426950408889634
NEG = -1e30
INT_MIN = -2147483648

TM = 512
TM_FFN = 2 * TM
TF = 256
TQ = 256
SHIFT_KEYS = 64
VMEM_LIMIT = 56 * 1024 * 1024


def _params(n_axes, vmem=VMEM_LIMIT):
    return pltpu.CompilerParams(dimension_semantics=("arbitrary",) * n_axes, vmem_limit_bytes=vmem)


def _const_spec(shape):
    nd = len(shape)
    return pl.BlockSpec(shape, lambda *_: (0,) * nd, pipeline_mode=pl.Buffered(1))


def _dot(a, b):
    return jnp.dot(a, b, preferred_element_type=F32)


def _dot_nt(a, b):
    return lax.dot_general(a, b, (((1,), (1,)), ((), ())), preferred_element_type=F32)


def _layer_norm(z, g, b):
    mu = jnp.mean(z, axis=-1, keepdims=True)
    d = z - mu
    var = jnp.mean(d * d, axis=-1, keepdims=True)
    return d * lax.rsqrt(var + LN_EPS) * g + b


def _rms_norm(z, g):
    ms = jnp.mean(z * z, axis=-1, keepdims=True)
    return z * lax.rsqrt(ms + RMS_EPS) * g


def _rope_slab(y, cos, sin_lo, sin_hi, half):
    return y * cos + pltpu.roll(y, half, 1) * sin_hi + pltpu.roll(y, LANES - half, 1) * sin_lo


def _store_values_t(vt_ref, v):
    tm = v.shape[0]
    extra = jnp.where(lax.broadcasted_iota(I32, (VT_ROWS - LANES, tm), 0) == 0, 1.0, 0.0).astype(BF16)
    for j in range(N_SLABS):
        vt_ref[0, 0, j * VT_ROWS:j * VT_ROWS + LANES, :] = v[:, j * LANES:(j + 1) * LANES].T.astype(BF16)
        vt_ref[0, 0, j * VT_ROWS + LANES:(j + 1) * VT_ROWS, :] = extra


def _values_t_spec(tiles_per_seq):
    return pl.BlockSpec((1, 1, N_SLABS * VT_ROWS, TM), lambda i: (i // tiles_per_seq, i % tiles_per_seq, 0, 0))


def _values_t_shape(n, seq):
    return jax.ShapeDtypeStruct((n // seq, seq // TM, N_SLABS * VT_ROWS, TM), BF16)


def _ffn_sublayer(x, w13_ref, w2_ref, g, b):
    xb = x.astype(BF16)
    ff = w2_ref.shape[0]
    acc = jnp.zeros(x.shape, F32)
    for c0 in range(0, ff, TF):
        gate = _dot(xb, w13_ref[:, c0:c0 + TF])
        up = _dot(xb, w13_ref[:, ff + c0:ff + c0 + TF])
        h = gate * jax.nn.sigmoid(gate) * up
        acc = acc + _dot(h.astype(BF16), w2_ref[c0:c0 + TF, :])
    return _layer_norm(ALPHA * x + 0.5 * acc, g, b)


def _row_tiles(ref):
    return [slice(r, r + TM) for r in range(0, ref.shape[0], TM)]


def _ffn_ln_kernel(x_ref, w13_ref, w2_ref, g_ref, b_ref, o_ref):
    for rows in _row_tiles(x_ref):
        o_ref[rows, :] = _ffn_sublayer(x_ref[rows, :], w13_ref, w2_ref, g_ref[...], b_ref[...])


def _ffn_ln(x2, w13, w2, g, b):
    n, d = x2.shape
    return pl.pallas_call(
        _ffn_ln_kernel,
        grid=(n // TM_FFN,),
        in_specs=[pl.BlockSpec((TM_FFN, d), lambda i: (i, 0)),
                  _const_spec(w13.shape), _const_spec(w2.shape),
                  _const_spec(g.shape), _const_spec(b.shape)],
        out_specs=pl.BlockSpec((TM_FFN, d), lambda i: (i, 0)),
        out_shape=jax.ShapeDtypeStruct((n, d), F32),
        compiler_params=_params(1),
        name="ffn_ln",
    )(x2, w13.astype(BF16), w2.astype(BF16), g, b)


def _mix_ffn_ln_kernel(x_ref, o_ref, wo_ref, g1_ref, b1_ref, w13_ref, w2_ref, g2_ref, b2_ref, out_ref):
    for rows in _row_tiles(x_ref):
        y = _dot(o_ref[rows, :], wo_ref[...])
        x1 = _layer_norm(ALPHA * x_ref[rows, :] + y, g1_ref[...], b1_ref[...])
        out_ref[rows, :] = _ffn_sublayer(x1, w13_ref, w2_ref, g2_ref[...], b2_ref[...])


def _mix_ffn_ln(x2, o2, wo, g1, b1, w13, w2, g2, b2):
    n, d = x2.shape
    consts = (wo.astype(BF16), g1, b1, w13.astype(BF16), w2.astype(BF16), g2, b2)
    return pl.pallas_call(
        _mix_ffn_ln_kernel,
        grid=(n // TM_FFN,),
        in_specs=[pl.BlockSpec((TM_FFN, d), lambda i: (i, 0)),
                  pl.BlockSpec((TM_FFN, o2.shape[1]), lambda i: (i, 0))] + [_const_spec(a.shape) for a in consts],
        out_specs=pl.BlockSpec((TM_FFN, d), lambda i: (i, 0)),
        out_shape=jax.ShapeDtypeStruct((n, d), F32),
        compiler_params=_params(1),
        name="mix_ffn_ln",
    )(x2, o2, *consts)


def _rope_tables(seq, rot, lane_of_rot0, period):
    half = rot // 2
    inv = ROPE_THETA ** (-jnp.arange(0, rot, 2, dtype=F32) / rot)
    ang = jnp.arange(seq, dtype=F32)[:, None] * inv[None, :]
    cos_h, sin_h = jnp.cos(ang), jnp.sin(ang)
    r = (jnp.arange(LANES) % period) - lane_of_rot0
    in_lo = (r >= 0) & (r < half)
    in_hi = (r >= half) & (r < rot)
    idx = jnp.clip(jnp.where(in_hi, r - half, r), 0, half - 1)
    cos = jnp.where((in_lo | in_hi)[None, :], cos_h[:, idx], 1.0)
    sin_lo = jnp.where(in_lo[None, :], -sin_h[:, idx], 0.0)
    sin_hi = jnp.where(in_hi[None, :], sin_h[:, idx], 0.0)
    return cos, sin_lo, sin_hi


def _dsa_proj_kernel(x_ref, w_ref, cos_ref, slo_ref, shi_ref,
                     q_ref, k_ref, vt_ref, qi_ref, ki_ref, wi_ref):
    xb = x_ref[...].astype(BF16)
    cos, slo, shi = cos_ref[...], slo_ref[...], shi_ref[...]
    half = ROT_DIM // 2
    d = MIX_WIDTH
    di = IDX_HEADS * IDX_DIM

    def roped(col0, width, scale, out_ref):
        y = _dot(xb, w_ref[:, col0:col0 + width])
        for j in range(width // LANES):
            s = _rope_slab(y[:, j * LANES:(j + 1) * LANES], cos, slo, shi, half)
            if scale != 1.0:
                s = s * scale
            out_ref[:, j * LANES:(j + 1) * LANES] = s.astype(out_ref.dtype)

    roped(0, d, HEAD_DIM ** -0.5 * LOG2E, q_ref)
    roped(d, d, 1.0, k_ref)
    _store_values_t(vt_ref, _dot(xb, w_ref[:, 2 * d:3 * d]))
    roped(3 * d, di, IDX_DIM ** -0.5, qi_ref)
    roped(3 * d + di, LANES, 1.0, ki_ref)
    wi_ref[...] = _dot(xb, w_ref[:, 3 * d + di + LANES:]) * (IDX_HEADS ** -0.5)


def _dsa_proj(x2, w, tabs, seq):
    n, dm = x2.shape
    d, di = MIX_WIDTH, IDX_HEADS * IDX_DIM
    tiles_per_seq = seq // TM
    row = lambda width: pl.BlockSpec((TM, width), lambda i: (i, 0))
    tab = pl.BlockSpec((TM, LANES), lambda i: (i % tiles_per_seq, 0))
    return pl.pallas_call(
        _dsa_proj_kernel,
        grid=(n // TM,),
        in_specs=[row(dm), _const_spec(w.shape), tab, tab, tab],
        out_specs=[row(d), row(d), _values_t_spec(tiles_per_seq), row(di), row(LANES), row(LANES)],
        out_shape=[jax.ShapeDtypeStruct((n, d), BF16), jax.ShapeDtypeStruct((n, d), BF16),
                   _values_t_shape(n, seq), jax.ShapeDtypeStruct((n, di), BF16),
                   jax.ShapeDtypeStruct((n, LANES), BF16), jax.ShapeDtypeStruct((n, LANES), F32)],
        compiler_params=_params(1),
        name="dsa_proj",
    )(x2, w, *tabs)


def _prep_dsa_w(w_in):
    d, di = MIX_WIDTH, IDX_HEADS * IDX_DIM
    ki = w_in[:, 3 * d + di:3 * d + di + IDX_DIM]
    wi = w_in[:, 3 * d + di + IDX_DIM:]
    pad = jnp.zeros((w_in.shape[0], LANES - IDX_HEADS), w_in.dtype)
    return jnp.concatenate([w_in[:, :3 * d + di], ki, ki, wi, pad], axis=1).astype(BF16)


def _fox_proj_kernel(x_ref, w_ref, wf_ref, bf_ref, q_ref, k_ref, vt_ref, cum_ref, carry_ref,
                     *, tiles_per_seq):
    i = pl.program_id(0)
    xb = x_ref[...].astype(BF16)
    d = MIX_WIDTH
    q_ref[...] = (_dot(xb, w_ref[:, :d]) * (HEAD_DIM ** -0.5 * LOG2E)).astype(BF16)
    k_ref[...] = _dot(xb, w_ref[:, d:2 * d]).astype(BF16)
    _store_values_t(vt_ref, _dot(xb, w_ref[:, 2 * d:]))

    f = _dot(xb, wf_ref[...]) + bf_ref[...]
    log_f = (jnp.minimum(f, 0.0) - jnp.log1p(jnp.exp(-jnp.abs(f)))) * LOG2E
    tm = log_f.shape[0]
    tri = (lax.broadcasted_iota(I32, (tm, tm), 0) >= lax.broadcasted_iota(I32, (tm, tm), 1))
    tri = jnp.where(tri, 1.0, 0.0).astype(BF16)
    p0 = log_f.astype(BF16)
    r1 = log_f - p0.astype(F32)
    p1 = r1.astype(BF16)
    p2 = (r1 - p1.astype(F32)).astype(BF16)
    local = _dot(tri, p0) + _dot(tri, p1) + _dot(tri, p2)

    @pl.when(i % tiles_per_seq == 0)
    def _():
        carry_ref[...] = jnp.zeros_like(carry_ref)

    cum = local + carry_ref[...]
    cum_ref[...] = cum
    carry_ref[...] = cum[tm - 1:tm, :]


def _fox_proj(x2, w, wf, bf, seq):
    n, dm = x2.shape
    d = MIX_WIDTH
    row = lambda width: pl.BlockSpec((TM, width), lambda i: (i, 0))
    return pl.pallas_call(
        functools.partial(_fox_proj_kernel, tiles_per_seq=seq // TM),
        grid=(n // TM,),
        in_specs=[row(dm), _const_spec(w.shape), _const_spec(wf.shape), _const_spec(bf.shape)],
        out_specs=[row(d), row(d), _values_t_spec(seq // TM), row(LANES)],
        out_shape=[jax.ShapeDtypeStruct((n, d), BF16), jax.ShapeDtypeStruct((n, d), BF16),
                   _values_t_shape(n, seq), jax.ShapeDtypeStruct((n, LANES), F32)],
        scratch_shapes=[pltpu.VMEM((1, LANES), F32)],
        compiler_params=_params(1),
        name="fox_proj",
    )(x2, w, wf, bf)


def _mla_proj_kernel(x_ref, wd_ref, gq_ref, gkv_ref, wuq_ref, wuk_ref, wuv_ref,
                     cos_ref, slo_ref, shi_ref, q_ref, k_ref, vt_ref):
    xb = x_ref[...].astype(BF16)
    cos, slo, shi = cos_ref[...], slo_ref[...], shi_ref[...]
    half = MLA_ROPE // 2
    c = _dot(xb, wd_ref[...])
    cq = _rms_norm(c[:, :Q_LORA], gq_ref[...]).astype(BF16)
    ckv = _rms_norm(c[:, Q_LORA:Q_LORA + KV_LORA], gkv_ref[...]).astype(BF16)
    k_rope = _rope_slab(c[:, Q_LORA + KV_LORA:], cos, slo, shi, half)
    scale = (MLA_NOPE + MLA_ROPE) ** -0.5 * LOG2E
    for h in range(N_HEADS):
        sl = slice(h * LANES, (h + 1) * LANES)
        q = _rope_slab(_dot(cq, wuq_ref[:, sl]), cos, slo, shi, half)
        q_ref[:, sl] = (q * scale).astype(BF16)
        k_ref[:, sl] = (_dot(ckv, wuk_ref[:, sl]) + k_rope).astype(BF16)
    _store_values_t(vt_ref, _dot(ckv, wuv_ref[...]))


def _mla_proj(x2, wd, gq, gkv, wuq, wuk, wuv, tabs, seq):
    n, dm = x2.shape
    tiles_per_seq = seq // TM
    row = lambda width: pl.BlockSpec((TM, width), lambda i: (i, 0))
    tab = pl.BlockSpec((TM, LANES), lambda i: (i % tiles_per_seq, 0))
    wide = N_HEADS * LANES
    return pl.pallas_call(
        _mla_proj_kernel,
        grid=(n // TM,),
        in_specs=[row(dm)] + [_const_spec(a.shape) for a in (wd, gq, gkv, wuq, wuk, wuv)] + [tab] * 3,
        out_specs=[row(wide), row(wide), _values_t_spec(tiles_per_seq)],
        out_shape=[jax.ShapeDtypeStruct((n, wide), BF16), jax.ShapeDtypeStruct((n, wide), BF16),
                   _values_t_shape(n, seq)],
        compiler_params=_params(1),
        name="mla_proj",
    )(x2, wd, gq, gkv, wuq, wuk, wuv, *tabs)


def _prep_mla_w(w_dqkv, w_uq, w_ukv):
    dm = w_dqkv.shape[0]
    zeros = lambda r, c: jnp.zeros((r, c), w_dqkv.dtype)
    k_rope_w = w_dqkv[:, Q_LORA + KV_LORA:]
    wd = jnp.concatenate([w_dqkv[:, :Q_LORA + KV_LORA], zeros(dm, MLA_NOPE), k_rope_w,
                          zeros(dm, LANES - MLA_NOPE - MLA_ROPE)], axis=1)
    uq = w_uq.reshape(Q_LORA, N_HEADS, MLA_NOPE + MLA_ROPE)
    uq = jnp.pad(uq, ((0, 0), (0, 0), (0, LANES - MLA_NOPE - MLA_ROPE))).reshape(Q_LORA, N_HEADS * LANES)
    ukv = w_ukv.reshape(KV_LORA, N_HEADS, MLA_NOPE + MLA_V)
    uk = jnp.pad(ukv[:, :, :MLA_NOPE], ((0, 0), (0, 0), (0, LANES - MLA_NOPE))).reshape(KV_LORA, N_HEADS * LANES)
    uv = ukv[:, :, MLA_NOPE:].reshape(KV_LORA, N_HEADS * MLA_V)
    return wd.astype(BF16), uq.astype(BF16), uk.astype(BF16), uv.astype(BF16)


def _attend_slabs(slabs, n_plain, causal, qpos, tk, one_pass):
    m2 = slabs[0][0].shape[0]
    tq = m2 // 2
    kiota = lax.broadcasted_iota(I32, (tk, m2), 0)

    def step(c, carries, masked, stale):
        scores = [_dot_nt(slab[1](c), slab[0]) for slab in slabs]
        mask = (c * tk + kiota <= qpos) if masked else None
        out = []
        for (_, _, vt_load, bias_fn, qbias), (m, acc), s in zip(slabs, carries, scores):
            if bias_fn is not None:
                s = bias_fn(c) + s
            if mask is not None:
                s = jnp.where(mask, s, NEG)
            smax = jnp.max(s, axis=0, keepdims=True)
            m_new = jnp.maximum(m, smax if qbias is None else smax + qbias)
            alpha = jnp.exp2(m - m_new)
            shift = m if stale else m_new
            p = jnp.exp2(s - (shift if qbias is None else shift - qbias)).astype(BF16)
            if stale:
                acc = (acc + _dot(vt_load(c), p)) * alpha
            else:
                acc = alpha * acc + _dot(vt_load(c), p)
            out.append((m_new, acc))
        return tuple(out)

    def first_shift(c, masked):
        sub_iota = lax.broadcasted_iota(I32, (SHIFT_KEYS, m2), 0)
        mask = (c * tk + sub_iota <= qpos) if masked else None
        shifts = []
        for q2, k_load, _, _, qbias in slabs:
            s = _dot_nt(k_load(c, SHIFT_KEYS), q2)
            if mask is not None:
                s = jnp.where(mask, s, NEG)
            smax = jnp.max(s, axis=0, keepdims=True)
            shifts.append(smax if qbias is None else smax + qbias)
        return shifts

    zero_acc = jnp.zeros((VT_ROWS, m2), F32)
    carries = tuple((jnp.full((1, m2), NEG, F32), zero_acc) for _ in slabs)
    if one_pass and causal:
        if slabs[0][3] is None:
            carries = tuple((m0, zero_acc) for m0 in first_shift(n_plain, True))
            carries = step(n_plain, carries, True, True)
        else:
            carries = step(n_plain, carries, True, False)
        carries = lax.fori_loop(0, n_plain, lambda t, cr: step(n_plain - 1 - t, cr, False, True), carries)
    elif one_pass:
        carries = tuple((m0, zero_acc) for m0 in first_shift(0, False))
        carries = lax.fori_loop(0, n_plain, lambda c, cr: step(c, cr, False, True), carries)
    else:
        carries = lax.fori_loop(0, n_plain, lambda c, cr: step(c, cr, False, False), carries)
        if causal:
            carries = step(n_plain, carries, True, False)
    row = lax.broadcasted_iota(I32, (LANES, tq), 0)
    outs = []
    for _, acc in carries:
        o = acc[:LANES] / acc[LANES:LANES + 1]
        outs.append(jnp.where(row < HEAD_DIM, o[:, :tq], o[:, tq:]))
    return outs


def _attend_to(o_ref, slabs, n_plain, causal, qpos, tk, one_pass):
    def run(one_pass):
        outs = _attend_slabs(slabs, n_plain, causal, qpos, tk, one_pass)
        for j, out in enumerate(outs):
            o_ref[0, :, j * LANES:(j + 1) * LANES] = out.T.astype(o_ref.dtype)
        return outs

    if not one_pass:
        run(False)
        return
    outs = run(True)
    total = functools.reduce(lambda a, b: a + b, [jnp.sum(o, axis=0, keepdims=True) for o in outs])
    overflowed = jnp.max(jnp.where(jnp.isfinite(total), 0.0, 1.0)) > 0.0

    @pl.when(overflowed)
    def _():
        run(False)


def _split_heads(q_slab):
    lane = lax.broadcasted_iota(I32, q_slab.shape, 1)
    zero = jnp.zeros_like(q_slab)
    return jnp.where(lane < HEAD_DIM, q_slab, zero), jnp.where(lane >= HEAD_DIM, q_slab, zero)


def _kv_chunk(ref, c, tk, lanes, rows=None):
    return ref[0, pl.ds(pl.multiple_of(c * tk, tk), tk if rows is None else rows), lanes]


def _per_batch_spec(a):
    nd = a.ndim
    return pl.BlockSpec((1,) + a.shape[1:], lambda bi, i: (bi,) + (0,) * (nd - 1),
                        pipeline_mode=pl.Buffered(1))


def _query_positions(row0, tq):
    pos = row0 + lax.broadcasted_iota(I32, (1, tq), 1)
    return jnp.concatenate([pos, pos], axis=1)


def _fox_attn_kernel(q_ref, k_ref, vt_ref, cq_ref, ck_ref, o_ref, *, tq, tk):
    i = pl.program_id(1)
    row0 = i * tq
    diag = row0 // tk
    qpos = _query_positions(row0, tq)
    cq = cq_ref[0]

    def slab(j):
        lanes = slice(j * LANES, (j + 1) * LANES)
        q2 = jnp.concatenate(_split_heads(q_ref[0, :, lanes]), axis=0)
        cq2 = jnp.concatenate([cq[2 * j:2 * j + 1, :], cq[2 * j + 1:2 * j + 2, :]], axis=1)

        def bias_fn(c):
            ck = -ck_ref[0, pl.ds(pl.multiple_of(c * tk, tk), tk), :]
            return jnp.concatenate([jnp.broadcast_to(ck[:, 2 * j:2 * j + 1], (tk, tq)),
                                    jnp.broadcast_to(ck[:, 2 * j + 1:2 * j + 2], (tk, tq))], axis=1)

        return (q2, lambda c, rows=None: _kv_chunk(k_ref, c, tk, lanes, rows),
                lambda c: vt_ref[0, c, j * VT_ROWS:(j + 1) * VT_ROWS, :], bias_fn, cq2)

    _attend_to(o_ref, [slab(j) for j in range(N_SLABS)], diag, True, qpos, tk, True)


def _fox_attn(q, k, vt, cum_q, cum_k):
    b, s, d = q.shape
    tq, tk = min(TQ, s), vt.shape[3]
    return pl.pallas_call(
        functools.partial(_fox_attn_kernel, tq=tq, tk=tk),
        grid=(b, s // tq),
        in_specs=[pl.BlockSpec((1, tq, d), lambda bi, i: (bi, i, 0)),
                  _per_batch_spec(k), _per_batch_spec(vt),
                  pl.BlockSpec((1, N_HEADS, tq), lambda bi, i: (bi, 0, i)),
                  _per_batch_spec(cum_k)],
        out_specs=pl.BlockSpec((1, tq, d), lambda bi, i: (bi, i, 0)),
        out_shape=jax.ShapeDtypeStruct((b, s, d), BF16),
        compiler_params=_params(2),
        name="fox_attn",
    )(q, k, vt, cum_q, cum_k)


def _mla_attn_kernel(q_ref, k_ref, vt_ref, o_ref, *, tq, tk):
    i = pl.program_id(1)
    row0 = i * tq
    diag = row0 // tk
    qpos = _query_positions(row0, tq)
    zero = jnp.zeros((tq, LANES), BF16)

    def slab(j):
        pair = slice(2 * j * LANES, (2 * j + 2) * LANES)
        qa = q_ref[0, :, 2 * j * LANES:(2 * j + 1) * LANES]
        qb = q_ref[0, :, (2 * j + 1) * LANES:(2 * j + 2) * LANES]
        q2 = jnp.concatenate([jnp.concatenate([qa, zero], axis=1),
                              jnp.concatenate([zero, qb], axis=1)], axis=0)
        lanes = slice(j * LANES, (j + 1) * LANES)
        return (q2, lambda c, rows=None: _kv_chunk(k_ref, c, tk, pair, rows),
                lambda c: vt_ref[0, c, j * VT_ROWS:(j + 1) * VT_ROWS, :], None, None)

    _attend_to(o_ref, [slab(j) for j in range(N_SLABS)], diag, True, qpos, tk, True)


def _mla_attn(q, k, vt):
    b, s, wide = q.shape
    d = MIX_WIDTH
    tq, tk = min(TQ, s), vt.shape[3]
    return pl.pallas_call(
        functools.partial(_mla_attn_kernel, tq=tq, tk=tk),
        grid=(b, s // tq),
        in_specs=[pl.BlockSpec((1, tq, wide), lambda bi, i: (bi, i, 0)),
                  _per_batch_spec(k), _per_batch_spec(vt)],
        out_specs=pl.BlockSpec((1, tq, d), lambda bi, i: (bi, i, 0)),
        out_shape=jax.ShapeDtypeStruct((b, s, d), BF16),
        compiler_params=_params(2),
        name="mla_attn",
    )(q, k, vt)


def _sortable_key(score):
    bits = pltpu.bitcast(score + 0.0, I32)
    return jnp.where(bits < 0, bits ^ 0x7FFFFFFF, bits)


def _count_keys(ref, n_chunks, indicator, acc_rows):
    _, tk, tq = ref.shape

    def body(c, acc):
        ind = indicator(ref[c], c)
        for r in range(tk // acc_rows):
            acc = acc + ind[r * acc_rows:(r + 1) * acc_rows]
        return acc

    acc = lax.fori_loop(0, n_chunks, body, jnp.zeros((acc_rows, tq), ref.dtype))
    return jnp.sum(acc.astype(F32), axis=0, keepdims=True)


def _radix_select16(ref, n_chunks, target):
    tq = ref.shape[2]
    one, zero = jnp.int16(1), jnp.int16(0)

    def signed(u):
        return lax.shift_right_arithmetic(lax.shift_left(u ^ 0x8000, 16), 16)

    def bit_step(t, prefix):
        cand = prefix | lax.shift_left(jnp.int32(1), 15 - t)
        cand16 = signed(cand).astype(I16)
        cnt = _count_keys(ref, n_chunks, lambda blk, c: jnp.where(blk >= cand16, one, zero), 64)
        return jnp.where(cnt >= target, cand, prefix)

    return signed(lax.fori_loop(0, 16, bit_step, jnp.zeros((1, tq), I32)))


def _dsa_attn_kernel(q_ref, k_ref, vt_ref, qi_ref, ki_ref, wit_ref, o_ref,
                     qim_ref, key_ref, hi_ref, lo_ref, bias_ref, *, tq, tk, topk):
    i = pl.program_id(1)
    row0 = i * tq
    n_chunks = row0 // tk + 1
    qpos = row0 + lax.broadcasted_iota(I32, (1, tq), 1)
    kiota = lax.broadcasted_iota(I32, (tk, tq), 0)

    for h in range(IDX_HEADS):
        pair = _split_heads(qi_ref[0, :, (h // 2) * LANES:(h // 2 + 1) * LANES])
        qim_ref[h] = pair[h % 2]
    wit = wit_ref[0]

    def score_chunk(c, _):
        ki = _kv_chunk(ki_ref, c, tk, slice(None))
        acc = jnp.zeros((tk, tq), F32)
        for h in range(IDX_HEADS):
            acc = acc + wit[h:h + 1, :] * jnp.maximum(_dot_nt(ki, qim_ref[h]), 0.0)
        key = jnp.where(c * tk + kiota <= qpos, _sortable_key(acc), INT_MIN)
        key_ref[c] = key
        hi_ref[c] = lax.shift_right_arithmetic(key, 16).astype(I16)
        return 0

    lax.fori_loop(0, n_chunks, score_chunk, 0)

    k_f = float(topk)
    one16, zero16 = jnp.int16(1), jnp.int16(0)
    thr_hi = _radix_select16(hi_ref, n_chunks, k_f)
    thr_hi16 = thr_hi.astype(I16)
    n_gt_hi = _count_keys(hi_ref, n_chunks, lambda blk, c: jnp.where(blk > thr_hi16, one16, zero16), 64)

    def low_plane_chunk(c, _):
        key = key_ref[c]
        low = (key & 0xFFFF) - 0x8000
        in_bucket = lax.shift_right_arithmetic(key, 16) == thr_hi
        lo_ref[c] = jnp.where(in_bucket, low, -0x8000).astype(I16)
        return 0

    lax.fori_loop(0, n_chunks, low_plane_chunk, 0)
    thr_lo = _radix_select16(lo_ref, n_chunks, k_f - n_gt_hi)
    thr_lo16 = thr_lo.astype(I16)
    live = thr_hi != -0x8000
    thr = jnp.where(live, thr_hi * 0x10000 + (thr_lo + 0x8000), INT_MIN)
    n_gt = n_gt_hi + _count_keys(lo_ref, n_chunks,
                                 lambda blk, c: jnp.where(blk > thr_lo16, one16, zero16), 64)

    def count(indicator):
        return _count_keys(key_ref, n_chunks, indicator, 32)

    n_eq = count(lambda blk, c: jnp.where(blk == thr, 1, 0))
    need = k_f - n_gt

    def tie_step(t, jmax):
        cand = jmax | lax.shift_left(jnp.int32(1), 12 - t)
        cnt = count(lambda blk, c: jnp.where(
            blk == thr, jnp.where(c * tk + kiota < cand, 1, 0), 0))
        return jnp.where(cnt < need, cand, jmax)

    has_ties = jnp.max(jnp.where(live, n_eq - need, 0.0)) > 0.0
    jmax = lax.cond(has_ties,
                    lambda: lax.fori_loop(0, 13, tie_step, jnp.zeros((1, tq), I32)),
                    lambda: jnp.full((1, tq), 2 ** 13 - 1, I32))
    jmax = jnp.where(live, jmax, -1)

    def bias_chunk(c, _):
        blk = key_ref[c]
        tie_ok = jnp.where(c * tk + kiota <= jmax, 0.0, NEG)
        bias_ref[c] = jnp.where(blk > thr, 0.0, jnp.where(blk == thr, tie_ok, NEG))
        return 0

    thr_all = jnp.where(live, thr, INT_MIN + 1)

    def bias_chunk_no_ties(c, _):
        bias_ref[c] = jnp.where(key_ref[c] >= thr_all, 0.0, NEG)
        return 0

    @pl.when(has_ties)
    def _():
        lax.fori_loop(0, n_chunks, bias_chunk, 0)

    @pl.when(jnp.logical_not(has_ties))
    def _():
        lax.fori_loop(0, n_chunks, bias_chunk_no_ties, 0)

    def bias_fn(c):
        b = bias_ref[c]
        return jnp.concatenate([b, b], axis=1)

    def slab(j):
        lanes = slice(j * LANES, (j + 1) * LANES)
        q2 = jnp.concatenate(_split_heads(q_ref[0, :, lanes]), axis=0)
        return (q2, lambda c, rows=None: _kv_chunk(k_ref, c, tk, lanes, rows),
                lambda c: vt_ref[0, c, j * VT_ROWS:(j + 1) * VT_ROWS, :], bias_fn, None)

    _attend_to(o_ref, [slab(j) for j in range(N_SLABS)], n_chunks, False, None, tk, True)


def _dsa_attn(q, k, vt, qi, ki, wit):
    b, s, d = q.shape
    tq, tk = min(TQ, s), vt.shape[3]
    topk = min(TOPK_MAX, s // 4)
    per_q = lambda width: pl.BlockSpec((1, tq, width), lambda bi, i: (bi, i, 0))
    return pl.pallas_call(
        functools.partial(_dsa_attn_kernel, tq=tq, tk=tk, topk=topk),
        grid=(b, s // tq),
        in_specs=[per_q(d), _per_batch_spec(k), _per_batch_spec(vt),
                  per_q(qi.shape[2]), _per_batch_spec(ki),
                  pl.BlockSpec((1, IDX_HEADS, tq), lambda bi, i: (bi, 0, i))],
        out_specs=per_q(d),
        out_shape=jax.ShapeDtypeStruct((b, s, d), BF16),
        scratch_shapes=[pltpu.VMEM((IDX_HEADS, tq, LANES), BF16),
                        pltpu.VMEM((s // tk, tk, tq), I32),
                        pltpu.VMEM((s // tk, tk, tq), I16),
                        pltpu.VMEM((s // tk, tk, tq), I16),
                        pltpu.VMEM((s // tk, tk, tq), F32)],
        compiler_params=_params(2),
        name="dsa_attn",
    )(q, k, vt, qi, ki, wit)


def _dsa_mixer(x2, b, s, w_in, tabs):
    q, k, vt, qi, ki, wi = _dsa_proj(x2, _prep_dsa_w(w_in), tabs, s)
    r3 = lambda a: a.reshape(b, s, a.shape[1])
    wit = r3(wi)[:, :, :IDX_HEADS].transpose(0, 2, 1)
    o = _dsa_attn(r3(q), r3(k), vt, r3(qi), r3(ki), wit)
    return o.reshape(b * s, MIX_WIDTH)


def _fox_mixer(x2, b, s, w_in, b_f):
    d = MIX_WIDTH
    wf = jnp.pad(w_in[:, 3 * d:], ((0, 0), (0, LANES - N_HEADS))).astype(BF16)
    bf = jnp.pad(b_f, (0, LANES - N_HEADS)).reshape(1, LANES)
    q, k, vt, cum = _fox_proj(x2, w_in[:, :3 * d].astype(BF16), wf, bf, s)
    r3 = lambda a: a.reshape(b, s, a.shape[1])
    cum_q = r3(cum)[:, :, :N_HEADS].transpose(0, 2, 1)
    o = _fox_attn(r3(q), r3(k), vt, cum_q, r3(cum))
    return o.reshape(b * s, d)


def _mla_mixer(x2, b, s, w_dqkv, gq, w_uq, gkv, w_ukv, tabs):
    wd, uq, uk, uv = _prep_mla_w(w_dqkv, w_uq, w_ukv)
    q, k, vt = _mla_proj(x2, wd, gq.reshape(1, -1), gkv.reshape(1, -1), uq, uk, uv, tabs, s)
    r3 = lambda a: a.reshape(b, s, a.shape[1])
    o = _mla_attn(r3(q), r3(k), vt)
    return o.reshape(b * s, MIX_WIDTH)


def kernel(x, ffn1_w13, ffn1_w2, ffn2_w13, ffn2_w2, ln_g, ln_b, w_out, dsa_w_in, fox_w_in, fox_b_f,
           mla_w_dqkv, mla_q_norm_g, mla_w_uq, mla_kv_norm_g, mla_w_ukv):
    b, s, dm = x.shape
    x2 = x.reshape(b * s, dm)
    tabs_p = _rope_tables(s, ROT_DIM, 0, HEAD_DIM)
    tabs_m = _rope_tables(s, MLA_ROPE, MLA_NOPE, LANES)
    for i in range(DEPTH):
        g = lambda r, i=i: ln_g[i, r].reshape(1, dm)
        be = lambda r, i=i: ln_b[i, r].reshape(1, dm)
        x2 = _ffn_ln(x2, ffn1_w13[i], ffn1_w2[i], g(0), be(0))
        kind, j = i % N_MIXERS, i // N_MIXERS
        if kind == 0:
            o = _dsa_mixer(x2, b, s, dsa_w_in[j], tabs_p)
        elif kind == 1:
            o = _fox_mixer(x2, b, s, fox_w_in[j], fox_b_f[j])
        else:
            o = _mla_mixer(x2, b, s, mla_w_dqkv[j], mla_q_norm_g[j], mla_w_uq[j],
                           mla_kv_norm_g[j], mla_w_ukv[j], tabs_m)
        x2 = _mix_ffn_ln(x2, o, w_out[i], g(1), be(1), ffn2_w13[i], ffn2_w2[i], g(2), be(2))
    return x2.reshape(b, s, dm)
```

```python
import functools

import jax
import jax.numpy as jnp
from jax import lax
from jax.experimental import pallas as pl
from jax.experimental.pallas import tpu as pltpu

F32 = jnp.float32
BF16 = jnp.bfloat16
I32 = jnp.int32
I16 = jnp.int16

D_MODEL = 1024
DEPTH = 4
N_MIXERS = 3
HEAD_DIM = 64
N_HEADS = D_MODEL // HEAD_DIM
MIX_WIDTH = N_HEADS * HEAD_DIM
ROT_DIM = HEAD_DIM // 4
ROPE_THETA = 500000.0
IDX_HEADS = 8
IDX_DIM = 64
TOPK_MAX = 256
MLA_NOPE = 64
MLA_ROPE = 32
MLA_V = 64
Q_LORA = 384
KV_LORA = 256
D_FF = 2816
ALPHA = (2.0 * DEPTH) ** 0.25
LN_EPS = 1e-5
RMS_EPS = 1e-6

LANES = 128
N_SLABS = MIX_WIDTH // LANES
VT_ROWS = LANES + 16
LOG2E = 1.4426950408889634
NEG = -1e30
INT_MIN = -2147483648

TM = 512
TM_FFN = 2 * TM
TF = 256
TQ = 256
SHIFT_KEYS = 64
VMEM_LIMIT = 56 * 1024 * 1024


def _params(n_axes, vmem=VMEM_LIMIT):
    return pltpu.CompilerParams(dimension_semantics=("arbitrary",) * n_axes, vmem_limit_bytes=vmem)


def _const_spec(shape):
    nd = len(shape)
    return pl.BlockSpec(shape, lambda *_: (0,) * nd, pipeline_mode=pl.Buffered(1))


def _dot(a, b):
    return jnp.dot(a, b, preferred_element_type=F32)


def _dot_nt(a, b):
    return lax.dot_general(a, b, (((1,), (1,)), ((), ())), preferred_element_type=F32)


def _layer_norm(z, g, b):
    mu = jnp.mean(z, axis=-1, keepdims=True)
    d = z - mu
    var = jnp.mean(d * d, axis=-1, keepdims=True)
    return d * lax.rsqrt(var + LN_EPS) * g + b


def _rms_norm(z, g):
    ms = jnp.mean(z * z, axis=-1, keepdims=True)
    return z * lax.rsqrt(ms + RMS_EPS) * g


def _rope_slab(y, cos, sin_lo, sin_hi, half):
    return y * cos + pltpu.roll(y, half, 1) * sin_hi + pltpu.roll(y, LANES - half, 1) * sin_lo


def _store_values_t(vt_ref, v):
    tm = v.shape[0]
    extra = jnp.where(lax.broadcasted_iota(I32, (VT_ROWS - LANES, tm), 0) == 0, 1.0, 0.0).astype(BF16)
    for j in range(N_SLABS):
        vt_ref[0, 0, j * VT_ROWS:j * VT_ROWS + LANES, :] = v[:, j * LANES:(j + 1) * LANES].T.astype(BF16)
        vt_ref[0, 0, j * VT_ROWS + LANES:(j + 1) * VT_ROWS, :] = extra


def _values_t_spec(tiles_per_seq):
    return pl.BlockSpec((1, 1, N_SLABS * VT_ROWS, TM), lambda i: (i // tiles_per_seq, i % tiles_per_seq, 0, 0))


def _values_t_shape(n, seq):
    return jax.ShapeDtypeStruct((n // seq, seq // TM, N_SLABS * VT_ROWS, TM), BF16)


def _ffn_sublayer(x, w13_ref, w2_ref, g, b):
    xb = x.astype(BF16)
    ff = w2_ref.shape[0]
    acc = jnp.zeros(x.shape, F32)
    for c0 in range(0, ff, TF):
        gate = _dot(xb, w13_ref[:, c0:c0 + TF])
        up = _dot(xb, w13_ref[:, ff + c0:ff + c0 + TF])
        h = gate * jax.nn.sigmoid(gate) * up
        acc = acc + _dot(h.astype(BF16), w2_ref[c0:c0 + TF, :])
    return _layer_norm(ALPHA * x + 0.5 * acc, g, b)


def _row_tiles(ref):
    return [slice(r, r + TM) for r in range(0, ref.shape[0], TM)]


def _ffn_ln_kernel(x_ref, w13_ref, w2_ref, g_ref, b_ref, o_ref):
    for rows in _row_tiles(x_ref):
        o_ref[rows, :] = _ffn_sublayer(x_ref[rows, :], w13_ref, w2_ref, g_ref[...], b_ref[...])


def _ffn_ln(x2, w13, w2, g, b):
    n, d = x2.shape
    return pl.pallas_call(
        _ffn_ln_kernel,
        grid=(n // TM_FFN,),
        in_specs=[pl.BlockSpec((TM_FFN, d), lambda i: (i, 0)),
                  _const_spec(w13.shape), _const_spec(w2.shape),
                  _const_spec(g.shape), _const_spec(b.shape)],
        out_specs=pl.BlockSpec((TM_FFN, d), lambda i: (i, 0)),
        out_shape=jax.ShapeDtypeStruct((n, d), F32),
        compiler_params=_params(1),
        name="ffn_ln",
    )(x2, w13.astype(BF16), w2.astype(BF16), g, b)


def _mix_ffn_ln_kernel(x_ref, o_ref, wo_ref, g1_ref, b1_ref, w13_ref, w2_ref, g2_ref, b2_ref, out_ref):
    for rows in _row_tiles(x_ref):
        y = _dot(o_ref[rows, :], wo_ref[...])
        x1 = _layer_norm(ALPHA * x_ref[rows, :] + y, g1_ref[...], b1_ref[...])
        out_ref[rows, :] = _ffn_sublayer(x1, w13_ref, w2_ref, g2_ref[...], b2_ref[...])


def _mix_ffn_ln(x2, o2, wo, g1, b1, w13, w2, g2, b2):
    n, d = x2.shape
    consts = (wo.astype(BF16), g1, b1, w13.astype(BF16), w2.astype(BF16), g2, b2)
    return pl.pallas_call(
        _mix_ffn_ln_kernel,
        grid=(n // TM_FFN,),
        in_specs=[pl.BlockSpec((TM_FFN, d), lambda i: (i, 0)),
                  pl.BlockSpec((TM_FFN, o2.shape[1]), lambda i: (i, 0))] + [_const_spec(a.shape) for a in consts],
        out_specs=pl.BlockSpec((TM_FFN, d), lambda i: (i, 0)),
        out_shape=jax.ShapeDtypeStruct((n, d), F32),
        compiler_params=_params(1),
        name="mix_ffn_ln",
    )(x2, o2, *consts)


def _rope_tables(seq, rot, lane_of_rot0, period):
    half = rot // 2
    inv = ROPE_THETA ** (-jnp.arange(0, rot, 2, dtype=F32) / rot)
    ang = jnp.arange(seq, dtype=F32)[:, None] * inv[None, :]
    cos_h, sin_h = jnp.cos(ang), jnp.sin(ang)
    r = (jnp.arange(LANES) % period) - lane_of_rot0
    in_lo = (r >= 0) & (r < half)
    in_hi = (r >= half) & (r < rot)
    idx = jnp.clip(jnp.where(in_hi, r - half, r), 0, half - 1)
    cos = jnp.where((in_lo | in_hi)[None, :], cos_h[:, idx], 1.0)
    sin_lo = jnp.where(in_lo[None, :], -sin_h[:, idx], 0.0)
    sin_hi = jnp.where(in_hi[None, :], sin_h[:, idx], 0.0)
    return cos, sin_lo, sin_hi


def _dsa_proj_kernel(x_ref, w_ref, cos_ref, slo_ref, shi_ref,
                     q_ref, k_ref, vt_ref, qi_ref, ki_ref, wi_ref):
    xb = x_ref[...].astype(BF16)
    cos, slo, shi = cos_ref[...], slo_ref[...], shi_ref[...]
    half = ROT_DIM // 2
    d = MIX_WIDTH
    di = IDX_HEADS * IDX_DIM

    def roped(col0, width, scale, out_ref):
        y = _dot(xb, w_ref[:, col0:col0 + width])
        for j in range(width // LANES):
            s = _rope_slab(y[:, j * LANES:(j + 1) * LANES], cos, slo, shi, half)
            if scale != 1.0:
                s = s * scale
            out_ref[:, j * LANES:(j + 1) * LANES] = s.astype(out_ref.dtype)

    roped(0, d, HEAD_DIM ** -0.5 * LOG2E, q_ref)
    roped(d, d, 1.0, k_ref)
    _store_values_t(vt_ref, _dot(xb, w_ref[:, 2 * d:3 * d]))
    roped(3 * d, di, IDX_DIM ** -0.5, qi_ref)
    roped(3 * d + di, LANES, 1.0, ki_ref)
    wi_ref[...] = _dot(xb, w_ref[:, 3 * d + di + LANES:]) * (IDX_HEADS ** -0.5)


def _dsa_proj(x2, w, tabs, seq):
    n, dm = x2.shape
    d, di = MIX_WIDTH, IDX_HEADS * IDX_DIM
    tiles_per_seq = seq // TM
    row = lambda width: pl.BlockSpec((TM, width), lambda i: (i, 0))
    tab = pl.BlockSpec((TM, LANES), lambda i: (i % tiles_per_seq, 0))
    return pl.pallas_call(
        _dsa_proj_kernel,
        grid=(n // TM,),
        in_specs=[row(dm), _const_spec(w.shape), tab, tab, tab],
        out_specs=[row(d), row(d), _values_t_spec(tiles_per_seq), row(di), row(LANES), row(LANES)],
        out_shape=[jax.ShapeDtypeStruct((n, d), BF16), jax.ShapeDtypeStruct((n, d), BF16),
                   _values_t_shape(n, seq), jax.ShapeDtypeStruct((n, di), BF16),
                   jax.ShapeDtypeStruct((n, LANES), BF16), jax.ShapeDtypeStruct((n, LANES), F32)],
        compiler_params=_params(1),
        name="dsa_proj",
    )(x2, w, *tabs)


def _prep_dsa_w(w_in):
    d, di = MIX_WIDTH, IDX_HEADS * IDX_DIM
    ki = w_in[:, 3 * d + di:3 * d + di + IDX_DIM]
    wi = w_in[:, 3 * d + di + IDX_DIM:]
    pad = jnp.zeros((w_in.shape[0], LANES - IDX_HEADS), w_in.dtype)
    return jnp.concatenate([w_in[:, :3 * d + di], ki, ki, wi, pad], axis=1).astype(BF16)


def _fox_proj_kernel(x_ref, w_ref, wf_ref, bf_ref, q_ref, k_ref, vt_ref, cum_ref, carry_ref,
                     *, tiles_per_seq):
    i = pl.program_id(0)
    xb = x_ref[...].astype(BF16)
    d = MIX_WIDTH
    q_ref[...] = (_dot(xb, w_ref[:, :d]) * (HEAD_DIM ** -0.5 * LOG2E)).astype(BF16)
    k_ref[...] = _dot(xb, w_ref[:, d:2 * d]).astype(BF16)
    _store_values_t(vt_ref, _dot(xb, w_ref[:, 2 * d:]))

    f = _dot(xb, wf_ref[...]) + bf_ref[...]
    log_f = (jnp.minimum(f, 0.0) - jnp.log1p(jnp.exp(-jnp.abs(f)))) * LOG2E
    tm = log_f.shape[0]
    tri = (lax.broadcasted_iota(I32, (tm, tm), 0) >= lax.broadcasted_iota(I32, (tm, tm), 1))
    tri = jnp.where(tri, 1.0, 0.0).astype(BF16)
    p0 = log_f.astype(BF16)
    r1 = log_f - p0.astype(F32)
    p1 = r1.astype(BF16)
    p2 = (r1 - p1.astype(F32)).astype(BF16)
    local = _dot(tri, p0) + _dot(tri, p1) + _dot(tri, p2)

    @pl.when(i % tiles_per_seq == 0)
    def _():
        carry_ref[...] = jnp.zeros_like(carry_ref)

    cum = local + carry_ref[...]
    cum_ref[...] = cum
    carry_ref[...] = cum[tm - 1:tm, :]


def _fox_proj(x2, w, wf, bf, seq):
    n, dm = x2.shape
    d = MIX_WIDTH
    row = lambda width: pl.BlockSpec((TM, width), lambda i: (i, 0))
    return pl.pallas_call(
        functools.partial(_fox_proj_kernel, tiles_per_seq=seq // TM),
        grid=(n // TM,),
        in_specs=[row(dm), _const_spec(w.shape), _const_spec(wf.shape), _const_spec(bf.shape)],
        out_specs=[row(d), row(d), _values_t_spec(seq // TM), row(LANES)],
        out_shape=[jax.ShapeDtypeStruct((n, d), BF16), jax.ShapeDtypeStruct((n, d), BF16),
                   _values_t_shape(n, seq), jax.ShapeDtypeStruct((n, LANES), F32)],
        scratch_shapes=[pltpu.VMEM((1, LANES), F32)],
        compiler_params=_params(1),
        name="fox_proj",
    )(x2, w, wf, bf)


def _mla_proj_kernel(x_ref, wd_ref, gq_ref, gkv_ref, wuq_ref, wuk_ref, wuv_ref,
                     cos_ref, slo_ref, shi_ref, q_ref, k_ref, vt_ref):
    xb = x_ref[...].astype(BF16)
    cos, slo, shi = cos_ref[...], slo_ref[...], shi_ref[...]
    half = MLA_ROPE // 2
    c = _dot(xb, wd_ref[...])
    cq = _rms_norm(c[:, :Q_LORA], gq_ref[...]).astype(BF16)
    ckv = _rms_norm(c[:, Q_LORA:Q_LORA + KV_LORA], gkv_ref[...]).astype(BF16)
    k_rope = _rope_slab(c[:, Q_LORA + KV_LORA:], cos, slo, shi, half)
    scale = (MLA_NOPE + MLA_ROPE) ** -0.5 * LOG2E
    for h in range(N_HEADS):
        sl = slice(h * LANES, (h + 1) * LANES)
        q = _rope_slab(_dot(cq, wuq_ref[:, sl]), cos, slo, shi, half)
        q_ref[:, sl] = (q * scale).astype(BF16)
        k_ref[:, sl] = (_dot(ckv, wuk_ref[:, sl]) + k_rope).astype(BF16)
    _store_values_t(vt_ref, _dot(ckv, wuv_ref[...]))


def _mla_proj(x2, wd, gq, gkv, wuq, wuk, wuv, tabs, seq):
    n, dm = x2.shape
    tiles_per_seq = seq // TM
    row = lambda width: pl.BlockSpec((TM, width), lambda i: (i, 0))
    tab = pl.BlockSpec((TM, LANES), lambda i: (i % tiles_per_seq, 0))
    wide = N_HEADS * LANES
    return pl.pallas_call(
        _mla_proj_kernel,
        grid=(n // TM,),
        in_specs=[row(dm)] + [_const_spec(a.shape) for a in (wd, gq, gkv, wuq, wuk, wuv)] + [tab] * 3,
        out_specs=[row(wide), row(wide), _values_t_spec(tiles_per_seq)],
        out_shape=[jax.ShapeDtypeStruct((n, wide), BF16), jax.ShapeDtypeStruct((n, wide), BF16),
                   _values_t_shape(n, seq)],
        compiler_params=_params(1),
        name="mla_proj",
    )(x2, wd, gq, gkv, wuq, wuk, wuv, *tabs)


def _prep_mla_w(w_dqkv, w_uq, w_ukv):
    dm = w_dqkv.shape[0]
    zeros = lambda r, c: jnp.zeros((r, c), w_dqkv.dtype)
    k_rope_w = w_dqkv[:, Q_LORA + KV_LORA:]
    wd = jnp.concatenate([w_dqkv[:, :Q_LORA + KV_LORA], zeros(dm, MLA_NOPE), k_rope_w,
                          zeros(dm, LANES - MLA_NOPE - MLA_ROPE)], axis=1)
    uq = w_uq.reshape(Q_LORA, N_HEADS, MLA_NOPE + MLA_ROPE)
    uq = jnp.pad(uq, ((0, 0), (0, 0), (0, LANES - MLA_NOPE - MLA_ROPE))).reshape(Q_LORA, N_HEADS * LANES)
    ukv = w_ukv.reshape(KV_LORA, N_HEADS, MLA_NOPE + MLA_V)
    uk = jnp.pad(ukv[:, :, :MLA_NOPE], ((0, 0), (0, 0), (0, LANES - MLA_NOPE))).reshape(KV_LORA, N_HEADS * LANES)
    uv = ukv[:, :, MLA_NOPE:].reshape(KV_LORA, N_HEADS * MLA_V)
    return wd.astype(BF16), uq.astype(BF16), uk.astype(BF16), uv.astype(BF16)


def _attend_slabs(slabs, n_plain, causal, qpos, tk, one_pass):
    m2 = slabs[0][0].shape[0]
    tq = m2 // 2
    kiota = lax.broadcasted_iota(I32, (tk, m2), 0)

    def step(c, carries, masked, stale):
        scores = [_dot_nt(slab[1](c), slab[0]) for slab in slabs]
        mask = (c * tk + kiota <= qpos) if masked else None
        out = []
        for (_, _, vt_load, bias_fn, qbias), (m, acc), s in zip(slabs, carries, scores):
            if bias_fn is not None:
                s = bias_fn(c) + s
            if mask is not None:
                s = jnp.where(mask, s, NEG)
            smax = jnp.max(s, axis=0, keepdims=True)
            m_new = jnp.maximum(m, smax if qbias is None else smax + qbias)
            alpha = jnp.exp2(m - m_new)
            shift = m if stale else m_new
            p = jnp.exp2(s - (shift if qbias is None else shift - qbias)).astype(BF16)
            if stale:
                acc = (acc + _dot(vt_load(c), p)) * alpha
            else:
                acc = alpha * acc + _dot(vt_load(c), p)
            out.append((m_new, acc))
        return tuple(out)

    def first_shift(c, masked):
        sub_iota = lax.broadcasted_iota(I32, (SHIFT_KEYS, m2), 0)
        mask = (c * tk + sub_iota <= qpos) if masked else None
        shifts = []
        for q2, k_load, _, _, qbias in slabs:
            s = _dot_nt(k_load(c, SHIFT_KEYS), q2)
            if mask is not None:
                s = jnp.where(mask, s, NEG)
            smax = jnp.max(s, axis=0, keepdims=True)
            shifts.append(smax if qbias is None else smax + qbias)
        return shifts

    zero_acc = jnp.zeros((VT_ROWS, m2), F32)
    carries = tuple((jnp.full((1, m2), NEG, F32), zero_acc) for _ in slabs)
    if one_pass and causal:
        if slabs[0][3] is None:
            carries = tuple((m0, zero_acc) for m0 in first_shift(n_plain, True))
            carries = step(n_plain, carries, True, True)
        else:
            carries = step(n_plain, carries, True, False)
        carries = lax.fori_loop(0, n_plain, lambda t, cr: step(n_plain - 1 - t, cr, False, True), carries)
    elif one_pass:
        carries = tuple((m0, zero_acc) for m0 in first_shift(0, False))
        carries = lax.fori_loop(0, n_plain, lambda c, cr: step(c, cr, False, True), carries)
    else:
        carries = lax.fori_loop(0, n_plain, lambda c, cr: step(c, cr, False, False), carries)
        if causal:
            carries = step(n_plain, carries, True, False)
    row = lax.broadcasted_iota(I32, (LANES, tq), 0)
    outs = []
    for _, acc in carries:
        o = acc[:LANES] / acc[LANES:LANES + 1]
        outs.append(jnp.where(row < HEAD_DIM, o[:, :tq], o[:, tq:]))
    return outs


def _attend_to(o_ref, slabs, n_plain, causal, qpos, tk, one_pass):
    def run(one_pass):
        outs = _attend_slabs(slabs, n_plain, causal, qpos, tk, one_pass)
        for j, out in enumerate(outs):
            o_ref[0, :, j * LANES:(j + 1) * LANES] = out.T.astype(o_ref.dtype)
        return outs

    if not one_pass:
        run(False)
        return
    outs = run(True)
    total = functools.reduce(lambda a, b: a + b, [jnp.sum(o, axis=0, keepdims=True) for o in outs])
    overflowed = jnp.max(jnp.where(jnp.isfinite(total), 0.0, 1.0)) > 0.0

    @pl.when(overflowed)
    def _():
        run(False)


def _split_heads(q_slab):
    lane = lax.broadcasted_iota(I32, q_slab.shape, 1)
    zero = jnp.zeros_like(q_slab)
    return jnp.where(lane < HEAD_DIM, q_slab, zero), jnp.where(lane >= HEAD_DIM, q_slab, zero)


def _kv_chunk(ref, c, tk, lanes, rows=None):
    return ref[0, pl.ds(pl.multiple_of(c * tk, tk), tk if rows is None else rows), lanes]


def _per_batch_spec(a):
    nd = a.ndim
    return pl.BlockSpec((1,) + a.shape[1:], lambda bi, i: (bi,) + (0,) * (nd - 1),
                        pipeline_mode=pl.Buffered(1))


def _query_positions(row0, tq):
    pos = row0 + lax.broadcasted_iota(I32, (1, tq), 1)
    return jnp.concatenate([pos, pos], axis=1)


def _fox_attn_kernel(q_ref, k_ref, vt_ref, cq_ref, ck_ref, o_ref, *, tq, tk):
    i = pl.program_id(1)
    row0 = i * tq
    diag = row0 // tk
    qpos = _query_positions(row0, tq)
    cq = cq_ref[0]

    def slab(j):
        lanes = slice(j * LANES, (j + 1) * LANES)
        q2 = jnp.concatenate(_split_heads(q_ref[0, :, lanes]), axis=0)
        cq2 = jnp.concatenate([cq[2 * j:2 * j + 1, :], cq[2 * j + 1:2 * j + 2, :]], axis=1)

        def bias_fn(c):
            ck = -ck_ref[0, pl.ds(pl.multiple_of(c * tk, tk), tk), :]
            return jnp.concatenate([jnp.broadcast_to(ck[:, 2 * j:2 * j + 1], (tk, tq)),
                                    jnp.broadcast_to(ck[:, 2 * j + 1:2 * j + 2], (tk, tq))], axis=1)

        return (q2, lambda c, rows=None: _kv_chunk(k_ref, c, tk, lanes, rows),
                lambda c: vt_ref[0, c, j * VT_ROWS:(j + 1) * VT_ROWS, :], bias_fn, cq2)

    _attend_to(o_ref, [slab(j) for j in range(N_SLABS)], diag, True, qpos, tk, True)


def _fox_attn(q, k, vt, cum_q, cum_k):
    b, s, d = q.shape
    tq, tk = min(TQ, s), vt.shape[3]
    return pl.pallas_call(
        functools.partial(_fox_attn_kernel, tq=tq, tk=tk),
        grid=(b, s // tq),
        in_specs=[pl.BlockSpec((1, tq, d), lambda bi, i: (bi, i, 0)),
                  _per_batch_spec(k), _per_batch_spec(vt),
                  pl.BlockSpec((1, N_HEADS, tq), lambda bi, i: (bi, 0, i)),
                  _per_batch_spec(cum_k)],
        out_specs=pl.BlockSpec((1, tq, d), lambda bi, i: (bi, i, 0)),
        out_shape=jax.ShapeDtypeStruct((b, s, d), BF16),
        compiler_params=_params(2),
        name="fox_attn",
    )(q, k, vt, cum_q, cum_k)


def _mla_attn_kernel(q_ref, k_ref, vt_ref, o_ref, *, tq, tk):
    i = pl.program_id(1)
    row0 = i * tq
    diag = row0 // tk
    qpos = _query_positions(row0, tq)
    zero = jnp.zeros((tq, LANES), BF16)

    def slab(j):
        pair = slice(2 * j * LANES, (2 * j + 2) * LANES)
        qa = q_ref[0, :, 2 * j * LANES:(2 * j + 1) * LANES]
        qb = q_ref[0, :, (2 * j + 1) * LANES:(2 * j + 2) * LANES]
        q2 = jnp.concatenate([jnp.concatenate([qa, zero], axis=1),
                              jnp.concatenate([zero, qb], axis=1)], axis=0)
        lanes = slice(j * LANES, (j + 1) * LANES)
        return (q2, lambda c, rows=None: _kv_chunk(k_ref, c, tk, pair, rows),
                lambda c: vt_ref[0, c, j * VT_ROWS:(j + 1) * VT_ROWS, :], None, None)

    _attend_to(o_ref, [slab(j) for j in range(N_SLABS)], diag, True, qpos, tk, True)


def _mla_attn(q, k, vt):
    b, s, wide = q.shape
    d = MIX_WIDTH
    tq, tk = min(TQ, s), vt.shape[3]
    return pl.pallas_call(
        functools.partial(_mla_attn_kernel, tq=tq, tk=tk),
        grid=(b, s // tq),
        in_specs=[pl.BlockSpec((1, tq, wide), lambda bi, i: (bi, i, 0)),
                  _per_batch_spec(k), _per_batch_spec(vt)],
        out_specs=pl.BlockSpec((1, tq, d), lambda bi, i: (bi, i, 0)),
        out_shape=jax.ShapeDtypeStruct((b, s, d), BF16),
        compiler_params=_params(2),
        name="mla_attn",
    )(q, k, vt)


def _sortable_key(score):
    bits = pltpu.bitcast(score + 0.0, I32)
    return jnp.where(bits < 0, bits ^ 0x7FFFFFFF, bits)


def _count_keys(ref, n_chunks, indicator, acc_rows):
    _, tk, tq = ref.shape

    def body(c, acc):
        ind = indicator(ref[c], c)
        for r in range(tk // acc_rows):
            acc = acc + ind[r * acc_rows:(r + 1) * acc_rows]
        return acc

    acc = lax.fori_loop(0, n_chunks, body, jnp.zeros((acc_rows, tq), ref.dtype))
    return jnp.sum(acc.astype(F32), axis=0, keepdims=True)


def _radix_select16(ref, n_chunks, target):
    tq = ref.shape[2]
    one, zero = jnp.int16(1), jnp.int16(0)

    def signed(u):
        return lax.shift_right_arithmetic(lax.shift_left(u ^ 0x8000, 16), 16)

    def bit_step(t, prefix):
        cand = prefix | lax.shift_left(jnp.int32(1), 15 - t)
        cand16 = signed(cand).astype(I16)
        cnt = _count_keys(ref, n_chunks, lambda blk, c: jnp.where(blk >= cand16, one, zero), 64)
        return jnp.where(cnt >= target, cand, prefix)

    return signed(lax.fori_loop(0, 16, bit_step, jnp.zeros((1, tq), I32)))


def _dsa_attn_kernel(q_ref, k_ref, vt_ref, qi_ref, ki_ref, wit_ref, tri_ref, o_ref,
                     qim_ref, key_ref, hi_ref, lo_ref, bias_ref, *, tq, tk, topk):
    i = pl.program_id(1)
    row0 = i * tq
    n_chunks = row0 // tk + 1
    qpos = row0 + lax.broadcasted_iota(I32, (1, tq), 1)
    kiota = lax.broadcasted_iota(I32, (tk, tq), 0)

    for h in range(IDX_HEADS):
        pair = _split_heads(qi_ref[0, :, (h // 2) * LANES:(h // 2 + 1) * LANES])
        qim_ref[h] = pair[h % 2]
    wit = wit_ref[0]

    def score_chunk(c, _):
        ki = _kv_chunk(ki_ref, c, tk, slice(None))
        acc = jnp.zeros((tk, tq), F32)
        for h in range(IDX_HEADS):
            acc = acc + wit[h:h + 1, :] * jnp.maximum(_dot_nt(ki, qim_ref[h]), 0.0)
        key = jnp.where(c * tk + kiota <= qpos, _sortable_key(acc), INT_MIN)
        key_ref[c] = key
        hi_ref[c] = lax.shift_right_arithmetic(key, 16).astype(I16)
        return 0

    lax.fori_loop(0, n_chunks, score_chunk, 0)

    k_f = float(topk)
    one16, zero16 = jnp.int16(1), jnp.int16(0)
    thr_hi = _radix_select16(hi_ref, n_chunks, k_f)
    thr_hi16 = thr_hi.astype(I16)
    n_gt_hi = _count_keys(hi_ref, n_chunks, lambda blk, c: jnp.where(blk > thr_hi16, one16, zero16), 64)

    def low_plane_chunk(c, _):
        key = key_ref[c]
        low = (key & 0xFFFF) - 0x8000
        in_bucket = lax.shift_right_arithmetic(key, 16) == thr_hi
        lo_ref[c] = jnp.where(in_bucket, low, -0x8000).astype(I16)
        return 0

    lax.fori_loop(0, n_chunks, low_plane_chunk, 0)
    thr_lo = _radix_select16(lo_ref, n_chunks, k_f - n_gt_hi)
    thr_lo16 = thr_lo.astype(I16)
    live = thr_hi != -0x8000
    thr = jnp.where(live, thr_hi * 0x10000 + (thr_lo + 0x8000), INT_MIN)
    n_gt = n_gt_hi + _count_keys(lo_ref, n_chunks,
                                 lambda blk, c: jnp.where(blk > thr_lo16, one16, zero16), 64)

    def count(indicator):
        return _count_keys(key_ref, n_chunks, indicator, 32)

    n_eq = count(lambda blk, c: jnp.where(blk == thr, 1, 0))
    need = k_f - n_gt

    has_ties = jnp.max(jnp.where(live, n_eq - need, 0.0)) > 0.0

    need_ties = jnp.where(live, need, 0.0)

    def bias_chunk_ties(c, before):
        blk = key_ref[c]
        is_thr = blk == thr
        eq = jnp.where(is_thr, 1.0, 0.0)
        rank = _dot(tri_ref[...], eq.astype(BF16)) + before
        tie_ok = jnp.where(rank < need_ties, 0.0, NEG)
        bias_ref[c] = jnp.where(blk > thr, 0.0, jnp.where(is_thr, tie_ok, NEG))
        return before + jnp.sum(eq, axis=0, keepdims=True)

    thr_all = jnp.where(live, thr, INT_MIN + 1)

    def bias_chunk_no_ties(c, _):
        bias_ref[c] = jnp.where(key_ref[c] >= thr_all, 0.0, NEG)
        return 0

    @pl.when(has_ties)
    def _():
        lax.fori_loop(0, n_chunks, bias_chunk_ties, jnp.zeros((1, tq), F32))

    @pl.when(jnp.logical_not(has_ties))
    def _():
        lax.fori_loop(0, n_chunks, bias_chunk_no_ties, 0)

    def bias_fn(c):
        b = bias_ref[c]
        return jnp.concatenate([b, b], axis=1)

    def slab(j):
        lanes = slice(j * LANES, (j + 1) * LANES)
        q2 = jnp.concatenate(_split_heads(q_ref[0, :, lanes]), axis=0)
        return (q2, lambda c, rows=None: _kv_chunk(k_ref, c, tk, lanes, rows),
                lambda c: vt_ref[0, c, j * VT_ROWS:(j + 1) * VT_ROWS, :], bias_fn, None)

    _attend_to(o_ref, [slab(j) for j in range(N_SLABS)], n_chunks, False, None, tk, True)


def _dsa_attn(q, k, vt, qi, ki, wit):
    b, s, d = q.shape
    tq, tk = min(TQ, s), vt.shape[3]
    topk = min(TOPK_MAX, s // 4)
    per_q = lambda width: pl.BlockSpec((1, tq, width), lambda bi, i: (bi, i, 0))
    tri = jnp.tril(jnp.ones((tk, tk), BF16), -1)
    return pl.pallas_call(
        functools.partial(_dsa_attn_kernel, tq=tq, tk=tk, topk=topk),
        grid=(b, s // tq),
        in_specs=[per_q(d), _per_batch_spec(k), _per_batch_spec(vt),
                  per_q(qi.shape[2]), _per_batch_spec(ki),
                  pl.BlockSpec((1, IDX_HEADS, tq), lambda bi, i: (bi, 0, i)),
                  _const_spec(tri.shape)],
        out_specs=per_q(d),
        out_shape=jax.ShapeDtypeStruct((b, s, d), BF16),
        scratch_shapes=[pltpu.VMEM((IDX_HEADS, tq, LANES), BF16),
                        pltpu.VMEM((s // tk, tk, tq), I32),
                        pltpu.VMEM((s // tk, tk, tq), I16),
                        pltpu.VMEM((s // tk, tk, tq), I16),
                        pltpu.VMEM((s // tk, tk, tq), F32)],
        compiler_params=_params(2),
        name="dsa_attn",
    )(q, k, vt, qi, ki, wit, tri)


def _dsa_mixer(x2, b, s, w_in, tabs):
    q, k, vt, qi, ki, wi = _dsa_proj(x2, _prep_dsa_w(w_in), tabs, s)
    r3 = lambda a: a.reshape(b, s, a.shape[1])
    wit = r3(wi)[:, :, :IDX_HEADS].transpose(0, 2, 1)
    o = _dsa_attn(r3(q), r3(k), vt, r3(qi), r3(ki), wit)
    return o.reshape(b * s, MIX_WIDTH)


def _fox_mixer(x2, b, s, w_in, b_f):
    d = MIX_WIDTH
    wf = jnp.pad(w_in[:, 3 * d:], ((0, 0), (0, LANES - N_HEADS))).astype(BF16)
    bf = jnp.pad(b_f, (0, LANES - N_HEADS)).reshape(1, LANES)
    q, k, vt, cum = _fox_proj(x2, w_in[:, :3 * d].astype(BF16), wf, bf, s)
    r3 = lambda a: a.reshape(b, s, a.shape[1])
    cum_q = r3(cum)[:, :, :N_HEADS].transpose(0, 2, 1)
    o = _fox_attn(r3(q), r3(k), vt, cum_q, r3(cum))
    return o.reshape(b * s, d)


def _mla_mixer(x2, b, s, w_dqkv, gq, w_uq, gkv, w_ukv, tabs):
    wd, uq, uk, uv = _prep_mla_w(w_dqkv, w_uq, w_ukv)
    q, k, vt = _mla_proj(x2, wd, gq.reshape(1, -1), gkv.reshape(1, -1), uq, uk, uv, tabs, s)
    r3 = lambda a: a.reshape(b, s, a.shape[1])
    o = _mla_attn(r3(q), r3(k), vt)
    return o.reshape(b * s, MIX_WIDTH)


def kernel(x, ffn1_w13, ffn1_w2, ffn2_w13, ffn2_w2, ln_g, ln_b, w_out, dsa_w_in, fox_w_in, fox_b_f,
           mla_w_dqkv, mla_q_norm_g, mla_w_uq, mla_kv_norm_g, mla_w_ukv):
    b, s, dm = x.shape
    x2 = x.reshape(b * s, dm)
    tabs_p = _rope_tables(s, ROT_DIM, 0, HEAD_DIM)
    tabs_m = _rope_tables(s, MLA_ROPE, MLA_NOPE, LANES)
    for i in range(DEPTH):
        g = lambda r, i=i: ln_g[i, r].reshape(1, dm)
        be = lambda r, i=i: ln_b[i, r].reshape(1, dm)
        x2 = _ffn_ln(x2, ffn1_w13[i], ffn1_w2[i], g(0), be(0))
        kind, j = i % N_MIXERS, i // N_MIXERS
        if kind == 0:
            o = _dsa_mixer(x2, b, s, dsa_w_in[j], tabs_p)
        elif kind == 1:
            o = _fox_mixer(x2, b, s, fox_w_in[j], fox_b_f[j])
        else:
            o = _mla_mixer(x2, b, s, mla_w_dqkv[j], mla_q_norm_g[j], mla_w_uq[j],
                           mla_kv_norm_g[j], mla_w_ukv[j], tabs_m)
        x2 = _mix_ffn_ln(x2, o, w_out[i], g(1), be(1), ffn2_w13[i], ffn2_w2[i], g(2), be(2))
    return x2.reshape(b, s, dm)
```

```python
import functools

import jax
import jax.numpy as jnp
from jax import lax
from jax.experimental import pallas as pl
from jax.experimental.pallas import tpu as pltpu

F32 = jnp.float32
BF16 = jnp.bfloat16
I32 = jnp.int32
I16 = jnp.int16

D_MODEL = 1024
DEPTH = 4
N_MIXERS = 3
HEAD_DIM = 64
N_HEADS = D_MODEL // HEAD_DIM
MIX_WIDTH = N_HEADS * HEAD_DIM
ROT_DIM = HEAD_DIM // 4
ROPE_THETA = 500000.0
IDX_HEADS = 8
IDX_DIM = 64
TOPK_MAX = 256
MLA_NOPE = 64
MLA_ROPE = 32
MLA_V = 64
Q_LORA = 384
KV_LORA = 256
D_FF = 2816
ALPHA = (2.0 * DEPTH) ** 0.25
LN_EPS = 1e-5
RMS_EPS = 1e-6

LANES = 128
N_SLABS = MIX_WIDTH // LANES
VT_ROWS = LANES + 16
LOG2E = 1.4426950408889634
NEG = -1e30
INT_MIN = -2147483648

TM = 512
TM_FFN = 2 * TM
TF = 256
TQ = 256
SHIFT_KEYS = 64
VMEM_LIMIT = 56 * 1024 * 1024


def _params(n_axes, vmem=VMEM_LIMIT):
    return pltpu.CompilerParams(dimension_semantics=("arbitrary",) * n_axes, vmem_limit_bytes=vmem)


def _const_spec(shape):
    nd = len(shape)
    return pl.BlockSpec(shape, lambda *_: (0,) * nd, pipeline_mode=pl.Buffered(1))


def _dot(a, b):
    return jnp.dot(a, b, preferred_element_type=F32)


def _dot_nt(a, b):
    return lax.dot_general(a, b, (((1,), (1,)), ((), ())), preferred_element_type=F32)


def _layer_norm(z, g, b):
    mu = jnp.mean(z, axis=-1, keepdims=True)
    d = z - mu
    var = jnp.mean(d * d, axis=-1, keepdims=True)
    return d * lax.rsqrt(var + LN_EPS) * g + b


def _rms_norm(z, g):
    ms = jnp.mean(z * z, axis=-1, keepdims=True)
    return z * lax.rsqrt(ms + RMS_EPS) * g


def _rope_slab(y, cos, sin_lo, sin_hi, half):
    return y * cos + pltpu.roll(y, half, 1) * sin_hi + pltpu.roll(y, LANES - half, 1) * sin_lo


def _store_values_t(vt_ref, v):
    tm = v.shape[0]
    extra = jnp.where(lax.broadcasted_iota(I32, (VT_ROWS - LANES, tm), 0) == 0, 1.0, 0.0).astype(BF16)
    for j in range(N_SLABS):
        vt_ref[0, 0, j * VT_ROWS:j * VT_ROWS + LANES, :] = v[:, j * LANES:(j + 1) * LANES].T.astype(BF16)
        vt_ref[0, 0, j * VT_ROWS + LANES:(j + 1) * VT_ROWS, :] = extra


def _values_t_spec(tiles_per_seq):
    return pl.BlockSpec((1, 1, N_SLABS * VT_ROWS, TM), lambda i: (i // tiles_per_seq, i % tiles_per_seq, 0, 0))


def _values_t_shape(n, seq):
    return jax.ShapeDtypeStruct((n // seq, seq // TM, N_SLABS * VT_ROWS, TM), BF16)


def _ffn_sublayer(x, w13_ref, w2_ref, g, b):
    xb = x.astype(BF16)
    ff = w2_ref.shape[0]
    acc = jnp.zeros(x.shape, F32)
    for c0 in range(0, ff, TF):
        gate = _dot(xb, w13_ref[:, c0:c0 + TF])
        up = _dot(xb, w13_ref[:, ff + c0:ff + c0 + TF])
        h = gate * jax.nn.sigmoid(gate) * up
        acc = acc + _dot(h.astype(BF16), w2_ref[c0:c0 + TF, :])
    return _layer_norm(ALPHA * x + 0.5 * acc, g, b)


def _row_tiles(ref):
    return [slice(r, r + TM) for r in range(0, ref.shape[0], TM)]


def _ffn_ln_kernel(x_ref, w13_ref, w2_ref, g_ref, b_ref, o_ref):
    for rows in _row_tiles(x_ref):
        o_ref[rows, :] = _ffn_sublayer(x_ref[rows, :], w13_ref, w2_ref, g_ref[...], b_ref[...])


def _ffn_ln(x2, w13, w2, g, b):
    n, d = x2.shape
    return pl.pallas_call(
        _ffn_ln_kernel,
        grid=(n // TM_FFN,),
        in_specs=[pl.BlockSpec((TM_FFN, d), lambda i: (i, 0)),
                  _const_spec(w13.shape), _const_spec(w2.shape),
                  _const_spec(g.shape), _const_spec(b.shape)],
        out_specs=pl.BlockSpec((TM_FFN, d), lambda i: (i, 0)),
        out_shape=jax.ShapeDtypeStruct((n, d), F32),
        compiler_params=_params(1),
        name="ffn_ln",
    )(x2, w13.astype(BF16), w2.astype(BF16), g, b)


def _mix_ffn_ln_kernel(x_ref, o_ref, wo_ref, g1_ref, b1_ref, w13_ref, w2_ref, g2_ref, b2_ref, out_ref):
    for rows in _row_tiles(x_ref):
        y = _dot(o_ref[rows, :], wo_ref[...])
        x1 = _layer_norm(ALPHA * x_ref[rows, :] + y, g1_ref[...], b1_ref[...])
        out_ref[rows, :] = _ffn_sublayer(x1, w13_ref, w2_ref, g2_ref[...], b2_ref[...])


def _mix_ffn_ln(x2, o2, wo, g1, b1, w13, w2, g2, b2):
    n, d = x2.shape
    consts = (wo.astype(BF16), g1, b1, w13.astype(BF16), w2.astype(BF16), g2, b2)
    return pl.pallas_call(
        _mix_ffn_ln_kernel,
        grid=(n // TM_FFN,),
        in_specs=[pl.BlockSpec((TM_FFN, d), lambda i: (i, 0)),
                  pl.BlockSpec((TM_FFN, o2.shape[1]), lambda i: (i, 0))] + [_const_spec(a.shape) for a in consts],
        out_specs=pl.BlockSpec((TM_FFN, d), lambda i: (i, 0)),
        out_shape=jax.ShapeDtypeStruct((n, d), F32),
        compiler_params=_params(1),
        name="mix_ffn_ln",
    )(x2, o2, *consts)


def _rope_tables(seq, rot, lane_of_rot0, period):
    half = rot // 2
    inv = ROPE_THETA ** (-jnp.arange(0, rot, 2, dtype=F32) / rot)
    ang = jnp.arange(seq, dtype=F32)[:, None] * inv[None, :]
    cos_h, sin_h = jnp.cos(ang), jnp.sin(ang)
    r = (jnp.arange(LANES) % period) - lane_of_rot0
    in_lo = (r >= 0) & (r < half)
    in_hi = (r >= half) & (r < rot)
    idx = jnp.clip(jnp.where(in_hi, r - half, r), 0, half - 1)
    cos = jnp.where((in_lo | in_hi)[None, :], cos_h[:, idx], 1.0)
    sin_lo = jnp.where(in_lo[None, :], -sin_h[:, idx], 0.0)
    sin_hi = jnp.where(in_hi[None, :], sin_h[:, idx], 0.0)
    return cos, sin_lo, sin_hi


def _dsa_proj_kernel(x_ref, w_ref, cos_ref, slo_ref, shi_ref,
                     q_ref, k_ref, vt_ref, qi_ref, ki_ref, wi_ref):
    xb = x_ref[...].astype(BF16)
    cos, slo, shi = cos_ref[...], slo_ref[...], shi_ref[...]
    half = ROT_DIM // 2
    d = MIX_WIDTH
    di = IDX_HEADS * IDX_DIM

    def roped(col0, width, scale, out_ref):
        y = _dot(xb, w_ref[:, col0:col0 + width])
        for j in range(width // LANES):
            s = _rope_slab(y[:, j * LANES:(j + 1) * LANES], cos, slo, shi, half)
            if scale != 1.0:
                s = s * scale
            out_ref[:, j * LANES:(j + 1) * LANES] = s.astype(out_ref.dtype)

    roped(0, d, HEAD_DIM ** -0.5 * LOG2E, q_ref)
    roped(d, d, 1.0, k_ref)
    _store_values_t(vt_ref, _dot(xb, w_ref[:, 2 * d:3 * d]))
    roped(3 * d, di, IDX_DIM ** -0.5, qi_ref)
    roped(3 * d + di, LANES, 1.0, ki_ref)
    wi_ref[...] = _dot(xb, w_ref[:, 3 * d + di + LANES:]) * (IDX_HEADS ** -0.5)


def _dsa_proj(x2, w, tabs, seq):
    n, dm = x2.shape
    d, di = MIX_WIDTH, IDX_HEADS * IDX_DIM
    tiles_per_seq = seq // TM
    row = lambda width: pl.BlockSpec((TM, width), lambda i: (i, 0))
    tab = pl.BlockSpec((TM, LANES), lambda i: (i % tiles_per_seq, 0))
    return pl.pallas_call(
        _dsa_proj_kernel,
        grid=(n // TM,),
        in_specs=[row(dm), _const_spec(w.shape), tab, tab, tab],
        out_specs=[row(d), row(d), _values_t_spec(tiles_per_seq), row(di), row(LANES), row(LANES)],
        out_shape=[jax.ShapeDtypeStruct((n, d), BF16), jax.ShapeDtypeStruct((n, d), BF16),
                   _values_t_shape(n, seq), jax.ShapeDtypeStruct((n, di), BF16),
                   jax.ShapeDtypeStruct((n, LANES), BF16), jax.ShapeDtypeStruct((n, LANES), F32)],
        compiler_params=_params(1),
        name="dsa_proj",
    )(x2, w, *tabs)


def _prep_dsa_w(w_in):
    d, di = MIX_WIDTH, IDX_HEADS * IDX_DIM
    ki = w_in[:, 3 * d + di:3 * d + di + IDX_DIM]
    wi = w_in[:, 3 * d + di + IDX_DIM:]
    pad = jnp.zeros((w_in.shape[0], LANES - IDX_HEADS), w_in.dtype)
    return jnp.concatenate([w_in[:, :3 * d + di], ki, ki, wi, pad], axis=1).astype(BF16)


def _fox_proj_kernel(x_ref, w_ref, wf_ref, bf_ref, q_ref, k_ref, vt_ref, cum_ref, carry_ref,
                     *, tiles_per_seq):
    i = pl.program_id(0)
    xb = x_ref[...].astype(BF16)
    d = MIX_WIDTH
    q_ref[...] = (_dot(xb, w_ref[:, :d]) * (HEAD_DIM ** -0.5 * LOG2E)).astype(BF16)
    k_ref[...] = _dot(xb, w_ref[:, d:2 * d]).astype(BF16)
    _store_values_t(vt_ref, _dot(xb, w_ref[:, 2 * d:]))

    f = _dot(xb, wf_ref[...]) + bf_ref[...]
    log_f = (jnp.minimum(f, 0.0) - jnp.log1p(jnp.exp(-jnp.abs(f)))) * LOG2E
    tm = log_f.shape[0]
    tri = (lax.broadcasted_iota(I32, (tm, tm), 0) >= lax.broadcasted_iota(I32, (tm, tm), 1))
    tri = jnp.where(tri, 1.0, 0.0).astype(BF16)
    p0 = log_f.astype(BF16)
    r1 = log_f - p0.astype(F32)
    p1 = r1.astype(BF16)
    p2 = (r1 - p1.astype(F32)).astype(BF16)
    local = _dot(tri, p0) + _dot(tri, p1) + _dot(tri, p2)

    @pl.when(i % tiles_per_seq == 0)
    def _():
        carry_ref[...] = jnp.zeros_like(carry_ref)

    cum = local + carry_ref[...]
    cum_ref[...] = cum
    carry_ref[...] = cum[tm - 1:tm, :]


def _fox_proj(x2, w, wf, bf, seq):
    n, dm = x2.shape
    d = MIX_WIDTH
    row = lambda width: pl.BlockSpec((TM, width), lambda i: (i, 0))
    return pl.pallas_call(
        functools.partial(_fox_proj_kernel, tiles_per_seq=seq // TM),
        grid=(n // TM,),
        in_specs=[row(dm), _const_spec(w.shape), _const_spec(wf.shape), _const_spec(bf.shape)],
        out_specs=[row(d), row(d), _values_t_spec(seq // TM), row(LANES)],
        out_shape=[jax.ShapeDtypeStruct((n, d), BF16), jax.ShapeDtypeStruct((n, d), BF16),
                   _values_t_shape(n, seq), jax.ShapeDtypeStruct((n, LANES), F32)],
        scratch_shapes=[pltpu.VMEM((1, LANES), F32)],
        compiler_params=_params(1),
        name="fox_proj",
    )(x2, w, wf, bf)


def _mla_proj_kernel(x_ref, wd_ref, gq_ref, gkv_ref, wuq_ref, wuk_ref, wuv_ref,
                     cos_ref, slo_ref, shi_ref, q_ref, k_ref, vt_ref):
    xb = x_ref[...].astype(BF16)
    cos, slo, shi = cos_ref[...], slo_ref[...], shi_ref[...]
    half = MLA_ROPE // 2
    c = _dot(xb, wd_ref[...])
    cq = _rms_norm(c[:, :Q_LORA], gq_ref[...]).astype(BF16)
    ckv = _rms_norm(c[:, Q_LORA:Q_LORA + KV_LORA], gkv_ref[...]).astype(BF16)
    k_rope = _rope_slab(c[:, Q_LORA + KV_LORA:], cos, slo, shi, half)
    scale = (MLA_NOPE + MLA_ROPE) ** -0.5 * LOG2E
    for h in range(N_HEADS):
        sl = slice(h * LANES, (h + 1) * LANES)
        q = _rope_slab(_dot(cq, wuq_ref[:, sl]), cos, slo, shi, half)
        q_ref[:, sl] = (q * scale).astype(BF16)
        k_ref[:, sl] = (_dot(ckv, wuk_ref[:, sl]) + k_rope).astype(BF16)
    _store_values_t(vt_ref, _dot(ckv, wuv_ref[...]))


def _mla_proj(x2, wd, gq, gkv, wuq, wuk, wuv, tabs, seq):
    n, dm = x2.shape
    tiles_per_seq = seq // TM
    row = lambda width: pl.BlockSpec((TM, width), lambda i: (i, 0))
    tab = pl.BlockSpec((TM, LANES), lambda i: (i % tiles_per_seq, 0))
    wide = N_HEADS * LANES
    return pl.pallas_call(
        _mla_proj_kernel,
        grid=(n // TM,),
        in_specs=[row(dm)] + [_const_spec(a.shape) for a in (wd, gq, gkv, wuq, wuk, wuv)] + [tab] * 3,
        out_specs=[row(wide), row(wide), _values_t_spec(tiles_per_seq)],
        out_shape=[jax.ShapeDtypeStruct((n, wide), BF16), jax.ShapeDtypeStruct((n, wide), BF16),
                   _values_t_shape(n, seq)],
        compiler_params=_params(1),
        name="mla_proj",
    )(x2, wd, gq, gkv, wuq, wuk, wuv, *tabs)


def _prep_mla_w(w_dqkv, w_uq, w_ukv):
    dm = w_dqkv.shape[0]
    zeros = lambda r, c: jnp.zeros((r, c), w_dqkv.dtype)
    k_rope_w = w_dqkv[:, Q_LORA + KV_LORA:]
    wd = jnp.concatenate([w_dqkv[:, :Q_LORA + KV_LORA], zeros(dm, MLA_NOPE), k_rope_w,
                          zeros(dm, LANES - MLA_NOPE - MLA_ROPE)], axis=1)
    uq = w_uq.reshape(Q_LORA, N_HEADS, MLA_NOPE + MLA_ROPE)
    uq = jnp.pad(uq, ((0, 0), (0, 0), (0, LANES - MLA_NOPE - MLA_ROPE))).reshape(Q_LORA, N_HEADS * LANES)
    ukv = w_ukv.reshape(KV_LORA, N_HEADS, MLA_NOPE + MLA_V)
    uk = jnp.pad(ukv[:, :, :MLA_NOPE], ((0, 0), (0, 0), (0, LANES - MLA_NOPE))).reshape(KV_LORA, N_HEADS * LANES)
    uv = ukv[:, :, MLA_NOPE:].reshape(KV_LORA, N_HEADS * MLA_V)
    return wd.astype(BF16), uq.astype(BF16), uk.astype(BF16), uv.astype(BF16)


def _attend_slabs(slabs, n_plain, causal, qpos, tk, one_pass, half_last):
    m2 = slabs[0][0].shape[0]
    tq = m2 // 2

    def step(c, carries, masked, stale, keys=tk):
        scores = [_dot_nt(slab[1](c, keys), slab[0]) for slab in slabs]
        mask = (c * tk + lax.broadcasted_iota(I32, (keys, m2), 0) <= qpos) if masked else None
        out = []
        for (_, _, vt_load, bias_fn, qbias), (m, acc), s in zip(slabs, carries, scores):
            if bias_fn is not None:
                s = bias_fn(c, keys) + s
            if mask is not None:
                s = jnp.where(mask, s, NEG)
            smax = jnp.max(s, axis=0, keepdims=True)
            m_new = jnp.maximum(m, smax if qbias is None else smax + qbias)
            alpha = jnp.exp2(m - m_new)
            shift = m if stale else m_new
            p = jnp.exp2(s - (shift if qbias is None else shift - qbias)).astype(BF16)
            if stale:
                acc = (acc + _dot(vt_load(c, keys), p)) * alpha
            else:
                acc = alpha * acc + _dot(vt_load(c, keys), p)
            out.append((m_new, acc))
        return tuple(out)

    def first_shift(c, masked):
        sub_iota = lax.broadcasted_iota(I32, (SHIFT_KEYS, m2), 0)
        mask = (c * tk + sub_iota <= qpos) if masked else None
        shifts = []
        for q2, k_load, _, _, qbias in slabs:
            s = _dot_nt(k_load(c, SHIFT_KEYS), q2)
            if mask is not None:
                s = jnp.where(mask, s, NEG)
            smax = jnp.max(s, axis=0, keepdims=True)
            shifts.append(smax if qbias is None else smax + qbias)
        return shifts

    zero_acc = jnp.zeros((VT_ROWS, m2), F32)
    carries = tuple((jnp.full((1, m2), NEG, F32), zero_acc) for _ in slabs)
    if one_pass and causal:
        stale_first = slabs[0][3] is None
        if stale_first:
            carries = tuple((m0, zero_acc) for m0 in first_shift(n_plain, True))
        carries = lax.fori_loop(0, half_last, lambda _, cr: step(n_plain, cr, True, stale_first, tk // 2), carries)
        carries = lax.fori_loop(half_last, 1, lambda _, cr: step(n_plain, cr, True, stale_first), carries)
        carries = lax.fori_loop(0, n_plain, lambda t, cr: step(n_plain - 1 - t, cr, False, True), carries)
    elif one_pass:
        carries = tuple((m0, zero_acc) for m0 in first_shift(0, False))
        carries = lax.fori_loop(0, n_plain - half_last, lambda c, cr: step(c, cr, False, True), carries)
        carries = lax.fori_loop(n_plain - half_last, n_plain,
                                lambda c, cr: step(c, cr, False, True, tk // 2), carries)
    else:
        carries = lax.fori_loop(0, n_plain, lambda c, cr: step(c, cr, False, False), carries)
        if causal:
            carries = step(n_plain, carries, True, False)
    row = lax.broadcasted_iota(I32, (LANES, tq), 0)
    outs = []
    for _, acc in carries:
        o = acc[:LANES] / acc[LANES:LANES + 1]
        outs.append(jnp.where(row < HEAD_DIM, o[:, :tq], o[:, tq:]))
    return outs


def _attend_to(o_ref, slabs, n_plain, causal, qpos, tk, one_pass, half_last):
    def run(one_pass):
        outs = _attend_slabs(slabs, n_plain, causal, qpos, tk, one_pass, half_last)
        for j, out in enumerate(outs):
            o_ref[0, :, j * LANES:(j + 1) * LANES] = out.T.astype(o_ref.dtype)
        return outs

    if not one_pass:
        run(False)
        return
    outs = run(True)
    total = functools.reduce(lambda a, b: a + b, [jnp.sum(o, axis=0, keepdims=True) for o in outs])
    overflowed = jnp.max(jnp.where(jnp.isfinite(total), 0.0, 1.0)) > 0.0

    @pl.when(overflowed)
    def _():
        run(False)


def _split_heads(q_slab):
    lane = lax.broadcasted_iota(I32, q_slab.shape, 1)
    zero = jnp.zeros_like(q_slab)
    return jnp.where(lane < HEAD_DIM, q_slab, zero), jnp.where(lane >= HEAD_DIM, q_slab, zero)


def _kv_chunk(ref, c, tk, lanes, rows=None):
    return ref[0, pl.ds(pl.multiple_of(c * tk, tk), tk if rows is None else rows), lanes]


def _per_batch_spec(a):
    nd = a.ndim
    return pl.BlockSpec((1,) + a.shape[1:], lambda bi, i: (bi,) + (0,) * (nd - 1),
                        pipeline_mode=pl.Buffered(1))


def _half_last(row0, tq, tk):
    if 2 * tq != tk:
        return jnp.int32(0)
    return jnp.where(row0 % tk == 0, 1, 0).astype(I32)


def _query_positions(row0, tq):
    pos = row0 + lax.broadcasted_iota(I32, (1, tq), 1)
    return jnp.concatenate([pos, pos], axis=1)


def _fox_attn_kernel(q_ref, k_ref, vt_ref, cq_ref, ck_ref, o_ref, *, tq, tk):
    i = pl.program_id(1)
    row0 = i * tq
    diag = row0 // tk
    qpos = _query_positions(row0, tq)
    cq = cq_ref[0]

    def slab(j):
        lanes = slice(j * LANES, (j + 1) * LANES)
        q2 = jnp.concatenate(_split_heads(q_ref[0, :, lanes]), axis=0)
        cq2 = jnp.concatenate([cq[2 * j:2 * j + 1, :], cq[2 * j + 1:2 * j + 2, :]], axis=1)

        def bias_fn(c, keys):
            ck = -_kv_chunk(ck_ref, c, tk, slice(None), keys)
            return jnp.concatenate([jnp.broadcast_to(ck[:, 2 * j:2 * j + 1], (keys, tq)),
                                    jnp.broadcast_to(ck[:, 2 * j + 1:2 * j + 2], (keys, tq))], axis=1)

        return (q2, lambda c, keys: _kv_chunk(k_ref, c, tk, lanes, keys),
                lambda c, keys: vt_ref[0, c, j * VT_ROWS:(j + 1) * VT_ROWS, :keys], bias_fn, cq2)

    _attend_to(o_ref, [slab(j) for j in range(N_SLABS)], diag, True, qpos, tk, True, _half_last(row0, tq, tk))


def _fox_attn(q, k, vt, cum_q, cum_k):
    b, s, d = q.shape
    tq, tk = min(TQ, s), vt.shape[3]
    return pl.pallas_call(
        functools.partial(_fox_attn_kernel, tq=tq, tk=tk),
        grid=(b, s // tq),
        in_specs=[pl.BlockSpec((1, tq, d), lambda bi, i: (bi, i, 0)),
                  _per_batch_spec(k), _per_batch_spec(vt),
                  pl.BlockSpec((1, N_HEADS, tq), lambda bi, i: (bi, 0, i)),
                  _per_batch_spec(cum_k)],
        out_specs=pl.BlockSpec((1, tq, d), lambda bi, i: (bi, i, 0)),
        out_shape=jax.ShapeDtypeStruct((b, s, d), BF16),
        compiler_params=_params(2),
        name="fox_attn",
    )(q, k, vt, cum_q, cum_k)


def _mla_attn_kernel(q_ref, k_ref, vt_ref, o_ref, *, tq, tk):
    i = pl.program_id(1)
    row0 = i * tq
    diag = row0 // tk
    qpos = _query_positions(row0, tq)
    zero = jnp.zeros((tq, LANES), BF16)

    def slab(j):
        pair = slice(2 * j * LANES, (2 * j + 2) * LANES)
        qa = q_ref[0, :, 2 * j * LANES:(2 * j + 1) * LANES]
        qb = q_ref[0, :, (2 * j + 1) * LANES:(2 * j + 2) * LANES]
        q2 = jnp.concatenate([jnp.concatenate([qa, zero], axis=1),
                              jnp.concatenate([zero, qb], axis=1)], axis=0)
        lanes = slice(j * LANES, (j + 1) * LANES)
        return (q2, lambda c, keys: _kv_chunk(k_ref, c, tk, pair, keys),
                lambda c, keys: vt_ref[0, c, j * VT_ROWS:(j + 1) * VT_ROWS, :keys], None, None)

    _attend_to(o_ref, [slab(j) for j in range(N_SLABS)], diag, True, qpos, tk, True, _half_last(row0, tq, tk))


def _mla_attn(q, k, vt):
    b, s, wide = q.shape
    d = MIX_WIDTH
    tq, tk = min(TQ, s), vt.shape[3]
    return pl.pallas_call(
        functools.partial(_mla_attn_kernel, tq=tq, tk=tk),
        grid=(b, s // tq),
        in_specs=[pl.BlockSpec((1, tq, wide), lambda bi, i: (bi, i, 0)),
                  _per_batch_spec(k), _per_batch_spec(vt)],
        out_specs=pl.BlockSpec((1, tq, d), lambda bi, i: (bi, i, 0)),
        out_shape=jax.ShapeDtypeStruct((b, s, d), BF16),
        compiler_params=_params(2),
        name="mla_attn",
    )(q, k, vt)


def _sortable_key(score):
    bits = pltpu.bitcast(score + 0.0, I32)
    return jnp.where(bits < 0, bits ^ 0x7FFFFFFF, bits)


def _count_keys(ref, n_chunks, indicator, acc_rows):
    _, tk, tq = ref.shape

    def body(c, acc):
        ind = indicator(ref[c], c)
        for r in range(tk // acc_rows):
            acc = acc + ind[r * acc_rows:(r + 1) * acc_rows]
        return acc

    acc = lax.fori_loop(0, n_chunks, body, jnp.zeros((acc_rows, tq), ref.dtype))
    return jnp.sum(acc.astype(F32), axis=0, keepdims=True)


def _radix_select16(ref, n_chunks, target):
    tq = ref.shape[2]
    one, zero = jnp.int16(1), jnp.int16(0)

    def signed(u):
        return lax.shift_right_arithmetic(lax.shift_left(u ^ 0x8000, 16), 16)

    def bit_step(t, prefix):
        cand = prefix | lax.shift_left(jnp.int32(1), 15 - t)
        cand16 = signed(cand).astype(I16)
        cnt = _count_keys(ref, n_chunks, lambda blk, c: jnp.where(blk >= cand16, one, zero), 64)
        return jnp.where(cnt >= target, cand, prefix)

    return signed(lax.fori_loop(0, 16, bit_step, jnp.zeros((1, tq), I32)))


def _dsa_attn_kernel(q_ref, k_ref, vt_ref, qi_ref, ki_ref, wit_ref, tri_ref, o_ref,
                     qim_ref, key_ref, hi_ref, lo_ref, bias_ref, *, tq, tk, topk):
    i = pl.program_id(1)
    row0 = i * tq
    n_chunks = row0 // tk + 1
    qpos = row0 + lax.broadcasted_iota(I32, (1, tq), 1)
    kiota = lax.broadcasted_iota(I32, (tk, tq), 0)

    for h in range(IDX_HEADS):
        pair = _split_heads(qi_ref[0, :, (h // 2) * LANES:(h // 2 + 1) * LANES])
        qim_ref[h] = pair[h % 2]
    wit = wit_ref[0]

    def score_chunk(c, _):
        ki = _kv_chunk(ki_ref, c, tk, slice(None))
        acc = jnp.zeros((tk, tq), F32)
        for h in range(IDX_HEADS):
            acc = acc + wit[h:h + 1, :] * jnp.maximum(_dot_nt(ki, qim_ref[h]), 0.0)
        key = jnp.where(c * tk + kiota <= qpos, _sortable_key(acc), INT_MIN)
        key_ref[c] = key
        hi_ref[c] = lax.shift_right_arithmetic(key, 16).astype(I16)
        return 0

    lax.fori_loop(0, n_chunks, score_chunk, 0)

    k_f = float(topk)
    one16, zero16 = jnp.int16(1), jnp.int16(0)
    thr_hi = _radix_select16(hi_ref, n_chunks, k_f)
    thr_hi16 = thr_hi.astype(I16)
    n_gt_hi = _count_keys(hi_ref, n_chunks, lambda blk, c: jnp.where(blk > thr_hi16, one16, zero16), 64)

    def low_plane_chunk(c, _):
        key = key_ref[c]
        low = (key & 0xFFFF) - 0x8000
        in_bucket = lax.shift_right_arithmetic(key, 16) == thr_hi
        lo_ref[c] = jnp.where(in_bucket, low, -0x8000).astype(I16)
        return 0

    lax.fori_loop(0, n_chunks, low_plane_chunk, 0)
    thr_lo = _radix_select16(lo_ref, n_chunks, k_f - n_gt_hi)
    thr_lo16 = thr_lo.astype(I16)
    live = thr_hi != -0x8000
    thr = jnp.where(live, thr_hi * 0x10000 + (thr_lo + 0x8000), INT_MIN)
    n_gt = n_gt_hi + _count_keys(lo_ref, n_chunks,
                                 lambda blk, c: jnp.where(blk > thr_lo16, one16, zero16), 64)

    def count(indicator):
        return _count_keys(key_ref, n_chunks, indicator, 32)

    n_eq = count(lambda blk, c: jnp.where(blk == thr, 1, 0))
    need = k_f - n_gt

    has_ties = jnp.max(jnp.where(live, n_eq - need, 0.0)) > 0.0

    need_ties = jnp.where(live, need, 0.0)

    def bias_chunk_ties(c, before):
        blk = key_ref[c]
        is_thr = blk == thr
        eq = jnp.where(is_thr, 1.0, 0.0)
        rank = _dot(tri_ref[...], eq.astype(BF16)) + before
        tie_ok = jnp.where(rank < need_ties, 0.0, NEG)
        bias_ref[c] = jnp.where(blk > thr, 0.0, jnp.where(is_thr, tie_ok, NEG))
        return before + jnp.sum(eq, axis=0, keepdims=True)

    thr_all = jnp.where(live, thr, INT_MIN + 1)

    def bias_chunk_no_ties(c, _):
        bias_ref[c] = jnp.where(key_ref[c] >= thr_all, 0.0, NEG)
        return 0

    @pl.when(has_ties)
    def _():
        lax.fori_loop(0, n_chunks, bias_chunk_ties, jnp.zeros((1, tq), F32))

    @pl.when(jnp.logical_not(has_ties))
    def _():
        lax.fori_loop(0, n_chunks, bias_chunk_no_ties, 0)

    def bias_fn(c, keys):
        b = bias_ref[c, :keys, :]
        return jnp.concatenate([b, b], axis=1)

    def slab(j):
        lanes = slice(j * LANES, (j + 1) * LANES)
        q2 = jnp.concatenate(_split_heads(q_ref[0, :, lanes]), axis=0)
        return (q2, lambda c, keys: _kv_chunk(k_ref, c, tk, lanes, keys),
                lambda c, keys: vt_ref[0, c, j * VT_ROWS:(j + 1) * VT_ROWS, :keys], bias_fn, None)

    _attend_to(o_ref, [slab(j) for j in range(N_SLABS)], n_chunks, False, None, tk, True, _half_last(row0, tq, tk))


def _dsa_attn(q, k, vt, qi, ki, wit):
    b, s, d = q.shape
    tq, tk = min(TQ, s), vt.shape[3]
    topk = min(TOPK_MAX, s // 4)
    per_q = lambda width: pl.BlockSpec((1, tq, width), lambda bi, i: (bi, i, 0))
    tri = jnp.tril(jnp.ones((tk, tk), BF16), -1)
    return pl.pallas_call(
        functools.partial(_dsa_attn_kernel, tq=tq, tk=tk, topk=topk),
        grid=(b, s // tq),
        in_specs=[per_q(d), _per_batch_spec(k), _per_batch_spec(vt),
                  per_q(qi.shape[2]), _per_batch_spec(ki),
                  pl.BlockSpec((1, IDX_HEADS, tq), lambda bi, i: (bi, 0, i)),
                  _const_spec(tri.shape)],
        out_specs=per_q(d),
        out_shape=jax.ShapeDtypeStruct((b, s, d), BF16),
        scratch_shapes=[pltpu.VMEM((IDX_HEADS, tq, LANES), BF16),
                        pltpu.VMEM((s // tk, tk, tq), I32),
                        pltpu.VMEM((s // tk, tk, tq), I16),
                        pltpu.VMEM((s // tk, tk, tq), I16),
                        pltpu.VMEM((s // tk, tk, tq), F32)],
        compiler_params=_params(2),
        name="dsa_attn",
    )(q, k, vt, qi, ki, wit, tri)


def _dsa_mixer(x2, b, s, w_in, tabs):
    q, k, vt, qi, ki, wi = _dsa_proj(x2, _prep_dsa_w(w_in), tabs, s)
    r3 = lambda a: a.reshape(b, s, a.shape[1])
    wit = r3(wi)[:, :, :IDX_HEADS].transpose(0, 2, 1)
    o = _dsa_attn(r3(q), r3(k), vt, r3(qi), r3(ki), wit)
    return o.reshape(b * s, MIX_WIDTH)


def _fox_mixer(x2, b, s, w_in, b_f):
    d = MIX_WIDTH
    wf = jnp.pad(w_in[:, 3 * d:], ((0, 0), (0, LANES - N_HEADS))).astype(BF16)
    bf = jnp.pad(b_f, (0, LANES - N_HEADS)).reshape(1, LANES)
    q, k, vt, cum = _fox_proj(x2, w_in[:, :3 * d].astype(BF16), wf, bf, s)
    r3 = lambda a: a.reshape(b, s, a.shape[1])
    cum_q = r3(cum)[:, :, :N_HEADS].transpose(0, 2, 1)
    o = _fox_attn(r3(q), r3(k), vt, cum_q, r3(cum))
    return o.reshape(b * s, d)


def _mla_mixer(x2, b, s, w_dqkv, gq, w_uq, gkv, w_ukv, tabs):
    wd, uq, uk, uv = _prep_mla_w(w_dqkv, w_uq, w_ukv)
    q, k, vt = _mla_proj(x2, wd, gq.reshape(1, -1), gkv.reshape(1, -1), uq, uk, uv, tabs, s)
    r3 = lambda a: a.reshape(b, s, a.shape[1])
    o = _mla_attn(r3(q), r3(k), vt)
    return o.reshape(b * s, MIX_WIDTH)


def kernel(x, ffn1_w13, ffn1_w2, ffn2_w13, ffn2_w2, ln_g, ln_b, w_out, dsa_w_in, fox_w_in, fox_b_f,
           mla_w_dqkv, mla_q_norm_g, mla_w_uq, mla_kv_norm_g, mla_w_ukv):
    b, s, dm = x.shape
    x2 = x.reshape(b * s, dm)
    tabs_p = _rope_tables(s, ROT_DIM, 0, HEAD_DIM)
    tabs_m = _rope_tables(s, MLA_ROPE, MLA_NOPE, LANES)
    for i in range(DEPTH):
        g = lambda r, i=i: ln_g[i, r].reshape(1, dm)
        be = lambda r, i=i: ln_b[i, r].reshape(1, dm)
        x2 = _ffn_ln(x2, ffn1_w13[i], ffn1_w2[i], g(0), be(0))
        kind, j = i % N_MIXERS, i // N_MIXERS
        if kind == 0:
            o = _dsa_mixer(x2, b, s, dsa_w_in[j], tabs_p)
        elif kind == 1:
            o = _fox_mixer(x2, b, s, fox_w_in[j], fox_b_f[j])
        else:
            o = _mla_mixer(x2, b, s, mla_w_dqkv[j], mla_q_norm_g[j], mla_w_uq[j],
                           mla_kv_norm_g[j], mla_w_ukv[j], tabs_m)
        x2 = _mix_ffn_ln(x2, o, w_out[i], g(1), be(1), ffn2_w13[i], ffn2_w2[i], g(2), be(2))
    return x2.reshape(b, s, dm)
```

```python
import functools

import jax
import jax.numpy as jnp
from jax import lax
from jax.experimental import pallas as pl
from jax.experimental.pallas import tpu as pltpu

F32 = jnp.float32
BF16 = jnp.bfloat16
I32 = jnp.int32
I16 = jnp.int16

D_MODEL = 1024
DEPTH = 4
N_MIXERS = 3
HEAD_DIM = 64
N_HEADS = D_MODEL // HEAD_DIM
MIX_WIDTH = N_HEADS * HEAD_DIM
ROT_DIM = HEAD_DIM // 4
ROPE_THETA = 500000.0
IDX_HEADS = 8
IDX_DIM = 64
TOPK_MAX = 256
MLA_NOPE = 64
MLA_ROPE = 32
MLA_V = 64
Q_LORA = 384
KV_LORA = 256
D_FF = 2816
ALPHA = (2.0 * DEPTH) ** 0.25
LN_EPS = 1e-5
RMS_EPS = 1e-6

LANES = 128
N_SLABS = MIX_WIDTH // LANES
VT_ROWS = LANES + 16
LOG2E = 1.4426950408889634
NEG = -1e30
INT_MIN = -2147483648

TM = 512
TM_FFN = 2 * TM
TF = 256
TQ = 256
SHIFT_KEYS = 64
VMEM_LIMIT = 56 * 1024 * 1024


def _params(n_axes, vmem=VMEM_LIMIT):
    return pltpu.CompilerParams(dimension_semantics=("arbitrary",) * n_axes, vmem_limit_bytes=vmem)


def _const_spec(shape):
    nd = len(shape)
    return pl.BlockSpec(shape, lambda *_: (0,) * nd, pipeline_mode=pl.Buffered(1))


def _dot(a, b):
    return jnp.dot(a, b, preferred_element_type=F32)


def _dot_nt(a, b):
    return lax.dot_general(a, b, (((1,), (1,)), ((), ())), preferred_element_type=F32)


def _layer_norm(z, g, b):
    mu = jnp.mean(z, axis=-1, keepdims=True)
    d = z - mu
    var = jnp.mean(d * d, axis=-1, keepdims=True)
    return d * lax.rsqrt(var + LN_EPS) * g + b


def _rms_norm(z, g):
    ms = jnp.mean(z * z, axis=-1, keepdims=True)
    return z * lax.rsqrt(ms + RMS_EPS) * g


def _rope_slab(y, cos, sin_lo, sin_hi, half):
    return y * cos + pltpu.roll(y, half, 1) * sin_hi + pltpu.roll(y, LANES - half, 1) * sin_lo


def _store_values_t(vt_ref, v):
    tm = v.shape[0]
    extra = jnp.where(lax.broadcasted_iota(I32, (VT_ROWS - LANES, tm), 0) == 0, 1.0, 0.0).astype(BF16)
    for j in range(N_SLABS):
        vt_ref[0, 0, j * VT_ROWS:j * VT_ROWS + LANES, :] = v[:, j * LANES:(j + 1) * LANES].T.astype(BF16)
        vt_ref[0, 0, j * VT_ROWS + LANES:(j + 1) * VT_ROWS, :] = extra


def _values_t_spec(tiles_per_seq):
    return pl.BlockSpec((1, 1, N_SLABS * VT_ROWS, TM), lambda i: (i // tiles_per_seq, i % tiles_per_seq, 0, 0))


def _values_t_shape(n, seq):
    return jax.ShapeDtypeStruct((n // seq, seq // TM, N_SLABS * VT_ROWS, TM), BF16)


def _ffn_sublayer(x, w13_ref, w2_ref, g, b):
    xb = x.astype(BF16)
    ff = w2_ref.shape[0]
    acc = jnp.zeros(x.shape, F32)
    for c0 in range(0, ff, TF):
        gate = _dot(xb, w13_ref[:, c0:c0 + TF])
        up = _dot(xb, w13_ref[:, ff + c0:ff + c0 + TF])
        h = gate * jax.nn.sigmoid(gate) * up
        acc = acc + _dot(h.astype(BF16), w2_ref[c0:c0 + TF, :])
    return _layer_norm(ALPHA * x + 0.5 * acc, g, b)


def _row_tiles(ref):
    return [slice(r, r + TM) for r in range(0, ref.shape[0], TM)]


def _ffn_ln_kernel(x_ref, w13_ref, w2_ref, g_ref, b_ref, o_ref):
    for rows in _row_tiles(x_ref):
        o_ref[rows, :] = _ffn_sublayer(x_ref[rows, :], w13_ref, w2_ref, g_ref[...], b_ref[...])


def _ffn_ln(x2, w13, w2, g, b):
    n, d = x2.shape
    return pl.pallas_call(
        _ffn_ln_kernel,
        grid=(n // TM_FFN,),
        in_specs=[pl.BlockSpec((TM_FFN, d), lambda i: (i, 0)),
                  _const_spec(w13.shape), _const_spec(w2.shape),
                  _const_spec(g.shape), _const_spec(b.shape)],
        out_specs=pl.BlockSpec((TM_FFN, d), lambda i: (i, 0)),
        out_shape=jax.ShapeDtypeStruct((n, d), F32),
        compiler_params=_params(1),
        name="ffn_ln",
    )(x2, w13.astype(BF16), w2.astype(BF16), g, b)


def _mix_ffn_ln_kernel(x_ref, o_ref, wo_ref, g1_ref, b1_ref, w13_ref, w2_ref, g2_ref, b2_ref, out_ref):
    for rows in _row_tiles(x_ref):
        y = _dot(o_ref[rows, :], wo_ref[...])
        x1 = _layer_norm(ALPHA * x_ref[rows, :] + y, g1_ref[...], b1_ref[...])
        out_ref[rows, :] = _ffn_sublayer(x1, w13_ref, w2_ref, g2_ref[...], b2_ref[...])


def _mix_ffn_ln(x2, o2, wo, g1, b1, w13, w2, g2, b2):
    n, d = x2.shape
    consts = (wo.astype(BF16), g1, b1, w13.astype(BF16), w2.astype(BF16), g2, b2)
    return pl.pallas_call(
        _mix_ffn_ln_kernel,
        grid=(n // TM_FFN,),
        in_specs=[pl.BlockSpec((TM_FFN, d), lambda i: (i, 0)),
                  pl.BlockSpec((TM_FFN, o2.shape[1]), lambda i: (i, 0))] + [_const_spec(a.shape) for a in consts],
        out_specs=pl.BlockSpec((TM_FFN, d), lambda i: (i, 0)),
        out_shape=jax.ShapeDtypeStruct((n, d), F32),
        compiler_params=_params(1),
        name="mix_ffn_ln",
    )(x2, o2, *consts)


def _rope_tables(seq, rot, lane_of_rot0, period):
    half = rot // 2
    inv = ROPE_THETA ** (-jnp.arange(0, rot, 2, dtype=F32) / rot)
    ang = jnp.arange(seq, dtype=F32)[:, None] * inv[None, :]
    cos_h, sin_h = jnp.cos(ang), jnp.sin(ang)
    r = (jnp.arange(LANES) % period) - lane_of_rot0
    in_lo = (r >= 0) & (r < half)
    in_hi = (r >= half) & (r < rot)
    idx = jnp.clip(jnp.where(in_hi, r - half, r), 0, half - 1)
    cos = jnp.where((in_lo | in_hi)[None, :], cos_h[:, idx], 1.0)
    sin_lo = jnp.where(in_lo[None, :], -sin_h[:, idx], 0.0)
    sin_hi = jnp.where(in_hi[None, :], sin_h[:, idx], 0.0)
    return cos, sin_lo, sin_hi


def _dsa_proj_kernel(x_ref, w_ref, cos_ref, slo_ref, shi_ref,
                     q_ref, k_ref, vt_ref, qi_ref, ki_ref, wi_ref):
    xb = x_ref[...].astype(BF16)
    cos, slo, shi = cos_ref[...], slo_ref[...], shi_ref[...]
    half = ROT_DIM // 2
    d = MIX_WIDTH
    di = IDX_HEADS * IDX_DIM

    def roped(col0, width, scale, out_ref):
        y = _dot(xb, w_ref[:, col0:col0 + width])
        for j in range(width // LANES):
            s = _rope_slab(y[:, j * LANES:(j + 1) * LANES], cos, slo, shi, half)
            if scale != 1.0:
                s = s * scale
            out_ref[:, j * LANES:(j + 1) * LANES] = s.astype(out_ref.dtype)

    roped(0, d, HEAD_DIM ** -0.5 * LOG2E, q_ref)
    roped(d, d, 1.0, k_ref)
    _store_values_t(vt_ref, _dot(xb, w_ref[:, 2 * d:3 * d]))
    roped(3 * d, di, IDX_DIM ** -0.5, qi_ref)
    roped(3 * d + di, LANES, 1.0, ki_ref)
    wi_ref[...] = _dot(xb, w_ref[:, 3 * d + di + LANES:]) * (IDX_HEADS ** -0.5)


def _dsa_proj(x2, w, tabs, seq):
    n, dm = x2.shape
    d, di = MIX_WIDTH, IDX_HEADS * IDX_DIM
    tiles_per_seq = seq // TM
    row = lambda width: pl.BlockSpec((TM, width), lambda i: (i, 0))
    tab = pl.BlockSpec((TM, LANES), lambda i: (i % tiles_per_seq, 0))
    return pl.pallas_call(
        _dsa_proj_kernel,
        grid=(n // TM,),
        in_specs=[row(dm), _const_spec(w.shape), tab, tab, tab],
        out_specs=[row(d), row(d), _values_t_spec(tiles_per_seq), row(di), row(LANES), row(LANES)],
        out_shape=[jax.ShapeDtypeStruct((n, d), BF16), jax.ShapeDtypeStruct((n, d), BF16),
                   _values_t_shape(n, seq), jax.ShapeDtypeStruct((n, di), BF16),
                   jax.ShapeDtypeStruct((n, LANES), BF16), jax.ShapeDtypeStruct((n, LANES), F32)],
        compiler_params=_params(1),
        name="dsa_proj",
    )(x2, w, *tabs)


def _prep_dsa_w(w_in):
    d, di = MIX_WIDTH, IDX_HEADS * IDX_DIM
    ki = w_in[:, 3 * d + di:3 * d + di + IDX_DIM]
    wi = w_in[:, 3 * d + di + IDX_DIM:]
    pad = jnp.zeros((w_in.shape[0], LANES - IDX_HEADS), w_in.dtype)
    return jnp.concatenate([w_in[:, :3 * d + di], ki, ki, wi, pad], axis=1).astype(BF16)


def _fox_proj_kernel(x_ref, w_ref, wf_ref, bf_ref, q_ref, k_ref, vt_ref, cum_ref, carry_ref,
                     *, tiles_per_seq):
    i = pl.program_id(0)
    xb = x_ref[...].astype(BF16)
    d = MIX_WIDTH
    q_ref[...] = (_dot(xb, w_ref[:, :d]) * (HEAD_DIM ** -0.5 * LOG2E)).astype(BF16)
    k_ref[...] = _dot(xb, w_ref[:, d:2 * d]).astype(BF16)
    _store_values_t(vt_ref, _dot(xb, w_ref[:, 2 * d:]))

    f = _dot(xb, wf_ref[...]) + bf_ref[...]
    log_f = (jnp.minimum(f, 0.0) - jnp.log1p(jnp.exp(-jnp.abs(f)))) * LOG2E
    tm = log_f.shape[0]
    tri = (lax.broadcasted_iota(I32, (tm, tm), 0) >= lax.broadcasted_iota(I32, (tm, tm), 1))
    tri = jnp.where(tri, 1.0, 0.0).astype(BF16)
    p0 = log_f.astype(BF16)
    r1 = log_f - p0.astype(F32)
    p1 = r1.astype(BF16)
    p2 = (r1 - p1.astype(F32)).astype(BF16)
    local = _dot(tri, p0) + _dot(tri, p1) + _dot(tri, p2)

    @pl.when(i % tiles_per_seq == 0)
    def _():
        carry_ref[...] = jnp.zeros_like(carry_ref)

    cum = local + carry_ref[...]
    cum_ref[...] = cum
    carry_ref[...] = cum[tm - 1:tm, :]


def _fox_proj(x2, w, wf, bf, seq):
    n, dm = x2.shape
    d = MIX_WIDTH
    row = lambda width: pl.BlockSpec((TM, width), lambda i: (i, 0))
    return pl.pallas_call(
        functools.partial(_fox_proj_kernel, tiles_per_seq=seq // TM),
        grid=(n // TM,),
        in_specs=[row(dm), _const_spec(w.shape), _const_spec(wf.shape), _const_spec(bf.shape)],
        out_specs=[row(d), row(d), _values_t_spec(seq // TM), row(LANES)],
        out_shape=[jax.ShapeDtypeStruct((n, d), BF16), jax.ShapeDtypeStruct((n, d), BF16),
                   _values_t_shape(n, seq), jax.ShapeDtypeStruct((n, LANES), F32)],
        scratch_shapes=[pltpu.VMEM((1, LANES), F32)],
        compiler_params=_params(1),
        name="fox_proj",
    )(x2, w, wf, bf)


def _mla_proj_kernel(x_ref, wd_ref, gq_ref, gkv_ref, wuq_ref, wuk_ref, wuv_ref,
                     cos_ref, slo_ref, shi_ref, q_ref, k_ref, vt_ref):
    xb = x_ref[...].astype(BF16)
    cos, slo, shi = cos_ref[...], slo_ref[...], shi_ref[...]
    half = MLA_ROPE // 2
    c = _dot(xb, wd_ref[...])
    cq = _rms_norm(c[:, :Q_LORA], gq_ref[...]).astype(BF16)
    ckv = _rms_norm(c[:, Q_LORA:Q_LORA + KV_LORA], gkv_ref[...]).astype(BF16)
    k_rope = _rope_slab(c[:, Q_LORA + KV_LORA:], cos, slo, shi, half)
    scale = (MLA_NOPE + MLA_ROPE) ** -0.5 * LOG2E
    for h in range(N_HEADS):
        sl = slice(h * LANES, (h + 1) * LANES)
        q = _rope_slab(_dot(cq, wuq_ref[:, sl]), cos, slo, shi, half)
        q_ref[:, sl] = (q * scale).astype(BF16)
        k_ref[:, sl] = (_dot(ckv, wuk_ref[:, sl]) + k_rope).astype(BF16)
    _store_values_t(vt_ref, _dot(ckv, wuv_ref[...]))


def _mla_proj(x2, wd, gq, gkv, wuq, wuk, wuv, tabs, seq):
    n, dm = x2.shape
    tiles_per_seq = seq // TM
    row = lambda width: pl.BlockSpec((TM, width), lambda i: (i, 0))
    tab = pl.BlockSpec((TM, LANES), lambda i: (i % tiles_per_seq, 0))
    wide = N_HEADS * LANES
    return pl.pallas_call(
        _mla_proj_kernel,
        grid=(n // TM,),
        in_specs=[row(dm)] + [_const_spec(a.shape) for a in (wd, gq, gkv, wuq, wuk, wuv)] + [tab] * 3,
        out_specs=[row(wide), row(wide), _values_t_spec(tiles_per_seq)],
        out_shape=[jax.ShapeDtypeStruct((n, wide), BF16), jax.ShapeDtypeStruct((n, wide), BF16),
                   _values_t_shape(n, seq)],
        compiler_params=_params(1),
        name="mla_proj",
    )(x2, wd, gq, gkv, wuq, wuk, wuv, *tabs)


def _prep_mla_w(w_dqkv, w_uq, w_ukv):
    dm = w_dqkv.shape[0]
    zeros = lambda r, c: jnp.zeros((r, c), w_dqkv.dtype)
    k_rope_w = w_dqkv[:, Q_LORA + KV_LORA:]
    wd = jnp.concatenate([w_dqkv[:, :Q_LORA + KV_LORA], zeros(dm, MLA_NOPE), k_rope_w,
                          zeros(dm, LANES - MLA_NOPE - MLA_ROPE)], axis=1)
    uq = w_uq.reshape(Q_LORA, N_HEADS, MLA_NOPE + MLA_ROPE)
    uq = jnp.pad(uq, ((0, 0), (0, 0), (0, LANES - MLA_NOPE - MLA_ROPE))).reshape(Q_LORA, N_HEADS * LANES)
    ukv = w_ukv.reshape(KV_LORA, N_HEADS, MLA_NOPE + MLA_V)
    uk = jnp.pad(ukv[:, :, :MLA_NOPE], ((0, 0), (0, 0), (0, LANES - MLA_NOPE))).reshape(KV_LORA, N_HEADS * LANES)
    uv = ukv[:, :, MLA_NOPE:].reshape(KV_LORA, N_HEADS * MLA_V)
    return wd.astype(BF16), uq.astype(BF16), uk.astype(BF16), uv.astype(BF16)


def _attend_slabs(slabs, n_plain, causal, qpos, tk, one_pass):
    m2 = slabs[0][0].shape[0]
    tq = m2 // 2
    kiota = lax.broadcasted_iota(I32, (tk, m2), 0)

    def step(c, carries, masked, stale):
        scores = [_dot_nt(slab[1](c), slab[0]) for slab in slabs]
        mask = (c * tk + kiota <= qpos) if masked else None
        out = []
        for (_, _, vt_load, bias_fn, qbias), (m, acc), s in zip(slabs, carries, scores):
            if bias_fn is not None:
                s = bias_fn(c) + s
            if mask is not None:
                s = jnp.where(mask, s, NEG)
            smax = jnp.max(s, axis=0, keepdims=True)
            m_new = jnp.maximum(m, smax if qbias is None else smax + qbias)
            alpha = jnp.exp2(m - m_new)
            shift = m if stale else m_new
            p = jnp.exp2(s - (shift if qbias is None else shift - qbias)).astype(BF16)
            if stale:
                acc = (acc + _dot(vt_load(c), p)) * alpha
            else:
                acc = alpha * acc + _dot(vt_load(c), p)
            out.append((m_new, acc))
        return tuple(out)

    def first_shift(c, masked):
        sub_iota = lax.broadcasted_iota(I32, (SHIFT_KEYS, m2), 0)
        mask = (c * tk + sub_iota <= qpos) if masked else None
        shifts = []
        for q2, k_load, _, _, qbias in slabs:
            s = _dot_nt(k_load(c, SHIFT_KEYS), q2)
            if mask is not None:
                s = jnp.where(mask, s, NEG)
            smax = jnp.max(s, axis=0, keepdims=True)
            shifts.append(smax if qbias is None else smax + qbias)
        return shifts

    zero_acc = jnp.zeros((VT_ROWS, m2), F32)
    carries = tuple((jnp.full((1, m2), NEG, F32), zero_acc) for _ in slabs)
    if one_pass and causal:
        if slabs[0][3] is None:
            carries = tuple((m0, zero_acc) for m0 in first_shift(n_plain, True))
            carries = step(n_plain, carries, True, True)
        else:
            carries = step(n_plain, carries, True, False)
        carries = lax.fori_loop(0, n_plain, lambda t, cr: step(n_plain - 1 - t, cr, False, True), carries)
    elif one_pass:
        carries = tuple((m0, zero_acc) for m0 in first_shift(0, False))
        carries = lax.fori_loop(0, n_plain, lambda c, cr: step(c, cr, False, True), carries)
    else:
        carries = lax.fori_loop(0, n_plain, lambda c, cr: step(c, cr, False, False), carries)
        if causal:
            carries = step(n_plain, carries, True, False)
    row = lax.broadcasted_iota(I32, (LANES, tq), 0)
    outs = []
    for _, acc in carries:
        o = acc[:LANES] / acc[LANES:LANES + 1]
        outs.append(jnp.where(row < HEAD_DIM, o[:, :tq], o[:, tq:]))
    return outs


def _attend_to(o_ref, slabs, n_plain, causal, qpos, tk, one_pass):
    def run(one_pass):
        outs = _attend_slabs(slabs, n_plain, causal, qpos, tk, one_pass)
        for j, out in enumerate(outs):
            o_ref[0, :, j * LANES:(j + 1) * LANES] = out.T.astype(o_ref.dtype)
        return outs

    if not one_pass:
        run(False)
        return
    outs = run(True)
    total = functools.reduce(lambda a, b: a + b, [jnp.sum(o, axis=0, keepdims=True) for o in outs])
    overflowed = jnp.max(jnp.where(jnp.isfinite(total), 0.0, 1.0)) > 0.0

    @pl.when(overflowed)
    def _():
        run(False)


def _split_heads(q_slab):
    lane = lax.broadcasted_iota(I32, q_slab.shape, 1)
    zero = jnp.zeros_like(q_slab)
    return jnp.where(lane < HEAD_DIM, q_slab, zero), jnp.where(lane >= HEAD_DIM, q_slab, zero)


def _kv_chunk(ref, c, tk, lanes, rows=None):
    return ref[0, pl.ds(pl.multiple_of(c * tk, tk), tk if rows is None else rows), lanes]


def _per_batch_spec(a):
    nd = a.ndim
    return pl.BlockSpec((1,) + a.shape[1:], lambda bi, i: (bi,) + (0,) * (nd - 1),
                        pipeline_mode=pl.Buffered(1))


def _query_positions(row0, tq):
    pos = row0 + lax.broadcasted_iota(I32, (1, tq), 1)
    return jnp.concatenate([pos, pos], axis=1)


def _fox_attn_kernel(q_ref, k_ref, vt_ref, cq_ref, ck_ref, o_ref, *, tq, tk):
    i = pl.program_id(1)
    row0 = i * tq
    diag = row0 // tk
    qpos = _query_positions(row0, tq)
    cq = cq_ref[0]

    def slab(j):
        lanes = slice(j * LANES, (j + 1) * LANES)
        q2 = jnp.concatenate(_split_heads(q_ref[0, :, lanes]), axis=0)
        cq2 = jnp.concatenate([cq[2 * j:2 * j + 1, :], cq[2 * j + 1:2 * j + 2, :]], axis=1)

        def bias_fn(c):
            ck = -ck_ref[0, pl.ds(pl.multiple_of(c * tk, tk), tk), :]
            return jnp.concatenate([jnp.broadcast_to(ck[:, 2 * j:2 * j + 1], (tk, tq)),
                                    jnp.broadcast_to(ck[:, 2 * j + 1:2 * j + 2], (tk, tq))], axis=1)

        return (q2, lambda c, rows=None: _kv_chunk(k_ref, c, tk, lanes, rows),
                lambda c: vt_ref[0, c, j * VT_ROWS:(j + 1) * VT_ROWS, :], bias_fn, cq2)

    _attend_to(o_ref, [slab(j) for j in range(N_SLABS)], diag, True, qpos, tk, True)


def _fox_attn(q, k, vt, cum_q, cum_k):
    b, s, d = q.shape
    tq, tk = min(TQ, s), vt.shape[3]
    return pl.pallas_call(
        functools.partial(_fox_attn_kernel, tq=tq, tk=tk),
        grid=(b, s // tq),
        in_specs=[pl.BlockSpec((1, tq, d), lambda bi, i: (bi, i, 0)),
                  _per_batch_spec(k), _per_batch_spec(vt),
                  pl.BlockSpec((1, N_HEADS, tq), lambda bi, i: (bi, 0, i)),
                  _per_batch_spec(cum_k)],
        out_specs=pl.BlockSpec((1, tq, d), lambda bi, i: (bi, i, 0)),
        out_shape=jax.ShapeDtypeStruct((b, s, d), BF16),
        compiler_params=_params(2),
        name="fox_attn",
    )(q, k, vt, cum_q, cum_k)


def _mla_attn_kernel(q_ref, k_ref, vt_ref, o_ref, *, tq, tk):
    i = pl.program_id(1)
    row0 = i * tq
    diag = row0 // tk
    qpos = _query_positions(row0, tq)
    zero = jnp.zeros((tq, LANES), BF16)

    def slab(j):
        pair = slice(2 * j * LANES, (2 * j + 2) * LANES)
        qa = q_ref[0, :, 2 * j * LANES:(2 * j + 1) * LANES]
        qb = q_ref[0, :, (2 * j + 1) * LANES:(2 * j + 2) * LANES]
        q2 = jnp.concatenate([jnp.concatenate([qa, zero], axis=1),
                              jnp.concatenate([zero, qb], axis=1)], axis=0)
        lanes = slice(j * LANES, (j + 1) * LANES)
        return (q2, lambda c, rows=None: _kv_chunk(k_ref, c, tk, pair, rows),
                lambda c: vt_ref[0, c, j * VT_ROWS:(j + 1) * VT_ROWS, :], None, None)

    _attend_to(o_ref, [slab(j) for j in range(N_SLABS)], diag, True, qpos, tk, True)


def _mla_attn(q, k, vt):
    b, s, wide = q.shape
    d = MIX_WIDTH
    tq, tk = min(TQ, s), vt.shape[3]
    return pl.pallas_call(
        functools.partial(_mla_attn_kernel, tq=tq, tk=tk),
        grid=(b, s // tq),
        in_specs=[pl.BlockSpec((1, tq, wide), lambda bi, i: (bi, i, 0)),
                  _per_batch_spec(k), _per_batch_spec(vt)],
        out_specs=pl.BlockSpec((1, tq, d), lambda bi, i: (bi, i, 0)),
        out_shape=jax.ShapeDtypeStruct((b, s, d), BF16),
        compiler_params=_params(2),
        name="mla_attn",
    )(q, k, vt)


def _sortable_key(score):
    bits = pltpu.bitcast(score + 0.0, I32)
    return jnp.where(bits < 0, bits ^ 0x7FFFFFFF, bits)


def _count_keys(ref, n_chunks, indicator, acc_rows):
    _, tk, tq = ref.shape

    def body(c, acc):
        ind = indicator(ref[c], c)
        for r in range(tk // acc_rows):
            acc = acc + ind[r * acc_rows:(r + 1) * acc_rows]
        return acc

    acc = lax.fori_loop(0, n_chunks, body, jnp.zeros((acc_rows, tq), ref.dtype))
    return jnp.sum(acc.astype(F32), axis=0, keepdims=True)


def _radix_select16(ref, n_chunks, target):
    tq = ref.shape[2]
    one, zero = jnp.int16(1), jnp.int16(0)

    def signed(u):
        return lax.shift_right_arithmetic(lax.shift_left(u ^ 0x8000, 16), 16)

    def bit_step(t, carry):
        prefix, n_above = carry
        cand = prefix | lax.shift_left(jnp.int32(1), 15 - t)
        cand16 = signed(cand).astype(I16)
        cnt = _count_keys(ref, n_chunks, lambda blk, c: jnp.where(blk >= cand16, one, zero), 64)
        accept = cnt >= target
        return jnp.where(accept, cand, prefix), jnp.where(accept, n_above, cnt)

    prefix, n_above = lax.fori_loop(0, 16, bit_step, (jnp.zeros((1, tq), I32), jnp.zeros((1, tq), F32)))
    return signed(prefix), n_above


def _dsa_attn_kernel(q_ref, k_ref, vt_ref, qi_ref, ki_ref, wit_ref, tri_ref, o_ref,
                     qim_ref, key_ref, hi_ref, lo_ref, bias_ref, *, tq, tk, topk):
    i = pl.program_id(1)
    row0 = i * tq
    n_chunks = row0 // tk + 1
    qpos = row0 + lax.broadcasted_iota(I32, (1, tq), 1)
    kiota = lax.broadcasted_iota(I32, (tk, tq), 0)

    for h in range(IDX_HEADS):
        pair = _split_heads(qi_ref[0, :, (h // 2) * LANES:(h // 2 + 1) * LANES])
        qim_ref[h] = pair[h % 2]
    wit = wit_ref[0]

    def score_chunk(c, _):
        ki = _kv_chunk(ki_ref, c, tk, slice(None))
        acc = jnp.zeros((tk, tq), F32)
        for h in range(IDX_HEADS):
            acc = acc + wit[h:h + 1, :] * jnp.maximum(_dot_nt(ki, qim_ref[h]), 0.0)
        key = jnp.where(c * tk + kiota <= qpos, _sortable_key(acc), INT_MIN)
        key_ref[c] = key
        hi_ref[c] = lax.shift_right_arithmetic(key, 16).astype(I16)
        return 0

    lax.fori_loop(0, n_chunks, score_chunk, 0)

    k_f = float(topk)
    thr_hi, n_gt_hi = _radix_select16(hi_ref, n_chunks, k_f)

    def low_plane_chunk(c, _):
        key = key_ref[c]
        low = (key & 0xFFFF) - 0x8000
        in_bucket = lax.shift_right_arithmetic(key, 16) == thr_hi
        lo_ref[c] = jnp.where(in_bucket, low, -0x8000).astype(I16)
        return 0

    lax.fori_loop(0, n_chunks, low_plane_chunk, 0)
    thr_lo, n_gt_lo = _radix_select16(lo_ref, n_chunks, k_f - n_gt_hi)
    live = thr_hi != -0x8000
    thr = jnp.where(live, thr_hi * 0x10000 + (thr_lo + 0x8000), INT_MIN)
    n_gt = n_gt_hi + n_gt_lo

    def count(indicator):
        return _count_keys(key_ref, n_chunks, indicator, 32)

    n_eq = count(lambda blk, c: jnp.where(blk == thr, 1, 0))
    need = k_f - n_gt

    has_ties = jnp.max(jnp.where(live, n_eq - need, 0.0)) > 0.0

    need_ties = jnp.where(live, need, 0.0)

    def bias_chunk_ties(c, before):
        blk = key_ref[c]
        is_thr = blk == thr
        eq = jnp.where(is_thr, 1.0, 0.0)
        rank = _dot(tri_ref[...], eq.astype(BF16)) + before
        tie_ok = jnp.where(rank < need_ties, 0.0, NEG)
        bias_ref[c] = jnp.where(blk > thr, 0.0, jnp.where(is_thr, tie_ok, NEG))
        return before + jnp.sum(eq, axis=0, keepdims=True)

    thr_all = jnp.where(live, thr, INT_MIN + 1)

    def bias_chunk_no_ties(c, _):
        bias_ref[c] = jnp.where(key_ref[c] >= thr_all, 0.0, NEG)
        return 0

    @pl.when(has_ties)
    def _():
        lax.fori_loop(0, n_chunks, bias_chunk_ties, jnp.zeros((1, tq), F32))

    @pl.when(jnp.logical_not(has_ties))
    def _():
        lax.fori_loop(0, n_chunks, bias_chunk_no_ties, 0)

    def bias_fn(c):
        b = bias_ref[c]
        return jnp.concatenate([b, b], axis=1)

    def slab(j):
        lanes = slice(j * LANES, (j + 1) * LANES)
        q2 = jnp.concatenate(_split_heads(q_ref[0, :, lanes]), axis=0)
        return (q2, lambda c, rows=None: _kv_chunk(k_ref, c, tk, lanes, rows),
                lambda c: vt_ref[0, c, j * VT_ROWS:(j + 1) * VT_ROWS, :], bias_fn, None)

    _attend_to(o_ref, [slab(j) for j in range(N_SLABS)], n_chunks, False, None, tk, True)


def _dsa_attn(q, k, vt, qi, ki, wit):
    b, s, d = q.shape
    tq, tk = min(TQ, s), vt.shape[3]
    topk = min(TOPK_MAX, s // 4)
    per_q = lambda width: pl.BlockSpec((1, tq, width), lambda bi, i: (bi, i, 0))
    tri = jnp.tril(jnp.ones((tk, tk), BF16), -1)
    return pl.pallas_call(
        functools.partial(_dsa_attn_kernel, tq=tq, tk=tk, topk=topk),
        grid=(b, s // tq),
        in_specs=[per_q(d), _per_batch_spec(k), _per_batch_spec(vt),
                  per_q(qi.shape[2]), _per_batch_spec(ki),
                  pl.BlockSpec((1, IDX_HEADS, tq), lambda bi, i: (bi, 0, i)),
                  _const_spec(tri.shape)],
        out_specs=per_q(d),
        out_shape=jax.ShapeDtypeStruct((b, s, d), BF16),
        scratch_shapes=[pltpu.VMEM((IDX_HEADS, tq, LANES), BF16),
                        pltpu.VMEM((s // tk, tk, tq), I32),
                        pltpu.VMEM((s // tk, tk, tq), I16),
                        pltpu.VMEM((s // tk, tk, tq), I16),
                        pltpu.VMEM((s // tk, tk, tq), F32)],
        compiler_params=_params(2),
        name="dsa_attn",
    )(q, k, vt, qi, ki, wit, tri)


def _dsa_mixer(x2, b, s, w_in, tabs):
    q, k, vt, qi, ki, wi = _dsa_proj(x2, _prep_dsa_w(w_in), tabs, s)
    r3 = lambda a: a.reshape(b, s, a.shape[1])
    wit = r3(wi)[:, :, :IDX_HEADS].transpose(0, 2, 1)
    o = _dsa_attn(r3(q), r3(k), vt, r3(qi), r3(ki), wit)
    return o.reshape(b * s, MIX_WIDTH)


def _fox_mixer(x2, b, s, w_in, b_f):
    d = MIX_WIDTH
    wf = jnp.pad(w_in[:, 3 * d:], ((0, 0), (0, LANES - N_HEADS))).astype(BF16)
    bf = jnp.pad(b_f, (0, LANES - N_HEADS)).reshape(1, LANES)
    q, k, vt, cum = _fox_proj(x2, w_in[:, :3 * d].astype(BF16), wf, bf, s)
    r3 = lambda a: a.reshape(b, s, a.shape[1])
    cum_q = r3(cum)[:, :, :N_HEADS].transpose(0, 2, 1)
    o = _fox_attn(r3(q), r3(k), vt, cum_q, r3(cum))
    return o.reshape(b * s, d)


def _mla_mixer(x2, b, s, w_dqkv, gq, w_uq, gkv, w_ukv, tabs):
    wd, uq, uk, uv = _prep_mla_w(w_dqkv, w_uq, w_ukv)
    q, k, vt = _mla_proj(x2, wd, gq.reshape(1, -1), gkv.reshape(1, -1), uq, uk, uv, tabs, s)
    r3 = lambda a: a.reshape(b, s, a.shape[1])
    o = _mla_attn(r3(q), r3(k), vt)
    return o.reshape(b * s, MIX_WIDTH)


def kernel(x, ffn1_w13, ffn1_w2, ffn2_w13, ffn2_w2, ln_g, ln_b, w_out, dsa_w_in, fox_w_in, fox_b_f,
           mla_w_dqkv, mla_q_norm_g, mla_w_uq, mla_kv_norm_g, mla_w_ukv):
    b, s, dm = x.shape
    x2 = x.reshape(b * s, dm)
    tabs_p = _rope_tables(s, ROT_DIM, 0, HEAD_DIM)
    tabs_m = _rope_tables(s, MLA_ROPE, MLA_NOPE, LANES)
    for i in range(DEPTH):
        g = lambda r, i=i: ln_g[i, r].reshape(1, dm)
        be = lambda r, i=i: ln_b[i, r].reshape(1, dm)
        x2 = _ffn_ln(x2, ffn1_w13[i], ffn1_w2[i], g(0), be(0))
        kind, j = i % N_MIXERS, i // N_MIXERS
        if kind == 0:
            o = _dsa_mixer(x2, b, s, dsa_w_in[j], tabs_p)
        elif kind == 1:
            o = _fox_mixer(x2, b, s, fox_w_in[j], fox_b_f[j])
        else:
            o = _mla_mixer(x2, b, s, mla_w_dqkv[j], mla_q_norm_g[j], mla_w_uq[j],
                           mla_kv_norm_g[j], mla_w_ukv[j], tabs_m)
        x2 = _mix_ffn_ln(x2, o, w_out[i], g(1), be(1), ffn2_w13[i], ffn2_w2[i], g(2), be(2))
    return x2.reshape(b, s, dm)
```

```python
import functools

import jax
import jax.numpy as jnp
from jax import lax
from jax.experimental import pallas as pl
from jax.experimental.pallas import tpu as pltpu

F32 = jnp.float32
BF16 = jnp.bfloat16
I32 = jnp.int32
I16 = jnp.int16

D_MODEL = 1024
DEPTH = 4
N_MIXERS = 3
HEAD_DIM = 64
N_HEADS = D_MODEL // HEAD_DIM
MIX_WIDTH = N_HEADS * HEAD_DIM
ROT_DIM = HEAD_DIM // 4
ROPE_THETA = 500000.0
IDX_HEADS = 8
IDX_DIM = 64
TOPK_MAX = 256
MLA_NOPE = 64
MLA_ROPE = 32
MLA_V = 64
Q_LORA = 384
KV_LORA = 256
D_FF = 2816
ALPHA = (2.0 * DEPTH) ** 0.25
LN_EPS = 1e-5
RMS_EPS = 1e-6

LANES = 128
N_SLABS = MIX_WIDTH // LANES
VT_ROWS = LANES + 16
LOG2E = 1.4426950408889634
NEG = -1e30
INT_MIN = -2147483648

TM = 512
TM_FFN = 2 * TM
TF = 256
TQ = 256
SHIFT_KEYS = 64
VMEM_LIMIT = 56 * 1024 * 1024


def _params(n_axes, vmem=VMEM_LIMIT):
    return pltpu.CompilerParams(dimension_semantics=("arbitrary",) * n_axes, vmem_limit_bytes=vmem)


def _const_spec(shape):
    nd = len(shape)
    return pl.BlockSpec(shape, lambda *_: (0,) * nd, pipeline_mode=pl.Buffered(1))


def _dot(a, b):
    return jnp.dot(a, b, preferred_element_type=F32)


def _dot_nt(a, b):
    return lax.dot_general(a, b, (((1,), (1,)), ((), ())), preferred_element_type=F32)


def _layer_norm(z, g, b):
    mu = jnp.mean(z, axis=-1, keepdims=True)
    d = z - mu
    var = jnp.mean(d * d, axis=-1, keepdims=True)
    return d * lax.rsqrt(var + LN_EPS) * g + b


def _rms_norm(z, g):
    ms = jnp.mean(z * z, axis=-1, keepdims=True)
    return z * lax.rsqrt(ms + RMS_EPS) * g


def _rope_slab(y, cos, sin_lo, sin_hi, half):
    return y * cos + pltpu.roll(y, half, 1) * sin_hi + pltpu.roll(y, LANES - half, 1) * sin_lo


def _store_values_t(vt_ref, v):
    tm = v.shape[0]
    extra = jnp.where(lax.broadcasted_iota(I32, (VT_ROWS - LANES, tm), 0) == 0, 1.0, 0.0).astype(BF16)
    for j in range(N_SLABS):
        vt_ref[0, 0, j * VT_ROWS:j * VT_ROWS + LANES, :] = v[:, j * LANES:(j + 1) * LANES].T.astype(BF16)
        vt_ref[0, 0, j * VT_ROWS + LANES:(j + 1) * VT_ROWS, :] = extra


def _values_t_spec(tiles_per_seq):
    return pl.BlockSpec((1, 1, N_SLABS * VT_ROWS, TM), lambda i: (i // tiles_per_seq, i % tiles_per_seq, 0, 0))


def _values_t_shape(n, seq):
    return jax.ShapeDtypeStruct((n // seq, seq // TM, N_SLABS * VT_ROWS, TM), BF16)


def _ffn_sublayer(x, w13_ref, w2_ref, g, b):
    xb = x.astype(BF16)
    ff = w2_ref.shape[0]
    acc = jnp.zeros(x.shape, F32)
    for c0 in range(0, ff, TF):
        gate = _dot(xb, w13_ref[:, c0:c0 + TF])
        up = _dot(xb, w13_ref[:, ff + c0:ff + c0 + TF])
        h = gate * jax.nn.sigmoid(gate) * up
        acc = acc + _dot(h.astype(BF16), w2_ref[c0:c0 + TF, :])
    return _layer_norm(ALPHA * x + 0.5 * acc, g, b)


def _row_tiles(ref):
    return [slice(r, r + TM) for r in range(0, ref.shape[0], TM)]


def _ffn_ln_kernel(x_ref, w13_ref, w2_ref, g_ref, b_ref, o_ref):
    for rows in _row_tiles(x_ref):
        o_ref[rows, :] = _ffn_sublayer(x_ref[rows, :], w13_ref, w2_ref, g_ref[...], b_ref[...])


def _ffn_ln(x2, w13, w2, g, b):
    n, d = x2.shape
    return pl.pallas_call(
        _ffn_ln_kernel,
        grid=(n // TM_FFN,),
        in_specs=[pl.BlockSpec((TM_FFN, d), lambda i: (i, 0)),
                  _const_spec(w13.shape), _const_spec(w2.shape),
                  _const_spec(g.shape), _const_spec(b.shape)],
        out_specs=pl.BlockSpec((TM_FFN, d), lambda i: (i, 0)),
        out_shape=jax.ShapeDtypeStruct((n, d), F32),
        compiler_params=_params(1),
        name="ffn_ln",
    )(x2, w13.astype(BF16), w2.astype(BF16), g, b)


def _mix_ffn_ln_kernel(x_ref, o_ref, wo_ref, g1_ref, b1_ref, w13_ref, w2_ref, g2_ref, b2_ref, out_ref):
    for rows in _row_tiles(x_ref):
        y = _dot(o_ref[rows, :], wo_ref[...])
        x1 = _layer_norm(ALPHA * x_ref[rows, :] + y, g1_ref[...], b1_ref[...])
        out_ref[rows, :] = _ffn_sublayer(x1, w13_ref, w2_ref, g2_ref[...], b2_ref[...])


def _mix_ffn_ln(x2, o2, wo, g1, b1, w13, w2, g2, b2):
    n, d = x2.shape
    consts = (wo.astype(BF16), g1, b1, w13.astype(BF16), w2.astype(BF16), g2, b2)
    return pl.pallas_call(
        _mix_ffn_ln_kernel,
        grid=(n // TM_FFN,),
        in_specs=[pl.BlockSpec((TM_FFN, d), lambda i: (i, 0)),
                  pl.BlockSpec((TM_FFN, o2.shape[1]), lambda i: (i, 0))] + [_const_spec(a.shape) for a in consts],
        out_specs=pl.BlockSpec((TM_FFN, d), lambda i: (i, 0)),
        out_shape=jax.ShapeDtypeStruct((n, d), F32),
        compiler_params=_params(1),
        name="mix_ffn_ln",
    )(x2, o2, *consts)


def _rope_tables(seq, rot, lane_of_rot0, period):
    half = rot // 2
    inv = ROPE_THETA ** (-jnp.arange(0, rot, 2, dtype=F32) / rot)
    ang = jnp.arange(seq, dtype=F32)[:, None] * inv[None, :]
    cos_h, sin_h = jnp.cos(ang), jnp.sin(ang)
    r = (jnp.arange(LANES) % period) - lane_of_rot0
    in_lo = (r >= 0) & (r < half)
    in_hi = (r >= half) & (r < rot)
    idx = jnp.clip(jnp.where(in_hi, r - half, r), 0, half - 1)
    cos = jnp.where((in_lo | in_hi)[None, :], cos_h[:, idx], 1.0)
    sin_lo = jnp.where(in_lo[None, :], -sin_h[:, idx], 0.0)
    sin_hi = jnp.where(in_hi[None, :], sin_h[:, idx], 0.0)
    return cos, sin_lo, sin_hi


def _dsa_proj_kernel(x_ref, w_ref, cos_ref, slo_ref, shi_ref,
                     q_ref, k_ref, vt_ref, qi_ref, ki_ref, wi_ref):
    xb = x_ref[...].astype(BF16)
    cos, slo, shi = cos_ref[...], slo_ref[...], shi_ref[...]
    half = ROT_DIM // 2
    d = MIX_WIDTH
    di = IDX_HEADS * IDX_DIM

    def roped(col0, width, scale, out_ref):
        y = _dot(xb, w_ref[:, col0:col0 + width])
        for j in range(width // LANES):
            s = _rope_slab(y[:, j * LANES:(j + 1) * LANES], cos, slo, shi, half)
            if scale != 1.0:
                s = s * scale
            out_ref[:, j * LANES:(j + 1) * LANES] = s.astype(out_ref.dtype)

    roped(0, d, HEAD_DIM ** -0.5 * LOG2E, q_ref)
    roped(d, d, 1.0, k_ref)
    _store_values_t(vt_ref, _dot(xb, w_ref[:, 2 * d:3 * d]))
    roped(3 * d, di, IDX_DIM ** -0.5, qi_ref)
    roped(3 * d + di, LANES, 1.0, ki_ref)
    wi_ref[...] = _dot(xb, w_ref[:, 3 * d + di + LANES:]) * (IDX_HEADS ** -0.5)


def _dsa_proj(x2, w, tabs, seq):
    n, dm = x2.shape
    d, di = MIX_WIDTH, IDX_HEADS * IDX_DIM
    tiles_per_seq = seq // TM
    row = lambda width: pl.BlockSpec((TM, width), lambda i: (i, 0))
    tab = pl.BlockSpec((TM, LANES), lambda i: (i % tiles_per_seq, 0))
    return pl.pallas_call(
        _dsa_proj_kernel,
        grid=(n // TM,),
        in_specs=[row(dm), _const_spec(w.shape), tab, tab, tab],
        out_specs=[row(d), row(d), _values_t_spec(tiles_per_seq), row(di), row(LANES), row(LANES)],
        out_shape=[jax.ShapeDtypeStruct((n, d), BF16), jax.ShapeDtypeStruct((n, d), BF16),
                   _values_t_shape(n, seq), jax.ShapeDtypeStruct((n, di), BF16),
                   jax.ShapeDtypeStruct((n, LANES), BF16), jax.ShapeDtypeStruct((n, LANES), F32)],
        compiler_params=_params(1),
        name="dsa_proj",
    )(x2, w, *tabs)


def _prep_dsa_w(w_in):
    d, di = MIX_WIDTH, IDX_HEADS * IDX_DIM
    ki = w_in[:, 3 * d + di:3 * d + di + IDX_DIM]
    wi = w_in[:, 3 * d + di + IDX_DIM:]
    pad = jnp.zeros((w_in.shape[0], LANES - IDX_HEADS), w_in.dtype)
    return jnp.concatenate([w_in[:, :3 * d + di], ki, ki, wi, pad], axis=1).astype(BF16)


def _fox_proj_kernel(x_ref, w_ref, wf_ref, bf_ref, q_ref, k_ref, vt_ref, cum_ref, carry_ref,
                     *, tiles_per_seq):
    i = pl.program_id(0)
    xb = x_ref[...].astype(BF16)
    d = MIX_WIDTH
    q_ref[...] = (_dot(xb, w_ref[:, :d]) * (HEAD_DIM ** -0.5 * LOG2E)).astype(BF16)
    k_ref[...] = _dot(xb, w_ref[:, d:2 * d]).astype(BF16)
    _store_values_t(vt_ref, _dot(xb, w_ref[:, 2 * d:]))

    f = _dot(xb, wf_ref[...]) + bf_ref[...]
    log_f = (jnp.minimum(f, 0.0) - jnp.log1p(jnp.exp(-jnp.abs(f)))) * LOG2E
    tm = log_f.shape[0]
    tri = (lax.broadcasted_iota(I32, (tm, tm), 0) >= lax.broadcasted_iota(I32, (tm, tm), 1))
    tri = jnp.where(tri, 1.0, 0.0).astype(BF16)
    p0 = log_f.astype(BF16)
    r1 = log_f - p0.astype(F32)
    p1 = r1.astype(BF16)
    p2 = (r1 - p1.astype(F32)).astype(BF16)
    local = _dot(tri, p0) + _dot(tri, p1) + _dot(tri, p2)

    @pl.when(i % tiles_per_seq == 0)
    def _():
        carry_ref[...] = jnp.zeros_like(carry_ref)

    cum = local + carry_ref[...]
    cum_ref[...] = cum
    carry_ref[...] = cum[tm - 1:tm, :]


def _fox_proj(x2, w, wf, bf, seq):
    n, dm = x2.shape
    d = MIX_WIDTH
    row = lambda width: pl.BlockSpec((TM, width), lambda i: (i, 0))
    return pl.pallas_call(
        functools.partial(_fox_proj_kernel, tiles_per_seq=seq // TM),
        grid=(n // TM,),
        in_specs=[row(dm), _const_spec(w.shape), _const_spec(wf.shape), _const_spec(bf.shape)],
        out_specs=[row(d), row(d), _values_t_spec(seq // TM), row(LANES)],
        out_shape=[jax.ShapeDtypeStruct((n, d), BF16), jax.ShapeDtypeStruct((n, d), BF16),
                   _values_t_shape(n, seq), jax.ShapeDtypeStruct((n, LANES), F32)],
        scratch_shapes=[pltpu.VMEM((1, LANES), F32)],
        compiler_params=_params(1),
        name="fox_proj",
    )(x2, w, wf, bf)


def _mla_proj_kernel(x_ref, wd_ref, gq_ref, gkv_ref, wuq_ref, wuk_ref, wuv_ref,
                     cos_ref, slo_ref, shi_ref, q_ref, k_ref, vt_ref):
    xb = x_ref[...].astype(BF16)
    cos, slo, shi = cos_ref[...], slo_ref[...], shi_ref[...]
    half = MLA_ROPE // 2
    c = _dot(xb, wd_ref[...])
    cq = _rms_norm(c[:, :Q_LORA], gq_ref[...]).astype(BF16)
    ckv = _rms_norm(c[:, Q_LORA:Q_LORA + KV_LORA], gkv_ref[...]).astype(BF16)
    k_rope = _rope_slab(c[:, Q_LORA + KV_LORA:], cos, slo, shi, half)
    scale = (MLA_NOPE + MLA_ROPE) ** -0.5 * LOG2E
    for h in range(N_HEADS):
        sl = slice(h * LANES, (h + 1) * LANES)
        q = _rope_slab(_dot(cq, wuq_ref[:, sl]), cos, slo, shi, half)
        q_ref[:, sl] = (q * scale).astype(BF16)
        k_ref[:, sl] = (_dot(ckv, wuk_ref[:, sl]) + k_rope).astype(BF16)
    _store_values_t(vt_ref, _dot(ckv, wuv_ref[...]))


def _mla_proj(x2, wd, gq, gkv, wuq, wuk, wuv, tabs, seq):
    n, dm = x2.shape
    tiles_per_seq = seq // TM
    row = lambda width: pl.BlockSpec((TM, width), lambda i: (i, 0))
    tab = pl.BlockSpec((TM, LANES), lambda i: (i % tiles_per_seq, 0))
    wide = N_HEADS * LANES
    return pl.pallas_call(
        _mla_proj_kernel,
        grid=(n // TM,),
        in_specs=[row(dm)] + [_const_spec(a.shape) for a in (wd, gq, gkv, wuq, wuk, wuv)] + [tab] * 3,
        out_specs=[row(wide), row(wide), _values_t_spec(tiles_per_seq)],
        out_shape=[jax.ShapeDtypeStruct((n, wide), BF16), jax.ShapeDtypeStruct((n, wide), BF16),
                   _values_t_shape(n, seq)],
        compiler_params=_params(1),
        name="mla_proj",
    )(x2, wd, gq, gkv, wuq, wuk, wuv, *tabs)


def _prep_mla_w(w_dqkv, w_uq, w_ukv):
    dm = w_dqkv.shape[0]
    zeros = lambda r, c: jnp.zeros((r, c), w_dqkv.dtype)
    k_rope_w = w_dqkv[:, Q_LORA + KV_LORA:]
    wd = jnp.concatenate([w_dqkv[:, :Q_LORA + KV_LORA], zeros(dm, MLA_NOPE), k_rope_w,
                          zeros(dm, LANES - MLA_NOPE - MLA_ROPE)], axis=1)
    uq = w_uq.reshape(Q_LORA, N_HEADS, MLA_NOPE + MLA_ROPE)
    uq = jnp.pad(uq, ((0, 0), (0, 0), (0, LANES - MLA_NOPE - MLA_ROPE))).reshape(Q_LORA, N_HEADS * LANES)
    ukv = w_ukv.reshape(KV_LORA, N_HEADS, MLA_NOPE + MLA_V)
    uk = jnp.pad(ukv[:, :, :MLA_NOPE], ((0, 0), (0, 0), (0, LANES - MLA_NOPE))).reshape(KV_LORA, N_HEADS * LANES)
    uv = ukv[:, :, MLA_NOPE:].reshape(KV_LORA, N_HEADS * MLA_V)
    return wd.astype(BF16), uq.astype(BF16), uk.astype(BF16), uv.astype(BF16)


def _attend_slabs(slabs, n_plain, causal, qpos, tk, one_pass):
    m2 = slabs[0][0].shape[0]
    tq = m2 // 2
    kiota = lax.broadcasted_iota(I32, (tk, m2), 0)

    def step(c, carries, masked, stale):
        scores = [_dot_nt(slab[1](c), slab[0]) for slab in slabs]
        mask = (c * tk + kiota <= qpos) if masked else None
        out = []
        for (_, _, vt_load, bias_fn, qbias), (m, acc), s in zip(slabs, carries, scores):
            if bias_fn is not None:
                s = bias_fn(c) + s
            if mask is not None:
                s = jnp.where(mask, s, NEG)
            smax = jnp.max(s, axis=0, keepdims=True)
            m_new = jnp.maximum(m, smax if qbias is None else smax + qbias)
            alpha = jnp.exp2(m - m_new)
            shift = m if stale else m_new
            p = jnp.exp2(s - (shift if qbias is None else shift - qbias)).astype(BF16)
            if stale:
                acc = (acc + _dot(vt_load(c), p)) * alpha
            else:
                acc = alpha * acc + _dot(vt_load(c), p)
            out.append((m_new, acc))
        return tuple(out)

    def first_shift(c, masked):
        sub_iota = lax.broadcasted_iota(I32, (SHIFT_KEYS, m2), 0)
        mask = (c * tk + sub_iota <= qpos) if masked else None
        shifts = []
        for q2, k_load, _, _, qbias in slabs:
            s = _dot_nt(k_load(c, SHIFT_KEYS), q2)
            if mask is not None:
                s = jnp.where(mask, s, NEG)
            smax = jnp.max(s, axis=0, keepdims=True)
            shifts.append(smax if qbias is None else smax + qbias)
        return shifts

    zero_acc = jnp.zeros((VT_ROWS, m2), F32)
    carries = tuple((jnp.full((1, m2), NEG, F32), zero_acc) for _ in slabs)
    if one_pass and causal:
        if slabs[0][3] is None:
            carries = tuple((m0, zero_acc) for m0 in first_shift(n_plain, True))
            carries = step(n_plain, carries, True, True)
        else:
            carries = step(n_plain, carries, True, False)
        carries = lax.fori_loop(0, n_plain, lambda t, cr: step(n_plain - 1 - t, cr, False, True), carries)
    elif one_pass:
        carries = tuple((m0, zero_acc) for m0 in first_shift(0, False))
        carries = lax.fori_loop(0, n_plain, lambda c, cr: step(c, cr, False, True), carries)
    else:
        carries = lax.fori_loop(0, n_plain, lambda c, cr: step(c, cr, False, False), carries)
        if causal:
            carries = step(n_plain, carries, True, False)
    row = lax.broadcasted_iota(I32, (LANES, tq), 0)
    outs = []
    for _, acc in carries:
        o = acc[:LANES] / acc[LANES:LANES + 1]
        outs.append(jnp.where(row < HEAD_DIM, o[:, :tq], o[:, tq:]))
    return outs


def _attend_to(o_ref, slabs, n_plain, causal, qpos, tk, one_pass):
    def run(one_pass):
        outs = _attend_slabs(slabs, n_plain, causal, qpos, tk, one_pass)
        for j, out in enumerate(outs):
            o_ref[0, :, j * LANES:(j + 1) * LANES] = out.T.astype(o_ref.dtype)
        return outs

    if not one_pass:
        run(False)
        return
    outs = run(True)
    total = functools.reduce(lambda a, b: a + b, [jnp.sum(o, axis=0, keepdims=True) for o in outs])
    overflowed = jnp.max(jnp.where(jnp.isfinite(total), 0.0, 1.0)) > 0.0

    @pl.when(overflowed)
    def _():
        run(False)


def _split_heads(q_slab):
    lane = lax.broadcasted_iota(I32, q_slab.shape, 1)
    zero = jnp.zeros_like(q_slab)
    return jnp.where(lane < HEAD_DIM, q_slab, zero), jnp.where(lane >= HEAD_DIM, q_slab, zero)


def _kv_chunk(ref, c, tk, lanes, rows=None):
    return ref[0, pl.ds(pl.multiple_of(c * tk, tk), tk if rows is None else rows), lanes]


def _per_batch_spec(a):
    nd = a.ndim
    return pl.BlockSpec((1,) + a.shape[1:], lambda bi, i: (bi,) + (0,) * (nd - 1),
                        pipeline_mode=pl.Buffered(1))


def _query_positions(row0, tq):
    pos = row0 + lax.broadcasted_iota(I32, (1, tq), 1)
    return jnp.concatenate([pos, pos], axis=1)


def _fox_attn_kernel(q_ref, k_ref, vt_ref, cq_ref, ck_ref, o_ref, *, tq, tk):
    i = pl.program_id(1)
    row0 = i * tq
    diag = row0 // tk
    qpos = _query_positions(row0, tq)
    cq = cq_ref[0]

    def slab(j):
        lanes = slice(j * LANES, (j + 1) * LANES)
        q2 = jnp.concatenate(_split_heads(q_ref[0, :, lanes]), axis=0)
        cq2 = jnp.concatenate([cq[2 * j:2 * j + 1, :], cq[2 * j + 1:2 * j + 2, :]], axis=1)

        def bias_fn(c):
            ck = -ck_ref[0, pl.ds(pl.multiple_of(c * tk, tk), tk), :]
            return jnp.concatenate([jnp.broadcast_to(ck[:, 2 * j:2 * j + 1], (tk, tq)),
                                    jnp.broadcast_to(ck[:, 2 * j + 1:2 * j + 2], (tk, tq))], axis=1)

        return (q2, lambda c, rows=None: _kv_chunk(k_ref, c, tk, lanes, rows),
                lambda c: vt_ref[0, c, j * VT_ROWS:(j + 1) * VT_ROWS, :], bias_fn, cq2)

    _attend_to(o_ref, [slab(j) for j in range(N_SLABS)], diag, True, qpos, tk, True)


def _fox_attn(q, k, vt, cum_q, cum_k):
    b, s, d = q.shape
    tq, tk = min(TQ, s), vt.shape[3]
    return pl.pallas_call(
        functools.partial(_fox_attn_kernel, tq=tq, tk=tk),
        grid=(b, s // tq),
        in_specs=[pl.BlockSpec((1, tq, d), lambda bi, i: (bi, i, 0)),
                  _per_batch_spec(k), _per_batch_spec(vt),
                  pl.BlockSpec((1, N_HEADS, tq), lambda bi, i: (bi, 0, i)),
                  _per_batch_spec(cum_k)],
        out_specs=pl.BlockSpec((1, tq, d), lambda bi, i: (bi, i, 0)),
        out_shape=jax.ShapeDtypeStruct((b, s, d), BF16),
        compiler_params=_params(2),
        name="fox_attn",
    )(q, k, vt, cum_q, cum_k)


def _mla_attn_kernel(q_ref, k_ref, vt_ref, o_ref, *, tq, tk):
    i = pl.program_id(1)
    row0 = i * tq
    diag = row0 // tk
    qpos = _query_positions(row0, tq)
    zero = jnp.zeros((tq, LANES), BF16)

    def slab(j):
        pair = slice(2 * j * LANES, (2 * j + 2) * LANES)
        qa = q_ref[0, :, 2 * j * LANES:(2 * j + 1) * LANES]
        qb = q_ref[0, :, (2 * j + 1) * LANES:(2 * j + 2) * LANES]
        q2 = jnp.concatenate([jnp.concatenate([qa, zero], axis=1),
                              jnp.concatenate([zero, qb], axis=1)], axis=0)
        lanes = slice(j * LANES, (j + 1) * LANES)
        return (q2, lambda c, rows=None: _kv_chunk(k_ref, c, tk, pair, rows),
                lambda c: vt_ref[0, c, j * VT_ROWS:(j + 1) * VT_ROWS, :], None, None)

    _attend_to(o_ref, [slab(j) for j in range(N_SLABS)], diag, True, qpos, tk, True)


def _mla_attn(q, k, vt):
    b, s, wide = q.shape
    d = MIX_WIDTH
    tq, tk = min(TQ, s), vt.shape[3]
    return pl.pallas_call(
        functools.partial(_mla_attn_kernel, tq=tq, tk=tk),
        grid=(b, s // tq),
        in_specs=[pl.BlockSpec((1, tq, wide), lambda bi, i: (bi, i, 0)),
                  _per_batch_spec(k), _per_batch_spec(vt)],
        out_specs=pl.BlockSpec((1, tq, d), lambda bi, i: (bi, i, 0)),
        out_shape=jax.ShapeDtypeStruct((b, s, d), BF16),
        compiler_params=_params(2),
        name="mla_attn",
    )(q, k, vt)


def _sortable_key(score):
    bits = pltpu.bitcast(score + 0.0, I32)
    return jnp.where(bits < 0, bits ^ 0x7FFFFFFF, bits)


def _count_keys(ref, n_chunks, indicator, acc_rows):
    _, tk, tq = ref.shape

    def body(c, acc):
        ind = indicator(ref[c], c)
        for r in range(tk // acc_rows):
            acc = acc + ind[r * acc_rows:(r + 1) * acc_rows]
        return acc

    acc = lax.fori_loop(0, n_chunks, body, jnp.zeros((acc_rows, tq), ref.dtype))
    return jnp.sum(acc.astype(F32), axis=0, keepdims=True)


def _radix_select16(ref, n_chunks, target, n_all):
    tq = ref.shape[2]
    one, zero = jnp.int16(1), jnp.int16(0)

    def signed(u):
        return lax.shift_right_arithmetic(lax.shift_left(u ^ 0x8000, 16), 16)

    def bit_step(t, carry):
        prefix, n_at, n_above = carry
        cand = prefix | lax.shift_left(jnp.int32(1), 15 - t)
        cand16 = signed(cand).astype(I16)
        cnt = _count_keys(ref, n_chunks, lambda blk, c: jnp.where(blk >= cand16, one, zero), 64)
        accept = cnt >= target
        return jnp.where(accept, cand, prefix), jnp.where(accept, cnt, n_at), jnp.where(accept, n_above, cnt)

    prefix, n_at, n_above = lax.fori_loop(
        0, 16, bit_step, (jnp.zeros((1, tq), I32), n_all, jnp.zeros((1, tq), F32)))
    return signed(prefix), n_at, n_above


def _dsa_attn_kernel(q_ref, k_ref, vt_ref, qi_ref, ki_ref, wit_ref, tri_ref, o_ref,
                     qim_ref, key_ref, hi_ref, lo_ref, bias_ref, *, tq, tk, topk):
    i = pl.program_id(1)
    row0 = i * tq
    n_chunks = row0 // tk + 1
    qpos = row0 + lax.broadcasted_iota(I32, (1, tq), 1)
    kiota = lax.broadcasted_iota(I32, (tk, tq), 0)

    for h in range(IDX_HEADS):
        pair = _split_heads(qi_ref[0, :, (h // 2) * LANES:(h // 2 + 1) * LANES])
        qim_ref[h] = pair[h % 2]
    wit = wit_ref[0]

    def score_chunk(c, _):
        ki = _kv_chunk(ki_ref, c, tk, slice(None))
        acc = jnp.zeros((tk, tq), F32)
        for h in range(IDX_HEADS):
            acc = acc + wit[h:h + 1, :] * jnp.maximum(_dot_nt(ki, qim_ref[h]), 0.0)
        key = jnp.where(c * tk + kiota <= qpos, _sortable_key(acc), INT_MIN)
        key_ref[c] = key
        hi_ref[c] = lax.shift_right_arithmetic(key, 16).astype(I16)
        lo_ref[c] = ((key & 0xFFFF) - 0x8000).astype(I16)
        return 0

    lax.fori_loop(0, n_chunks, score_chunk, 0)

    k_f = float(topk)
    n_keys = (n_chunks * tk).astype(F32) + jnp.zeros((1, tq), F32)
    thr_hi, n_ge_hi, n_gt_hi = _radix_select16(hi_ref, n_chunks, k_f, n_keys)
    thr_hi16 = thr_hi.astype(I16)
    low_min = jnp.int16(-0x8000)

    def low_plane_chunk(c, _):
        lo_ref[c] = jnp.where(hi_ref[c] == thr_hi16, lo_ref[c], low_min)
        return 0

    lax.fori_loop(0, n_chunks, low_plane_chunk, 0)
    n_bucket = n_ge_hi - n_gt_hi
    thr_lo, n_ge_lo, n_gt_lo = _radix_select16(lo_ref, n_chunks, k_f - n_gt_hi, n_bucket)
    live = thr_hi != -0x8000
    thr = jnp.where(live, thr_hi * 0x10000 + (thr_lo + 0x8000), INT_MIN)
    n_gt = n_gt_hi + n_gt_lo
    n_eq = n_ge_lo - n_gt_lo
    need = k_f - n_gt

    has_ties = jnp.max(jnp.where(live, n_eq - need, 0.0)) > 0.0

    need_ties = jnp.where(live, need, 0.0)

    def bias_chunk_ties(c, before):
        blk = key_ref[c]
        is_thr = blk == thr
        eq = jnp.where(is_thr, 1.0, 0.0)
        rank = _dot(tri_ref[...], eq.astype(BF16)) + before
        tie_ok = jnp.where(rank < need_ties, 0.0, NEG)
        bias_ref[c] = jnp.where(blk > thr, 0.0, jnp.where(is_thr, tie_ok, NEG))
        return before + jnp.sum(eq, axis=0, keepdims=True)

    thr_all = jnp.where(live, thr, INT_MIN + 1)

    def bias_chunk_no_ties(c, _):
        bias_ref[c] = jnp.where(key_ref[c] >= thr_all, 0.0, NEG)
        return 0

    @pl.when(has_ties)
    def _():
        lax.fori_loop(0, n_chunks, bias_chunk_ties, jnp.zeros((1, tq), F32))

    @pl.when(jnp.logical_not(has_ties))
    def _():
        lax.fori_loop(0, n_chunks, bias_chunk_no_ties, 0)

    def bias_fn(c):
        b = bias_ref[c]
        return jnp.concatenate([b, b], axis=1)

    def slab(j):
        lanes = slice(j * LANES, (j + 1) * LANES)
        q2 = jnp.concatenate(_split_heads(q_ref[0, :, lanes]), axis=0)
        return (q2, lambda c, rows=None: _kv_chunk(k_ref, c, tk, lanes, rows),
                lambda c: vt_ref[0, c, j * VT_ROWS:(j + 1) * VT_ROWS, :], bias_fn, None)

    _attend_to(o_ref, [slab(j) for j in range(N_SLABS)], n_chunks, False, None, tk, True)


def _dsa_attn(q, k, vt, qi, ki, wit):
    b, s, d = q.shape
    tq, tk = min(TQ, s), vt.shape[3]
    topk = min(TOPK_MAX, s // 4)
    per_q = lambda width: pl.BlockSpec((1, tq, width), lambda bi, i: (bi, i, 0))
    tri = jnp.tril(jnp.ones((tk, tk), BF16), -1)
    return pl.pallas_call(
        functools.partial(_dsa_attn_kernel, tq=tq, tk=tk, topk=topk),
        grid=(b, s // tq),
        in_specs=[per_q(d), _per_batch_spec(k), _per_batch_spec(vt),
                  per_q(qi.shape[2]), _per_batch_spec(ki),
                  pl.BlockSpec((1, IDX_HEADS, tq), lambda bi, i: (bi, 0, i)),
                  _const_spec(tri.shape)],
        out_specs=per_q(d),
        out_shape=jax.ShapeDtypeStruct((b, s, d), BF16),
        scratch_shapes=[pltpu.VMEM((IDX_HEADS, tq, LANES), BF16),
                        pltpu.VMEM((s // tk, tk, tq), I32),
                        pltpu.VMEM((s // tk, tk, tq), I16),
                        pltpu.VMEM((s // tk, tk, tq), I16),
                        pltpu.VMEM((s // tk, tk, tq), F32)],
        compiler_params=_params(2),
        name="dsa_attn",
    )(q, k, vt, qi, ki, wit, tri)


def _dsa_mixer(x2, b, s, w_in, tabs):
    q, k, vt, qi, ki, wi = _dsa_proj(x2, _prep_dsa_w(w_in), tabs, s)
    r3 = lambda a: a.reshape(b, s, a.shape[1])
    wit = r3(wi)[:, :, :IDX_HEADS].transpose(0, 2, 1)
    o = _dsa_attn(r3(q), r3(k), vt, r3(qi), r3(ki), wit)
    return o.reshape(b * s, MIX_WIDTH)


def _fox_mixer(x2, b, s, w_in, b_f):
    d = MIX_WIDTH
    wf = jnp.pad(w_in[:, 3 * d:], ((0, 0), (0, LANES - N_HEADS))).astype(BF16)
    bf = jnp.pad(b_f, (0, LANES - N_HEADS)).reshape(1, LANES)
    q, k, vt, cum = _fox_proj(x2, w_in[:, :3 * d].astype(BF16), wf, bf, s)
    r3 = lambda a: a.reshape(b, s, a.shape[1])
    cum_q = r3(cum)[:, :, :N_HEADS].transpose(0, 2, 1)
    o = _fox_attn(r3(q), r3(k), vt, cum_q, r3(cum))
    return o.reshape(b * s, d)


def _mla_mixer(x2, b, s, w_dqkv, gq, w_uq, gkv, w_ukv, tabs):
    wd, uq, uk, uv = _prep_mla_w(w_dqkv, w_uq, w_ukv)
    q, k, vt = _mla_proj(x2, wd, gq.reshape(1, -1), gkv.reshape(1, -1), uq, uk, uv, tabs, s)
    r3 = lambda a: a.reshape(b, s, a.shape[1])
    o = _mla_attn(r3(q), r3(k), vt)
    return o.reshape(b * s, MIX_WIDTH)


def kernel(x, ffn1_w13, ffn1_w2, ffn2_w13, ffn2_w2, ln_g, ln_b, w_out, dsa_w_in, fox_w_in, fox_b_f,
           mla_w_dqkv, mla_q_norm_g, mla_w_uq, mla_kv_norm_g, mla_w_ukv):
    b, s, dm = x.shape
    x2 = x.reshape(b * s, dm)
    tabs_p = _rope_tables(s, ROT_DIM, 0, HEAD_DIM)
    tabs_m = _rope_tables(s, MLA_ROPE, MLA_NOPE, LANES)
    for i in range(DEPTH):
        g = lambda r, i=i: ln_g[i, r].reshape(1, dm)
        be = lambda r, i=i: ln_b[i, r].reshape(1, dm)
        x2 = _ffn_ln(x2, ffn1_w13[i], ffn1_w2[i], g(0), be(0))
        kind, j = i % N_MIXERS, i // N_MIXERS
        if kind == 0:
            o = _dsa_mixer(x2, b, s, dsa_w_in[j], tabs_p)
        elif kind == 1:
            o = _fox_mixer(x2, b, s, fox_w_in[j], fox_b_f[j])
        else:
            o = _mla_mixer(x2, b, s, mla_w_dqkv[j], mla_q_norm_g[j], mla_w_uq[j],
                           mla_kv_norm_g[j], mla_w_ukv[j], tabs_m)
        x2 = _mix_ffn_ln(x2, o, w_out[i], g(1), be(1), ffn2_w13[i], ffn2_w2[i], g(2), be(2))
    return x2.reshape(b, s, dm)
```

```python
import functools

import jax
import jax.numpy as jnp
from jax import lax
from jax.experimental import pallas as pl
from jax.experimental.pallas import tpu as pltpu

F32 = jnp.float32
BF16 = jnp.bfloat16
I32 = jnp.int32
I16 = jnp.int16

D_MODEL = 1024
DEPTH = 4
N_MIXERS = 3
HEAD_DIM = 64
N_HEADS = D_MODEL // HEAD_DIM
MIX_WIDTH = N_HEADS * HEAD_DIM
ROT_DIM = HEAD_DIM // 4
ROPE_THETA = 500000.0
IDX_HEADS = 8
IDX_DIM = 64
TOPK_MAX = 256
MLA_NOPE = 64
MLA_ROPE = 32
MLA_V = 64
Q_LORA = 384
KV_LORA = 256
D_FF = 2816
ALPHA = (2.0 * DEPTH) ** 0.25
LN_EPS = 1e-5
RMS_EPS = 1e-6

LANES = 128
N_SLABS = MIX_WIDTH // LANES
VT_ROWS = LANES + 16
LOG2E = 1.4426950408889634
NEG = -1e30
INT_MIN = -2147483648

TM = 512
TM_FFN = 2 * TM
TF = 256
TQ = 256
SHIFT_KEYS = 64
VMEM_LIMIT = 56 * 1024 * 1024


def _params(n_axes, vmem=VMEM_LIMIT):
    return pltpu.CompilerParams(dimension_semantics=("arbitrary",) * n_axes, vmem_limit_bytes=vmem)


def _const_spec(shape):
    nd = len(shape)
    return pl.BlockSpec(shape, lambda *_: (0,) * nd, pipeline_mode=pl.Buffered(1))


def _dot(a, b):
    return jnp.dot(a, b, preferred_element_type=F32)


def _dot_nt(a, b):
    return lax.dot_general(a, b, (((1,), (1,)), ((), ())), preferred_element_type=F32)


def _layer_norm(z, g, b):
    mu = jnp.mean(z, axis=-1, keepdims=True)
    d = z - mu
    var = jnp.mean(d * d, axis=-1, keepdims=True)
    return d * lax.rsqrt(var + LN_EPS) * g + b


def _rms_norm(z, g):
    ms = jnp.mean(z * z, axis=-1, keepdims=True)
    return z * lax.rsqrt(ms + RMS_EPS) * g


def _rope_slab(y, cos, sin_lo, sin_hi, half):
    return y * cos + pltpu.roll(y, half, 1) * sin_hi + pltpu.roll(y, LANES - half, 1) * sin_lo


def _store_values_t(vt_ref, v):
    tm = v.shape[0]
    extra = jnp.where(lax.broadcasted_iota(I32, (VT_ROWS - LANES, tm), 0) == 0, 1.0, 0.0).astype(BF16)
    for j in range(N_SLABS):
        vt_ref[0, 0, j * VT_ROWS:j * VT_ROWS + LANES, :] = v[:, j * LANES:(j + 1) * LANES].T.astype(BF16)
        vt_ref[0, 0, j * VT_ROWS + LANES:(j + 1) * VT_ROWS, :] = extra


def _values_t_spec(tiles_per_seq):
    return pl.BlockSpec((1, 1, N_SLABS * VT_ROWS, TM), lambda i: (i // tiles_per_seq, i % tiles_per_seq, 0, 0))


def _values_t_shape(n, seq):
    return jax.ShapeDtypeStruct((n // seq, seq // TM, N_SLABS * VT_ROWS, TM), BF16)


def _ffn_sublayer(x, w13_ref, w2_ref, g, b):
    xb = x.astype(BF16)
    ff = w2_ref.shape[0]
    acc = jnp.zeros(x.shape, F32)
    for c0 in range(0, ff, TF):
        gate = _dot(xb, w13_ref[:, c0:c0 + TF])
        up = _dot(xb, w13_ref[:, ff + c0:ff + c0 + TF])
        h = gate * jax.nn.sigmoid(gate) * up
        acc = acc + _dot(h.astype(BF16), w2_ref[c0:c0 + TF, :])
    return _layer_norm(ALPHA * x + 0.5 * acc, g, b)


def _row_tiles(ref):
    return [slice(r, r + TM) for r in range(0, ref.shape[0], TM)]


def _ffn_ln_kernel(x_ref, w13_ref, w2_ref, g_ref, b_ref, o_ref):
    for rows in _row_tiles(x_ref):
        o_ref[rows, :] = _ffn_sublayer(x_ref[rows, :], w13_ref, w2_ref, g_ref[...], b_ref[...])


def _ffn_ln(x2, w13, w2, g, b):
    n, d = x2.shape
    return pl.pallas_call(
        _ffn_ln_kernel,
        grid=(n // TM_FFN,),
        in_specs=[pl.BlockSpec((TM_FFN, d), lambda i: (i, 0)),
                  _const_spec(w13.shape), _const_spec(w2.shape),
                  _const_spec(g.shape), _const_spec(b.shape)],
        out_specs=pl.BlockSpec((TM_FFN, d), lambda i: (i, 0)),
        out_shape=jax.ShapeDtypeStruct((n, d), F32),
        compiler_params=_params(1),
        name="ffn_ln",
    )(x2, w13.astype(BF16), w2.astype(BF16), g, b)


def _mix_ffn_ln_kernel(x_ref, o_ref, wo_ref, g1_ref, b1_ref, w13_ref, w2_ref, g2_ref, b2_ref, out_ref):
    for rows in _row_tiles(x_ref):
        y = _dot(o_ref[rows, :], wo_ref[...])
        x1 = _layer_norm(ALPHA * x_ref[rows, :] + y, g1_ref[...], b1_ref[...])
        out_ref[rows, :] = _ffn_sublayer(x1, w13_ref, w2_ref, g2_ref[...], b2_ref[...])


def _mix_ffn_ln(x2, o2, wo, g1, b1, w13, w2, g2, b2):
    n, d = x2.shape
    consts = (wo.astype(BF16), g1, b1, w13.astype(BF16), w2.astype(BF16), g2, b2)
    return pl.pallas_call(
        _mix_ffn_ln_kernel,
        grid=(n // TM_FFN,),
        in_specs=[pl.BlockSpec((TM_FFN, d), lambda i: (i, 0)),
                  pl.BlockSpec((TM_FFN, o2.shape[1]), lambda i: (i, 0))] + [_const_spec(a.shape) for a in consts],
        out_specs=pl.BlockSpec((TM_FFN, d), lambda i: (i, 0)),
        out_shape=jax.ShapeDtypeStruct((n, d), F32),
        compiler_params=_params(1),
        name="mix_ffn_ln",
    )(x2, o2, *consts)


def _rope_tables(seq, rot, lane_of_rot0, period):
    half = rot // 2
    inv = ROPE_THETA ** (-jnp.arange(0, rot, 2, dtype=F32) / rot)
    ang = jnp.arange(seq, dtype=F32)[:, None] * inv[None, :]
    cos_h, sin_h = jnp.cos(ang), jnp.sin(ang)
    r = (jnp.arange(LANES) % period) - lane_of_rot0
    in_lo = (r >= 0) & (r < half)
    in_hi = (r >= half) & (r < rot)
    idx = jnp.clip(jnp.where(in_hi, r - half, r), 0, half - 1)
    cos = jnp.where((in_lo | in_hi)[None, :], cos_h[:, idx], 1.0)
    sin_lo = jnp.where(in_lo[None, :], -sin_h[:, idx], 0.0)
    sin_hi = jnp.where(in_hi[None, :], sin_h[:, idx], 0.0)
    return cos, sin_lo, sin_hi


def _dsa_proj_kernel(x_ref, w_ref, cos_ref, slo_ref, shi_ref,
                     q_ref, k_ref, vt_ref, qi_ref, ki_ref, wi_ref):
    xb = x_ref[...].astype(BF16)
    cos, slo, shi = cos_ref[...], slo_ref[...], shi_ref[...]
    half = ROT_DIM // 2
    d = MIX_WIDTH
    di = IDX_HEADS * IDX_DIM

    def roped(col0, width, scale, out_ref):
        y = _dot(xb, w_ref[:, col0:col0 + width])
        for j in range(width // LANES):
            s = _rope_slab(y[:, j * LANES:(j + 1) * LANES], cos, slo, shi, half)
            if scale != 1.0:
                s = s * scale
            out_ref[:, j * LANES:(j + 1) * LANES] = s.astype(out_ref.dtype)

    roped(0, d, HEAD_DIM ** -0.5 * LOG2E, q_ref)
    roped(d, d, 1.0, k_ref)
    _store_values_t(vt_ref, _dot(xb, w_ref[:, 2 * d:3 * d]))
    roped(3 * d, di, IDX_DIM ** -0.5, qi_ref)
    roped(3 * d + di, LANES, 1.0, ki_ref)
    wi_ref[...] = _dot(xb, w_ref[:, 3 * d + di + LANES:]) * (IDX_HEADS ** -0.5)


def _dsa_proj(x2, w, tabs, seq):
    n, dm = x2.shape
    d, di = MIX_WIDTH, IDX_HEADS * IDX_DIM
    tiles_per_seq = seq // TM
    row = lambda width: pl.BlockSpec((TM, width), lambda i: (i, 0))
    tab = pl.BlockSpec((TM, LANES), lambda i: (i % tiles_per_seq, 0))
    return pl.pallas_call(
        _dsa_proj_kernel,
        grid=(n // TM,),
        in_specs=[row(dm), _const_spec(w.shape), tab, tab, tab],
        out_specs=[row(d), row(d), _values_t_spec(tiles_per_seq), row(di), row(LANES), row(LANES)],
        out_shape=[jax.ShapeDtypeStruct((n, d), BF16), jax.ShapeDtypeStruct((n, d), BF16),
                   _values_t_shape(n, seq), jax.ShapeDtypeStruct((n, di), BF16),
                   jax.ShapeDtypeStruct((n, LANES), BF16), jax.ShapeDtypeStruct((n, LANES), F32)],
        compiler_params=_params(1),
        name="dsa_proj",
    )(x2, w, *tabs)


def _prep_dsa_w(w_in):
    d, di = MIX_WIDTH, IDX_HEADS * IDX_DIM
    ki = w_in[:, 3 * d + di:3 * d + di + IDX_DIM]
    wi = w_in[:, 3 * d + di + IDX_DIM:]
    pad = jnp.zeros((w_in.shape[0], LANES - IDX_HEADS), w_in.dtype)
    return jnp.concatenate([w_in[:, :3 * d + di], ki, ki, wi, pad], axis=1).astype(BF16)


def _fox_proj_kernel(x_ref, w_ref, wf_ref, bf_ref, q_ref, k_ref, vt_ref, cum_ref, carry_ref,
                     *, tiles_per_seq):
    i = pl.program_id(0)
    xb = x_ref[...].astype(BF16)
    d = MIX_WIDTH
    q_ref[...] = (_dot(xb, w_ref[:, :d]) * (HEAD_DIM ** -0.5 * LOG2E)).astype(BF16)
    k_ref[...] = _dot(xb, w_ref[:, d:2 * d]).astype(BF16)
    _store_values_t(vt_ref, _dot(xb, w_ref[:, 2 * d:]))

    f = _dot(xb, wf_ref[...]) + bf_ref[...]
    log_f = (jnp.minimum(f, 0.0) - jnp.log1p(jnp.exp(-jnp.abs(f)))) * LOG2E
    tm = log_f.shape[0]
    tri = (lax.broadcasted_iota(I32, (tm, tm), 0) >= lax.broadcasted_iota(I32, (tm, tm), 1))
    tri = jnp.where(tri, 1.0, 0.0).astype(BF16)
    p0 = log_f.astype(BF16)
    r1 = log_f - p0.astype(F32)
    p1 = r1.astype(BF16)
    p2 = (r1 - p1.astype(F32)).astype(BF16)
    local = _dot(tri, p0) + _dot(tri, p1) + _dot(tri, p2)

    @pl.when(i % tiles_per_seq == 0)
    def _():
        carry_ref[...] = jnp.zeros_like(carry_ref)

    cum = local + carry_ref[...]
    cum_ref[...] = cum
    carry_ref[...] = cum[tm - 1:tm, :]


def _fox_proj(x2, w, wf, bf, seq):
    n, dm = x2.shape
    d = MIX_WIDTH
    row = lambda width: pl.BlockSpec((TM, width), lambda i: (i, 0))
    return pl.pallas_call(
        functools.partial(_fox_proj_kernel, tiles_per_seq=seq // TM),
        grid=(n // TM,),
        in_specs=[row(dm), _const_spec(w.shape), _const_spec(wf.shape), _const_spec(bf.shape)],
        out_specs=[row(d), row(d), _values_t_spec(seq // TM), row(LANES)],
        out_shape=[jax.ShapeDtypeStruct((n, d), BF16), jax.ShapeDtypeStruct((n, d), BF16),
                   _values_t_shape(n, seq), jax.ShapeDtypeStruct((n, LANES), F32)],
        scratch_shapes=[pltpu.VMEM((1, LANES), F32)],
        compiler_params=_params(1),
        name="fox_proj",
    )(x2, w, wf, bf)


def _mla_proj_kernel(x_ref, wd_ref, gq_ref, gkv_ref, wuq_ref, wuk_ref, wuv_ref,
                     cos_ref, slo_ref, shi_ref, q_ref, k_ref, vt_ref):
    xb = x_ref[...].astype(BF16)
    cos, slo, shi = cos_ref[...], slo_ref[...], shi_ref[...]
    half = MLA_ROPE // 2
    c = _dot(xb, wd_ref[...])
    cq = _rms_norm(c[:, :Q_LORA], gq_ref[...]).astype(BF16)
    ckv = _rms_norm(c[:, Q_LORA:Q_LORA + KV_LORA], gkv_ref[...]).astype(BF16)
    k_rope = _rope_slab(c[:, Q_LORA + KV_LORA:], cos, slo, shi, half)
    scale = (MLA_NOPE + MLA_ROPE) ** -0.5 * LOG2E
    for h in range(N_HEADS):
        sl = slice(h * LANES, (h + 1) * LANES)
        q = _rope_slab(_dot(cq, wuq_ref[:, sl]), cos, slo, shi, half)
        q_ref[:, sl] = (q * scale).astype(BF16)
        k_ref[:, sl] = (_dot(ckv, wuk_ref[:, sl]) + k_rope).astype(BF16)
    _store_values_t(vt_ref, _dot(ckv, wuv_ref[...]))


def _mla_proj(x2, wd, gq, gkv, wuq, wuk, wuv, tabs, seq):
    n, dm = x2.shape
    tiles_per_seq = seq // TM
    row = lambda width: pl.BlockSpec((TM, width), lambda i: (i, 0))
    tab = pl.BlockSpec((TM, LANES), lambda i: (i % tiles_per_seq, 0))
    wide = N_HEADS * LANES
    return pl.pallas_call(
        _mla_proj_kernel,
        grid=(n // TM,),
        in_specs=[row(dm)] + [_const_spec(a.shape) for a in (wd, gq, gkv, wuq, wuk, wuv)] + [tab] * 3,
        out_specs=[row(wide), row(wide), _values_t_spec(tiles_per_seq)],
        out_shape=[jax.ShapeDtypeStruct((n, wide), BF16), jax.ShapeDtypeStruct((n, wide), BF16),
                   _values_t_shape(n, seq)],
        compiler_params=_params(1),
        name="mla_proj",
    )(x2, wd, gq, gkv, wuq, wuk, wuv, *tabs)


def _prep_mla_w(w_dqkv, w_uq, w_ukv):
    dm = w_dqkv.shape[0]
    zeros = lambda r, c: jnp.zeros((r, c), w_dqkv.dtype)
    k_rope_w = w_dqkv[:, Q_LORA + KV_LORA:]
    wd = jnp.concatenate([w_dqkv[:, :Q_LORA + KV_LORA], zeros(dm, MLA_NOPE), k_rope_w,
                          zeros(dm, LANES - MLA_NOPE - MLA_ROPE)], axis=1)
    uq = w_uq.reshape(Q_LORA, N_HEADS, MLA_NOPE + MLA_ROPE)
    uq = jnp.pad(uq, ((0, 0), (0, 0), (0, LANES - MLA_NOPE - MLA_ROPE))).reshape(Q_LORA, N_HEADS * LANES)
    ukv = w_ukv.reshape(KV_LORA, N_HEADS, MLA_NOPE + MLA_V)
    uk = jnp.pad(ukv[:, :, :MLA_NOPE], ((0, 0), (0, 0), (0, LANES - MLA_NOPE))).reshape(KV_LORA, N_HEADS * LANES)
    uv = ukv[:, :, MLA_NOPE:].reshape(KV_LORA, N_HEADS * MLA_V)
    return wd.astype(BF16), uq.astype(BF16), uk.astype(BF16), uv.astype(BF16)


def _attend_slabs(slabs, n_plain, causal, qpos, tk, one_pass):
    m2 = slabs[0][0].shape[0]
    tq = m2 // 2
    kiota = lax.broadcasted_iota(I32, (tk, m2), 0)

    def step(c, carries, masked, stale):
        scores = [_dot_nt(slab[1](c), slab[0]) for slab in slabs]
        mask = (c * tk + kiota <= qpos) if masked else None
        out = []
        for (_, _, vt_load, bias_fn, qbias), (m, acc), s in zip(slabs, carries, scores):
            if bias_fn is not None:
                s = bias_fn(c) + s
            if mask is not None:
                s = jnp.where(mask, s, NEG)
            smax = jnp.max(s, axis=0, keepdims=True)
            m_new = jnp.maximum(m, smax if qbias is None else smax + qbias)
            alpha = jnp.exp2(m - m_new)
            shift = m if stale else m_new
            p = jnp.exp2(s - (shift if qbias is None else shift - qbias)).astype(BF16)
            if stale:
                acc = (acc + _dot(vt_load(c), p)) * alpha
            else:
                acc = alpha * acc + _dot(vt_load(c), p)
            out.append((m_new, acc))
        return tuple(out)

    def first_shift(c, masked):
        sub_iota = lax.broadcasted_iota(I32, (SHIFT_KEYS, m2), 0)
        mask = (c * tk + sub_iota <= qpos) if masked else None
        shifts = []
        for q2, k_load, _, _, qbias in slabs:
            s = _dot_nt(k_load(c, SHIFT_KEYS), q2)
            if mask is not None:
                s = jnp.where(mask, s, NEG)
            smax = jnp.max(s, axis=0, keepdims=True)
            shifts.append(smax if qbias is None else smax + qbias)
        return shifts

    zero_acc = jnp.zeros((VT_ROWS, m2), F32)
    carries = tuple((jnp.full((1, m2), NEG, F32), zero_acc) for _ in slabs)
    if one_pass and causal:
        if slabs[0][3] is None:
            carries = tuple((m0, zero_acc) for m0 in first_shift(n_plain, True))
            carries = step(n_plain, carries, True, True)
        else:
            carries = step(n_plain, carries, True, False)
        carries = lax.fori_loop(0, n_plain, lambda t, cr: step(n_plain - 1 - t, cr, False, True), carries)
    elif one_pass:
        carries = tuple((m0, zero_acc) for m0 in first_shift(0, False))
        carries = lax.fori_loop(0, n_plain, lambda c, cr: step(c, cr, False, True), carries)
    else:
        carries = lax.fori_loop(0, n_plain, lambda c, cr: step(c, cr, False, False), carries)
        if causal:
            carries = step(n_plain, carries, True, False)
    row = lax.broadcasted_iota(I32, (LANES, tq), 0)
    outs = []
    for _, acc in carries:
        o = acc[:LANES] / acc[LANES:LANES + 1]
        outs.append(jnp.where(row < HEAD_DIM, o[:, :tq], o[:, tq:]))
    return outs


def _attend_to(o_ref, slabs, n_plain, causal, qpos, tk, one_pass):
    def run(one_pass):
        outs = _attend_slabs(slabs, n_plain, causal, qpos, tk, one_pass)
        for j, out in enumerate(outs):
            o_ref[0, :, j * LANES:(j + 1) * LANES] = out.T.astype(o_ref.dtype)
        return outs

    if not one_pass:
        run(False)
        return
    outs = run(True)
    total = functools.reduce(lambda a, b: a + b, [jnp.sum(o, axis=0, keepdims=True) for o in outs])
    overflowed = jnp.max(jnp.where(jnp.isfinite(total), 0.0, 1.0)) > 0.0

    @pl.when(overflowed)
    def _():
        run(False)


def _split_heads(q_slab):
    lane = lax.broadcasted_iota(I32, q_slab.shape, 1)
    zero = jnp.zeros_like(q_slab)
    return jnp.where(lane < HEAD_DIM, q_slab, zero), jnp.where(lane >= HEAD_DIM, q_slab, zero)


def _kv_chunk(ref, c, tk, lanes, rows=None):
    return ref[0, pl.ds(pl.multiple_of(c * tk, tk), tk if rows is None else rows), lanes]


def _per_batch_spec(a):
    nd = a.ndim
    return pl.BlockSpec((1,) + a.shape[1:], lambda bi, i: (bi,) + (0,) * (nd - 1),
                        pipeline_mode=pl.Buffered(1))


def _query_positions(row0, tq):
    pos = row0 + lax.broadcasted_iota(I32, (1, tq), 1)
    return jnp.concatenate([pos, pos], axis=1)


def _fox_attn_kernel(q_ref, k_ref, vt_ref, cq_ref, ck_ref, o_ref, *, tq, tk):
    i = pl.program_id(1)
    row0 = i * tq
    diag = row0 // tk
    qpos = _query_positions(row0, tq)
    cq = cq_ref[0]

    def slab(j):
        lanes = slice(j * LANES, (j + 1) * LANES)
        q2 = jnp.concatenate(_split_heads(q_ref[0, :, lanes]), axis=0)
        cq2 = jnp.concatenate([cq[2 * j:2 * j + 1, :], cq[2 * j + 1:2 * j + 2, :]], axis=1)

        def bias_fn(c):
            ck = -ck_ref[0, pl.ds(pl.multiple_of(c * tk, tk), tk), :]
            return jnp.concatenate([jnp.broadcast_to(ck[:, 2 * j:2 * j + 1], (tk, tq)),
                                    jnp.broadcast_to(ck[:, 2 * j + 1:2 * j + 2], (tk, tq))], axis=1)

        return (q2, lambda c, rows=None: _kv_chunk(k_ref, c, tk, lanes, rows),
                lambda c: vt_ref[0, c, j * VT_ROWS:(j + 1) * VT_ROWS, :], bias_fn, cq2)

    _attend_to(o_ref, [slab(j) for j in range(N_SLABS)], diag, True, qpos, tk, True)


def _fox_attn(q, k, vt, cum_q, cum_k):
    b, s, d = q.shape
    tq, tk = min(TQ, s), vt.shape[3]
    return pl.pallas_call(
        functools.partial(_fox_attn_kernel, tq=tq, tk=tk),
        grid=(b, s // tq),
        in_specs=[pl.BlockSpec((1, tq, d), lambda bi, i: (bi, i, 0)),
                  _per_batch_spec(k), _per_batch_spec(vt),
                  pl.BlockSpec((1, N_HEADS, tq), lambda bi, i: (bi, 0, i)),
                  _per_batch_spec(cum_k)],
        out_specs=pl.BlockSpec((1, tq, d), lambda bi, i: (bi, i, 0)),
        out_shape=jax.ShapeDtypeStruct((b, s, d), BF16),
        compiler_params=_params(2),
        name="fox_attn",
    )(q, k, vt, cum_q, cum_k)


def _mla_attn_kernel(q_ref, k_ref, vt_ref, o_ref, *, tq, tk):
    i = pl.program_id(1)
    row0 = i * tq
    diag = row0 // tk
    qpos = _query_positions(row0, tq)
    zero = jnp.zeros((tq, LANES), BF16)

    def slab(j):
        pair = slice(2 * j * LANES, (2 * j + 2) * LANES)
        qa = q_ref[0, :, 2 * j * LANES:(2 * j + 1) * LANES]
        qb = q_ref[0, :, (2 * j + 1) * LANES:(2 * j + 2) * LANES]
        q2 = jnp.concatenate([jnp.concatenate([qa, zero], axis=1),
                              jnp.concatenate([zero, qb], axis=1)], axis=0)
        lanes = slice(j * LANES, (j + 1) * LANES)
        return (q2, lambda c, rows=None: _kv_chunk(k_ref, c, tk, pair, rows),
                lambda c: vt_ref[0, c, j * VT_ROWS:(j + 1) * VT_ROWS, :], None, None)

    _attend_to(o_ref, [slab(j) for j in range(N_SLABS)], diag, True, qpos, tk, True)


def _mla_attn(q, k, vt):
    b, s, wide = q.shape
    d = MIX_WIDTH
    tq, tk = min(TQ, s), vt.shape[3]
    return pl.pallas_call(
        functools.partial(_mla_attn_kernel, tq=tq, tk=tk),
        grid=(b, s // tq),
        in_specs=[pl.BlockSpec((1, tq, wide), lambda bi, i: (bi, i, 0)),
                  _per_batch_spec(k), _per_batch_spec(vt)],
        out_specs=pl.BlockSpec((1, tq, d), lambda bi, i: (bi, i, 0)),
        out_shape=jax.ShapeDtypeStruct((b, s, d), BF16),
        compiler_params=_params(2),
        name="mla_attn",
    )(q, k, vt)


def _sortable_key(score):
    bits = pltpu.bitcast(score + 0.0, I32)
    return jnp.where(bits < 0, bits ^ 0x7FFFFFFF, bits)


def _count_keys(ref, n_chunks, indicator, acc_rows):
    _, tk, tq = ref.shape

    def body(c, acc):
        ind = indicator(ref[c], c)
        for r in range(tk // acc_rows):
            acc = acc + ind[r * acc_rows:(r + 1) * acc_rows]
        return acc

    acc = lax.fori_loop(0, n_chunks, body, jnp.zeros((acc_rows, tq), ref.dtype))
    return jnp.sum(acc.astype(F32), axis=0, keepdims=True)


def _radix_select16(ref, n_chunks, target, n_all):
    tq = ref.shape[2]
    one, zero = jnp.int16(1), jnp.int16(0)

    def signed(u):
        return lax.shift_right_arithmetic(lax.shift_left(u ^ 0x8000, 16), 16)

    def bit_step(t, carry):
        prefix, n_at, n_above = carry
        cand = prefix | lax.shift_left(jnp.int32(1), 15 - t)
        cand16 = signed(cand).astype(I16)
        cnt = _count_keys(ref, n_chunks, lambda blk, c: jnp.where(blk >= cand16, one, zero), 64)
        accept = cnt >= target
        return jnp.where(accept, cand, prefix), jnp.where(accept, cnt, n_at), jnp.where(accept, n_above, cnt)

    prefix, n_at, n_above = lax.fori_loop(
        0, 16, bit_step, (jnp.zeros((1, tq), I32), n_all, jnp.zeros((1, tq), F32)))
    return signed(prefix), n_at, n_above


def _dsa_attn_kernel(q_ref, k_ref, vt_ref, qi_ref, ki_ref, wit_ref, tri_ref, o_ref,
                     qim_ref, key_ref, hi_ref, lo_ref, bias_ref, *, tq, tk, topk):
    i = pl.program_id(1)
    row0 = i * tq
    n_chunks = row0 // tk + 1
    qpos = row0 + lax.broadcasted_iota(I32, (1, tq), 1)
    kiota = lax.broadcasted_iota(I32, (tk, tq), 0)

    for h in range(IDX_HEADS):
        pair = _split_heads(qi_ref[0, :, (h // 2) * LANES:(h // 2 + 1) * LANES])
        qim_ref[h] = pair[h % 2]
    wit = wit_ref[0]

    def score_chunk(c, causal):
        ki = _kv_chunk(ki_ref, c, tk, slice(None))
        acc = jnp.zeros((tk, tq), F32)
        for h in range(IDX_HEADS):
            acc = acc + wit[h:h + 1, :] * jnp.maximum(_dot_nt(ki, qim_ref[h]), 0.0)
        key = _sortable_key(acc)
        if causal:
            key = jnp.where(c * tk + kiota <= qpos, key, INT_MIN)
        key_ref[c] = key
        hi_ref[c] = lax.shift_right_arithmetic(key, 16).astype(I16)
        lo_ref[c] = ((key & 0xFFFF) - 0x8000).astype(I16)
        return 0

    lax.fori_loop(0, n_chunks - 1, lambda c, _: score_chunk(c, False), 0)
    score_chunk(n_chunks - 1, True)

    k_f = float(topk)
    n_keys = (n_chunks * tk).astype(F32) + jnp.zeros((1, tq), F32)
    thr_hi, n_ge_hi, n_gt_hi = _radix_select16(hi_ref, n_chunks, k_f, n_keys)
    thr_hi16 = thr_hi.astype(I16)
    low_min = jnp.int16(-0x8000)

    def low_plane_chunk(c, _):
        lo_ref[c] = jnp.where(hi_ref[c] == thr_hi16, lo_ref[c], low_min)
        return 0

    lax.fori_loop(0, n_chunks, low_plane_chunk, 0)
    n_bucket = n_ge_hi - n_gt_hi
    thr_lo, n_ge_lo, n_gt_lo = _radix_select16(lo_ref, n_chunks, k_f - n_gt_hi, n_bucket)
    live = thr_hi != -0x8000
    thr = jnp.where(live, thr_hi * 0x10000 + (thr_lo + 0x8000), INT_MIN)
    n_gt = n_gt_hi + n_gt_lo
    n_eq = n_ge_lo - n_gt_lo
    need = k_f - n_gt

    has_ties = jnp.max(jnp.where(live, n_eq - need, 0.0)) > 0.0

    need_ties = jnp.where(live, need, 0.0)

    def bias_chunk_ties(c, before):
        blk = key_ref[c]
        is_thr = blk == thr
        eq = jnp.where(is_thr, 1.0, 0.0)
        rank = _dot(tri_ref[...], eq.astype(BF16)) + before
        tie_ok = jnp.where(rank < need_ties, 0.0, NEG)
        bias_ref[c] = jnp.where(blk > thr, 0.0, jnp.where(is_thr, tie_ok, NEG))
        return before + jnp.sum(eq, axis=0, keepdims=True)

    thr_all = jnp.where(live, thr, INT_MIN + 1)

    def bias_chunk_no_ties(c, _):
        bias_ref[c] = jnp.where(key_ref[c] >= thr_all, 0.0, NEG)
        return 0

    @pl.when(has_ties)
    def _():
        lax.fori_loop(0, n_chunks, bias_chunk_ties, jnp.zeros((1, tq), F32))

    @pl.when(jnp.logical_not(has_ties))
    def _():
        lax.fori_loop(0, n_chunks, bias_chunk_no_ties, 0)

    def bias_fn(c):
        b = bias_ref[c]
        return jnp.concatenate([b, b], axis=1)

    def slab(j):
        lanes = slice(j * LANES, (j + 1) * LANES)
        q2 = jnp.concatenate(_split_heads(q_ref[0, :, lanes]), axis=0)
        return (q2, lambda c, rows=None: _kv_chunk(k_ref, c, tk, lanes, rows),
                lambda c: vt_ref[0, c, j * VT_ROWS:(j + 1) * VT_ROWS, :], bias_fn, None)

    _attend_to(o_ref, [slab(j) for j in range(N_SLABS)], n_chunks, False, None, tk, True)


def _dsa_attn(q, k, vt, qi, ki, wit):
    b, s, d = q.shape
    tq, tk = min(TQ, s), vt.shape[3]
    topk = min(TOPK_MAX, s // 4)
    per_q = lambda width: pl.BlockSpec((1, tq, width), lambda bi, i: (bi, i, 0))
    tri = jnp.tril(jnp.ones((tk, tk), BF16), -1)
    return pl.pallas_call(
        functools.partial(_dsa_attn_kernel, tq=tq, tk=tk, topk=topk),
        grid=(b, s // tq),
        in_specs=[per_q(d), _per_batch_spec(k), _per_batch_spec(vt),
                  per_q(qi.shape[2]), _per_batch_spec(ki),
                  pl.BlockSpec((1, IDX_HEADS, tq), lambda bi, i: (bi, 0, i)),
                  _const_spec(tri.shape)],
        out_specs=per_q(d),
        out_shape=jax.ShapeDtypeStruct((b, s, d), BF16),
        scratch_shapes=[pltpu.VMEM((IDX_HEADS, tq, LANES), BF16),
                        pltpu.VMEM((s // tk, tk, tq), I32),
                        pltpu.VMEM((s // tk, tk, tq), I16),
                        pltpu.VMEM((s // tk, tk, tq), I16),
                        pltpu.VMEM((s // tk, tk, tq), F32)],
        compiler_params=_params(2),
        name="dsa_attn",
    )(q, k, vt, qi, ki, wit, tri)


def _dsa_mixer(x2, b, s, w_in, tabs):
    q, k, vt, qi, ki, wi = _dsa_proj(x2, _prep_dsa_w(w_in), tabs, s)
    r3 = lambda a: a.reshape(b, s, a.shape[1])
    wit = r3(wi)[:, :, :IDX_HEADS].transpose(0, 2, 1)
    o = _dsa_attn(r3(q), r3(k), vt, r3(qi), r3(ki), wit)
    return o.reshape(b * s, MIX_WIDTH)


def _fox_mixer(x2, b, s, w_in, b_f):
    d = MIX_WIDTH
    wf = jnp.pad(w_in[:, 3 * d:], ((0, 0), (0, LANES - N_HEADS))).astype(BF16)
    bf = jnp.pad(b_f, (0, LANES - N_HEADS)).reshape(1, LANES)
    q, k, vt, cum = _fox_proj(x2, w_in[:, :3 * d].astype(BF16), wf, bf, s)
    r3 = lambda a: a.reshape(b, s, a.shape[1])
    cum_q = r3(cum)[:, :, :N_HEADS].transpose(0, 2, 1)
    o = _fox_attn(r3(q), r3(k), vt, cum_q, r3(cum))
    return o.reshape(b * s, d)


def _mla_mixer(x2, b, s, w_dqkv, gq, w_uq, gkv, w_ukv, tabs):
    wd, uq, uk, uv = _prep_mla_w(w_dqkv, w_uq, w_ukv)
    q, k, vt = _mla_proj(x2, wd, gq.reshape(1, -1), gkv.reshape(1, -1), uq, uk, uv, tabs, s)
    r3 = lambda a: a.reshape(b, s, a.shape[1])
    o = _mla_attn(r3(q), r3(k), vt)
    return o.reshape(b * s, MIX_WIDTH)


def kernel(x, ffn1_w13, ffn1_w2, ffn2_w13, ffn2_w2, ln_g, ln_b, w_out, dsa_w_in, fox_w_in, fox_b_f,
           mla_w_dqkv, mla_q_norm_g, mla_w_uq, mla_kv_norm_g, mla_w_ukv):
    b, s, dm = x.shape
    x2 = x.reshape(b * s, dm)
    tabs_p = _rope_tables(s, ROT_DIM, 0, HEAD_DIM)
    tabs_m = _rope_tables(s, MLA_ROPE, MLA_NOPE, LANES)
    for i in range(DEPTH):
        g = lambda r, i=i: ln_g[i, r].reshape(1, dm)
        be = lambda r, i=i: ln_b[i, r].reshape(1, dm)
        x2 = _ffn_ln(x2, ffn1_w13[i], ffn1_w2[i], g(0), be(0))
        kind, j = i % N_MIXERS, i // N_MIXERS
        if kind == 0:
            o = _dsa_mixer(x2, b, s, dsa_w_in[j], tabs_p)
        elif kind == 1:
            o = _fox_mixer(x2, b, s, fox_w_in[j], fox_b_f[j])
        else:
            o = _mla_mixer(x2, b, s, mla_w_dqkv[j], mla_q_norm_g[j], mla_w_uq[j],
                           mla_kv_norm_g[j], mla_w_ukv[j], tabs_m)
        x2 = _mix_ffn_ln(x2, o, w_out[i], g(1), be(1), ffn2_w13[i], ffn2_w2[i], g(2), be(2))
    return x2.reshape(b, s, dm)
```

```python
import functools

import jax
import jax.numpy as jnp
from jax import lax
from jax.experimental import pallas as pl
from jax.experimental.pallas import tpu as pltpu

F32 = jnp.float32
BF16 = jnp.bfloat16
I32 = jnp.int32
I16 = jnp.int16

D_MODEL = 1024
DEPTH = 4
N_MIXERS = 3
HEAD_DIM = 64
N_HEADS = D_MODEL // HEAD_DIM
MIX_WIDTH = N_HEADS * HEAD_DIM
ROT_DIM = HEAD_DIM // 4
ROPE_THETA = 500000.0
IDX_HEADS = 8
IDX_DIM = 64
TOPK_MAX = 256
MLA_NOPE = 64
MLA_ROPE = 32
MLA_V = 64
Q_LORA = 384
KV_LORA = 256
D_FF = 2816
ALPHA = (2.0 * DEPTH) ** 0.25
LN_EPS = 1e-5
RMS_EPS = 1e-6

LANES = 128
N_SLABS = MIX_WIDTH // LANES
VT_ROWS = LANES + 16
LOG2E = 1.4426950408889634
NEG = -1e30
INT_MIN = -2147483648

TM = 512
TM_FFN = 2 * TM
TF = 256
TQ = 256
SHIFT_KEYS = 64
VMEM_LIMIT = 56 * 1024 * 1024


def _params(n_axes, vmem=VMEM_LIMIT):
    return pltpu.CompilerParams(dimension_semantics=("arbitrary",) * n_axes, vmem_limit_bytes=vmem)


def _const_spec(shape):
    nd = len(shape)
    return pl.BlockSpec(shape, lambda *_: (0,) * nd, pipeline_mode=pl.Buffered(1))


def _dot(a, b):
    return jnp.dot(a, b, preferred_element_type=F32)


def _dot_nt(a, b):
    return lax.dot_general(a, b, (((1,), (1,)), ((), ())), preferred_element_type=F32)


def _layer_norm(z, g, b):
    mu = jnp.mean(z, axis=-1, keepdims=True)
    d = z - mu
    var = jnp.mean(d * d, axis=-1, keepdims=True)
    return d * lax.rsqrt(var + LN_EPS) * g + b


def _rms_norm(z, g):
    ms = jnp.mean(z * z, axis=-1, keepdims=True)
    return z * lax.rsqrt(ms + RMS_EPS) * g


def _rope_slab(y, cos, sin_lo, sin_hi, half):
    return y * cos + pltpu.roll(y, half, 1) * sin_hi + pltpu.roll(y, LANES - half, 1) * sin_lo


def _store_values_t(vt_ref, v):
    tm = v.shape[0]
    extra = jnp.where(lax.broadcasted_iota(I32, (VT_ROWS - LANES, tm), 0) == 0, 1.0, 0.0).astype(BF16)
    for j in range(N_SLABS):
        vt_ref[0, 0, j * VT_ROWS:j * VT_ROWS + LANES, :] = v[:, j * LANES:(j + 1) * LANES].T.astype(BF16)
        vt_ref[0, 0, j * VT_ROWS + LANES:(j + 1) * VT_ROWS, :] = extra


def _values_t_spec(tiles_per_seq):
    return pl.BlockSpec((1, 1, N_SLABS * VT_ROWS, TM), lambda i: (i // tiles_per_seq, i % tiles_per_seq, 0, 0))


def _values_t_shape(n, seq):
    return jax.ShapeDtypeStruct((n // seq, seq // TM, N_SLABS * VT_ROWS, TM), BF16)


def _ffn_sublayer(x, w13_ref, w2_ref, g, b):
    xb = x.astype(BF16)
    ff = w2_ref.shape[0]
    acc = jnp.zeros(x.shape, F32)
    for c0 in range(0, ff, TF):
        gate = _dot(xb, w13_ref[:, c0:c0 + TF])
        up = _dot(xb, w13_ref[:, ff + c0:ff + c0 + TF])
        h = gate * jax.nn.sigmoid(gate) * up
        acc = acc + _dot(h.astype(BF16), w2_ref[c0:c0 + TF, :])
    return _layer_norm(ALPHA * x + 0.5 * acc, g, b)


def _row_tiles(ref):
    return [slice(r, r + TM) for r in range(0, ref.shape[0], TM)]


def _ffn_ln_kernel(x_ref, w13_ref, w2_ref, g_ref, b_ref, o_ref):
    for rows in _row_tiles(x_ref):
        o_ref[rows, :] = _ffn_sublayer(x_ref[rows, :], w13_ref, w2_ref, g_ref[...], b_ref[...])


def _ffn_ln(x2, w13, w2, g, b):
    n, d = x2.shape
    return pl.pallas_call(
        _ffn_ln_kernel,
        grid=(n // TM_FFN,),
        in_specs=[pl.BlockSpec((TM_FFN, d), lambda i: (i, 0)),
                  _const_spec(w13.shape), _const_spec(w2.shape),
                  _const_spec(g.shape), _const_spec(b.shape)],
        out_specs=pl.BlockSpec((TM_FFN, d), lambda i: (i, 0)),
        out_shape=jax.ShapeDtypeStruct((n, d), F32),
        compiler_params=_params(1),
        name="ffn_ln",
    )(x2, w13.astype(BF16), w2.astype(BF16), g, b)


def _mix_ffn_ln_kernel(x_ref, o_ref, wo_ref, g1_ref, b1_ref, w13_ref, w2_ref, g2_ref, b2_ref, out_ref):
    for rows in _row_tiles(x_ref):
        y = _dot(o_ref[rows, :], wo_ref[...])
        x1 = _layer_norm(ALPHA * x_ref[rows, :] + y, g1_ref[...], b1_ref[...])
        out_ref[rows, :] = _ffn_sublayer(x1, w13_ref, w2_ref, g2_ref[...], b2_ref[...])


def _mix_ffn_ln(x2, o2, wo, g1, b1, w13, w2, g2, b2):
    n, d = x2.shape
    consts = (wo.astype(BF16), g1, b1, w13.astype(BF16), w2.astype(BF16), g2, b2)
    return pl.pallas_call(
        _mix_ffn_ln_kernel,
        grid=(n // TM_FFN,),
        in_specs=[pl.BlockSpec((TM_FFN, d), lambda i: (i, 0)),
                  pl.BlockSpec((TM_FFN, o2.shape[1]), lambda i: (i, 0))] + [_const_spec(a.shape) for a in consts],
        out_specs=pl.BlockSpec((TM_FFN, d), lambda i: (i, 0)),
        out_shape=jax.ShapeDtypeStruct((n, d), F32),
        compiler_params=_params(1),
        name="mix_ffn_ln",
    )(x2, o2, *consts)


def _rope_tables(seq, rot, lane_of_rot0, period):
    half = rot // 2
    inv = ROPE_THETA ** (-jnp.arange(0, rot, 2, dtype=F32) / rot)
    ang = jnp.arange(seq, dtype=F32)[:, None] * inv[None, :]
    cos_h, sin_h = jnp.cos(ang), jnp.sin(ang)
    r = (jnp.arange(LANES) % period) - lane_of_rot0
    in_lo = (r >= 0) & (r < half)
    in_hi = (r >= half) & (r < rot)
    idx = jnp.clip(jnp.where(in_hi, r - half, r), 0, half - 1)
    cos = jnp.where((in_lo | in_hi)[None, :], cos_h[:, idx], 1.0)
    sin_lo = jnp.where(in_lo[None, :], -sin_h[:, idx], 0.0)
    sin_hi = jnp.where(in_hi[None, :], sin_h[:, idx], 0.0)
    return cos, sin_lo, sin_hi


def _dsa_proj_kernel(x_ref, w_ref, cos_ref, slo_ref, shi_ref,
                     q_ref, k_ref, vt_ref, qi_ref, ki_ref, wi_ref):
    xb = x_ref[...].astype(BF16)
    cos, slo, shi = cos_ref[...], slo_ref[...], shi_ref[...]
    half = ROT_DIM // 2
    d = MIX_WIDTH
    di = IDX_HEADS * IDX_DIM

    def roped(col0, width, scale, out_ref):
        y = _dot(xb, w_ref[:, col0:col0 + width])
        for j in range(width // LANES):
            s = _rope_slab(y[:, j * LANES:(j + 1) * LANES], cos, slo, shi, half)
            if scale != 1.0:
                s = s * scale
            out_ref[:, j * LANES:(j + 1) * LANES] = s.astype(out_ref.dtype)

    roped(0, d, HEAD_DIM ** -0.5 * LOG2E, q_ref)
    roped(d, d, 1.0, k_ref)
    _store_values_t(vt_ref, _dot(xb, w_ref[:, 2 * d:3 * d]))
    roped(3 * d, di, IDX_DIM ** -0.5, qi_ref)
    roped(3 * d + di, LANES, 1.0, ki_ref)
    wi_ref[...] = _dot(xb, w_ref[:, 3 * d + di + LANES:]) * (IDX_HEADS ** -0.5)


def _dsa_proj(x2, w, tabs, seq):
    n, dm = x2.shape
    d, di = MIX_WIDTH, IDX_HEADS * IDX_DIM
    tiles_per_seq = seq // TM
    row = lambda width: pl.BlockSpec((TM, width), lambda i: (i, 0))
    tab = pl.BlockSpec((TM, LANES), lambda i: (i % tiles_per_seq, 0))
    return pl.pallas_call(
        _dsa_proj_kernel,
        grid=(n // TM,),
        in_specs=[row(dm), _const_spec(w.shape), tab, tab, tab],
        out_specs=[row(d), row(d), _values_t_spec(tiles_per_seq), row(di), row(LANES), row(LANES)],
        out_shape=[jax.ShapeDtypeStruct((n, d), BF16), jax.ShapeDtypeStruct((n, d), BF16),
                   _values_t_shape(n, seq), jax.ShapeDtypeStruct((n, di), BF16),
                   jax.ShapeDtypeStruct((n, LANES), BF16), jax.ShapeDtypeStruct((n, LANES), F32)],
        compiler_params=_params(1),
        name="dsa_proj",
    )(x2, w, *tabs)


def _prep_dsa_w(w_in):
    d, di = MIX_WIDTH, IDX_HEADS * IDX_DIM
    ki = w_in[:, 3 * d + di:3 * d + di + IDX_DIM]
    wi = w_in[:, 3 * d + di + IDX_DIM:]
    pad = jnp.zeros((w_in.shape[0], LANES - IDX_HEADS), w_in.dtype)
    return jnp.concatenate([w_in[:, :3 * d + di], ki, ki, wi, pad], axis=1).astype(BF16)


def _fox_proj_kernel(x_ref, w_ref, wf_ref, bf_ref, q_ref, k_ref, vt_ref, cum_ref, carry_ref,
                     *, tiles_per_seq):
    i = pl.program_id(0)
    xb = x_ref[...].astype(BF16)
    d = MIX_WIDTH
    q_ref[...] = (_dot(xb, w_ref[:, :d]) * (HEAD_DIM ** -0.5 * LOG2E)).astype(BF16)
    k_ref[...] = _dot(xb, w_ref[:, d:2 * d]).astype(BF16)
    _store_values_t(vt_ref, _dot(xb, w_ref[:, 2 * d:]))

    f = _dot(xb, wf_ref[...]) + bf_ref[...]
    log_f = (jnp.minimum(f, 0.0) - jnp.log1p(jnp.exp(-jnp.abs(f)))) * LOG2E
    tm = log_f.shape[0]
    tri = (lax.broadcasted_iota(I32, (tm, tm), 0) >= lax.broadcasted_iota(I32, (tm, tm), 1))
    tri = jnp.where(tri, 1.0, 0.0).astype(BF16)
    p0 = log_f.astype(BF16)
    r1 = log_f - p0.astype(F32)
    p1 = r1.astype(BF16)
    p2 = (r1 - p1.astype(F32)).astype(BF16)
    local = _dot(tri, p0) + _dot(tri, p1) + _dot(tri, p2)

    @pl.when(i % tiles_per_seq == 0)
    def _():
        carry_ref[...] = jnp.zeros_like(carry_ref)

    cum = local + carry_ref[...]
    cum_ref[...] = cum
    carry_ref[...] = cum[tm - 1:tm, :]


def _fox_proj(x2, w, wf, bf, seq):
    n, dm = x2.shape
    d = MIX_WIDTH
    row = lambda width: pl.BlockSpec((TM, width), lambda i: (i, 0))
    return pl.pallas_call(
        functools.partial(_fox_proj_kernel, tiles_per_seq=seq // TM),
        grid=(n // TM,),
        in_specs=[row(dm), _const_spec(w.shape), _const_spec(wf.shape), _const_spec(bf.shape)],
        out_specs=[row(d), row(d), _values_t_spec(seq // TM), row(LANES)],
        out_shape=[jax.ShapeDtypeStruct((n, d), BF16), jax.ShapeDtypeStruct((n, d), BF16),
                   _values_t_shape(n, seq), jax.ShapeDtypeStruct((n, LANES), F32)],
        scratch_shapes=[pltpu.VMEM((1, LANES), F32)],
        compiler_params=_params(1),
        name="fox_proj",
    )(x2, w, wf, bf)


def _mla_proj_kernel(x_ref, wd_ref, gq_ref, gkv_ref, wuq_ref, wuk_ref, wuv_ref,
                     cos_ref, slo_ref, shi_ref, q_ref, k_ref, vt_ref):
    xb = x_ref[...].astype(BF16)
    cos, slo, shi = cos_ref[...], slo_ref[...], shi_ref[...]
    half = MLA_ROPE // 2
    c = _dot(xb, wd_ref[...])
    cq = _rms_norm(c[:, :Q_LORA], gq_ref[...]).astype(BF16)
    ckv = _rms_norm(c[:, Q_LORA:Q_LORA + KV_LORA], gkv_ref[...]).astype(BF16)
    k_rope = _rope_slab(c[:, Q_LORA + KV_LORA:], cos, slo, shi, half)
    scale = (MLA_NOPE + MLA_ROPE) ** -0.5 * LOG2E
    for h in range(N_HEADS):
        sl = slice(h * LANES, (h + 1) * LANES)
        q = _rope_slab(_dot(cq, wuq_ref[:, sl]), cos, slo, shi, half)
        q_ref[:, sl] = (q * scale).astype(BF16)
        k_ref[:, sl] = (_dot(ckv, wuk_ref[:, sl]) + k_rope).astype(BF16)
    _store_values_t(vt_ref, _dot(ckv, wuv_ref[...]))


def _mla_proj(x2, wd, gq, gkv, wuq, wuk, wuv, tabs, seq):
    n, dm = x2.shape
    tiles_per_seq = seq // TM
    row = lambda width: pl.BlockSpec((TM, width), lambda i: (i, 0))
    tab = pl.BlockSpec((TM, LANES), lambda i: (i % tiles_per_seq, 0))
    wide = N_HEADS * LANES
    return pl.pallas_call(
        _mla_proj_kernel,
        grid=(n // TM,),
        in_specs=[row(dm)] + [_const_spec(a.shape) for a in (wd, gq, gkv, wuq, wuk, wuv)] + [tab] * 3,
        out_specs=[row(wide), row(wide), _values_t_spec(tiles_per_seq)],
        out_shape=[jax.ShapeDtypeStruct((n, wide), BF16), jax.ShapeDtypeStruct((n, wide), BF16),
                   _values_t_shape(n, seq)],
        compiler_params=_params(1),
        name="mla_proj",
    )(x2, wd, gq, gkv, wuq, wuk, wuv, *tabs)


def _prep_mla_w(w_dqkv, w_uq, w_ukv):
    dm = w_dqkv.shape[0]
    zeros = lambda r, c: jnp.zeros((r, c), w_dqkv.dtype)
    k_rope_w = w_dqkv[:, Q_LORA + KV_LORA:]
    wd = jnp.concatenate([w_dqkv[:, :Q_LORA + KV_LORA], zeros(dm, MLA_NOPE), k_rope_w,
                          zeros(dm, LANES - MLA_NOPE - MLA_ROPE)], axis=1)
    uq = w_uq.reshape(Q_LORA, N_HEADS, MLA_NOPE + MLA_ROPE)
    uq = jnp.pad(uq, ((0, 0), (0, 0), (0, LANES - MLA_NOPE - MLA_ROPE))).reshape(Q_LORA, N_HEADS * LANES)
    ukv = w_ukv.reshape(KV_LORA, N_HEADS, MLA_NOPE + MLA_V)
    uk = jnp.pad(ukv[:, :, :MLA_NOPE], ((0, 0), (0, 0), (0, LANES - MLA_NOPE))).reshape(KV_LORA, N_HEADS * LANES)
    uv = ukv[:, :, MLA_NOPE:].reshape(KV_LORA, N_HEADS * MLA_V)
    return wd.astype(BF16), uq.astype(BF16), uk.astype(BF16), uv.astype(BF16)


def _attend_slabs(slabs, n_plain, causal, qpos, tk, one_pass):
    m2 = slabs[0][0].shape[0]
    tq = m2 // 2
    kiota = lax.broadcasted_iota(I32, (tk, m2), 0)

    def step(c, carries, masked, stale):
        scores = [_dot_nt(slab[1](c), slab[0]) for slab in slabs]
        mask = (c * tk + kiota <= qpos) if masked else None
        out = []
        for (_, _, vt_load, bias_fn, qbias), (m, acc), s in zip(slabs, carries, scores):
            if bias_fn is not None:
                s = bias_fn(c) + s
            if mask is not None:
                s = jnp.where(mask, s, NEG)
            smax = jnp.max(s, axis=0, keepdims=True)
            m_new = jnp.maximum(m, smax if qbias is None else smax + qbias)
            alpha = jnp.exp2(m - m_new)
            shift = m if stale else m_new
            p = jnp.exp2(s - (shift if qbias is None else shift - qbias)).astype(BF16)
            if stale:
                acc = (acc + _dot(vt_load(c), p)) * alpha
            else:
                acc = alpha * acc + _dot(vt_load(c), p)
            out.append((m_new, acc))
        return tuple(out)

    def first_shift(c, masked):
        sub_iota = lax.broadcasted_iota(I32, (SHIFT_KEYS, m2), 0)
        mask = (c * tk + sub_iota <= qpos) if masked else None
        shifts = []
        for q2, k_load, _, _, qbias in slabs:
            s = _dot_nt(k_load(c, SHIFT_KEYS), q2)
            if mask is not None:
                s = jnp.where(mask, s, NEG)
            smax = jnp.max(s, axis=0, keepdims=True)
            shifts.append(smax if qbias is None else smax + qbias)
        return shifts

    zero_acc = jnp.zeros((VT_ROWS, m2), F32)
    carries = tuple((jnp.full((1, m2), NEG, F32), zero_acc) for _ in slabs)
    if one_pass and causal:
        if slabs[0][3] is None:
            carries = tuple((m0, zero_acc) for m0 in first_shift(n_plain, True))
            carries = step(n_plain, carries, True, True)
        else:
            carries = step(n_plain, carries, True, False)
        carries = lax.fori_loop(0, n_plain, lambda t, cr: step(n_plain - 1 - t, cr, False, True), carries)
    elif one_pass:
        carries = tuple((m0, zero_acc) for m0 in first_shift(0, False))
        carries = lax.fori_loop(0, n_plain, lambda c, cr: step(c, cr, False, True), carries)
    else:
        carries = lax.fori_loop(0, n_plain, lambda c, cr: step(c, cr, False, False), carries)
        if causal:
            carries = step(n_plain, carries, True, False)
    row = lax.broadcasted_iota(I32, (LANES, tq), 0)
    outs = []
    for _, acc in carries:
        o = acc[:LANES] / acc[LANES:LANES + 1]
        outs.append(jnp.where(row < HEAD_DIM, o[:, :tq], o[:, tq:]))
    return outs


def _attend_to(o_ref, slabs, n_plain, causal, qpos, tk, one_pass):
    def run(one_pass):
        outs = _attend_slabs(slabs, n_plain, causal, qpos, tk, one_pass)
        for j, out in enumerate(outs):
            o_ref[0, :, j * LANES:(j + 1) * LANES] = out.T.astype(o_ref.dtype)
        return outs

    if not one_pass:
        run(False)
        return
    outs = run(True)
    total = functools.reduce(lambda a, b: a + b, [jnp.sum(o, axis=0, keepdims=True) for o in outs])
    overflowed = jnp.max(jnp.where(jnp.isfinite(total), 0.0, 1.0)) > 0.0

    @pl.when(overflowed)
    def _():
        run(False)


def _split_heads(q_slab):
    lane = lax.broadcasted_iota(I32, q_slab.shape, 1)
    zero = jnp.zeros_like(q_slab)
    return jnp.where(lane < HEAD_DIM, q_slab, zero), jnp.where(lane >= HEAD_DIM, q_slab, zero)


def _kv_chunk(ref, c, tk, lanes, rows=None):
    return ref[0, pl.ds(pl.multiple_of(c * tk, tk), tk if rows is None else rows), lanes]


def _per_batch_spec(a):
    nd = a.ndim
    return pl.BlockSpec((1,) + a.shape[1:], lambda bi, i: (bi,) + (0,) * (nd - 1),
                        pipeline_mode=pl.Buffered(1))


def _query_positions(row0, tq):
    pos = row0 + lax.broadcasted_iota(I32, (1, tq), 1)
    return jnp.concatenate([pos, pos], axis=1)


def _fox_attn_kernel(q_ref, k_ref, vt_ref, cq_ref, ck_ref, o_ref, *, tq, tk):
    i = pl.program_id(1)
    row0 = i * tq
    diag = row0 // tk
    qpos = _query_positions(row0, tq)
    cq = cq_ref[0]

    def slab(j):
        lanes = slice(j * LANES, (j + 1) * LANES)
        q2 = jnp.concatenate(_split_heads(q_ref[0, :, lanes]), axis=0)
        cq2 = jnp.concatenate([cq[2 * j:2 * j + 1, :], cq[2 * j + 1:2 * j + 2, :]], axis=1)

        def bias_fn(c):
            ck = -ck_ref[0, pl.ds(pl.multiple_of(c * tk, tk), tk), :]
            return jnp.concatenate([jnp.broadcast_to(ck[:, 2 * j:2 * j + 1], (tk, tq)),
                                    jnp.broadcast_to(ck[:, 2 * j + 1:2 * j + 2], (tk, tq))], axis=1)

        return (q2, lambda c, rows=None: _kv_chunk(k_ref, c, tk, lanes, rows),
                lambda c: vt_ref[0, c, j * VT_ROWS:(j + 1) * VT_ROWS, :], bias_fn, cq2)

    _attend_to(o_ref, [slab(j) for j in range(N_SLABS)], diag, True, qpos, tk, True)


def _fox_attn(q, k, vt, cum_q, cum_k):
    b, s, d = q.shape
    tq, tk = min(TQ, s), vt.shape[3]
    return pl.pallas_call(
        functools.partial(_fox_attn_kernel, tq=tq, tk=tk),
        grid=(b, s // tq),
        in_specs=[pl.BlockSpec((1, tq, d), lambda bi, i: (bi, i, 0)),
                  _per_batch_spec(k), _per_batch_spec(vt),
                  pl.BlockSpec((1, N_HEADS, tq), lambda bi, i: (bi, 0, i)),
                  _per_batch_spec(cum_k)],
        out_specs=pl.BlockSpec((1, tq, d), lambda bi, i: (bi, i, 0)),
        out_shape=jax.ShapeDtypeStruct((b, s, d), BF16),
        compiler_params=_params(2),
        name="fox_attn",
    )(q, k, vt, cum_q, cum_k)


def _mla_attn_kernel(q_ref, k_ref, vt_ref, o_ref, *, tq, tk):
    i = pl.program_id(1)
    row0 = i * tq
    diag = row0 // tk
    qpos = _query_positions(row0, tq)
    zero = jnp.zeros((tq, LANES), BF16)

    def slab(j):
        pair = slice(2 * j * LANES, (2 * j + 2) * LANES)
        qa = q_ref[0, :, 2 * j * LANES:(2 * j + 1) * LANES]
        qb = q_ref[0, :, (2 * j + 1) * LANES:(2 * j + 2) * LANES]
        q2 = jnp.concatenate([jnp.concatenate([qa, zero], axis=1),
                              jnp.concatenate([zero, qb], axis=1)], axis=0)
        lanes = slice(j * LANES, (j + 1) * LANES)
        return (q2, lambda c, rows=None: _kv_chunk(k_ref, c, tk, pair, rows),
                lambda c: vt_ref[0, c, j * VT_ROWS:(j + 1) * VT_ROWS, :], None, None)

    _attend_to(o_ref, [slab(j) for j in range(N_SLABS)], diag, True, qpos, tk, True)


def _mla_attn(q, k, vt):
    b, s, wide = q.shape
    d = MIX_WIDTH
    tq, tk = min(TQ, s), vt.shape[3]
    return pl.pallas_call(
        functools.partial(_mla_attn_kernel, tq=tq, tk=tk),
        grid=(b, s // tq),
        in_specs=[pl.BlockSpec((1, tq, wide), lambda bi, i: (bi, i, 0)),
                  _per_batch_spec(k), _per_batch_spec(vt)],
        out_specs=pl.BlockSpec((1, tq, d), lambda bi, i: (bi, i, 0)),
        out_shape=jax.ShapeDtypeStruct((b, s, d), BF16),
        compiler_params=_params(2),
        name="mla_attn",
    )(q, k, vt)


def _sortable_key(score):
    bits = pltpu.bitcast(score + 0.0, I32)
    return jnp.where(bits < 0, bits ^ 0x7FFFFFFF, bits)


def _count_keys(ref, n_chunks, indicator, acc_rows):
    _, tk, tq = ref.shape

    def body(c, acc):
        ind = indicator(ref[c], c)
        for r in range(tk // acc_rows):
            acc = acc + ind[r * acc_rows:(r + 1) * acc_rows]
        return acc

    acc = lax.fori_loop(0, n_chunks, body, jnp.zeros((acc_rows, tq), ref.dtype))
    rows, tile_rows = acc_rows, 32 // ref.dtype.itemsize
    while rows > tile_rows:
        rows //= 2
        acc = acc[:rows] + acc[rows:]
    return jnp.sum(acc.astype(F32), axis=0, keepdims=True)


def _radix_select16(ref, n_chunks, target, n_all):
    tq = ref.shape[2]
    one, zero = jnp.int16(1), jnp.int16(0)

    def signed(u):
        return lax.shift_right_arithmetic(lax.shift_left(u ^ 0x8000, 16), 16)

    def bit_step(t, carry):
        prefix, n_at, n_above = carry
        cand = prefix | lax.shift_left(jnp.int32(1), 15 - t)
        cand16 = signed(cand).astype(I16)
        cnt = _count_keys(ref, n_chunks, lambda blk, c: jnp.where(blk >= cand16, one, zero), 64)
        accept = cnt >= target
        return jnp.where(accept, cand, prefix), jnp.where(accept, cnt, n_at), jnp.where(accept, n_above, cnt)

    prefix, n_at, n_above = lax.fori_loop(
        0, 16, bit_step, (jnp.zeros((1, tq), I32), n_all, jnp.zeros((1, tq), F32)))
    return signed(prefix), n_at, n_above


def _dsa_attn_kernel(q_ref, k_ref, vt_ref, qi_ref, ki_ref, wit_ref, tri_ref, o_ref,
                     qim_ref, key_ref, hi_ref, lo_ref, bias_ref, *, tq, tk, topk):
    i = pl.program_id(1)
    row0 = i * tq
    n_chunks = row0 // tk + 1
    qpos = row0 + lax.broadcasted_iota(I32, (1, tq), 1)
    kiota = lax.broadcasted_iota(I32, (tk, tq), 0)

    for h in range(IDX_HEADS):
        pair = _split_heads(qi_ref[0, :, (h // 2) * LANES:(h // 2 + 1) * LANES])
        qim_ref[h] = pair[h % 2]
    wit = wit_ref[0]

    def score_chunk(c, causal):
        ki = _kv_chunk(ki_ref, c, tk, slice(None))
        acc = jnp.zeros((tk, tq), F32)
        for h in range(IDX_HEADS):
            acc = acc + wit[h:h + 1, :] * jnp.maximum(_dot_nt(ki, qim_ref[h]), 0.0)
        key = _sortable_key(acc)
        if causal:
            key = jnp.where(c * tk + kiota <= qpos, key, INT_MIN)
        key_ref[c] = key
        hi_ref[c] = lax.shift_right_arithmetic(key, 16).astype(I16)
        lo_ref[c] = ((key & 0xFFFF) - 0x8000).astype(I16)
        return 0

    lax.fori_loop(0, n_chunks - 1, lambda c, _: score_chunk(c, False), 0)
    score_chunk(n_chunks - 1, True)

    k_f = float(topk)
    n_keys = (n_chunks * tk).astype(F32) + jnp.zeros((1, tq), F32)
    thr_hi, n_ge_hi, n_gt_hi = _radix_select16(hi_ref, n_chunks, k_f, n_keys)
    thr_hi16 = thr_hi.astype(I16)
    low_min = jnp.int16(-0x8000)

    def low_plane_chunk(c, _):
        lo_ref[c] = jnp.where(hi_ref[c] == thr_hi16, lo_ref[c], low_min)
        return 0

    lax.fori_loop(0, n_chunks, low_plane_chunk, 0)
    n_bucket = n_ge_hi - n_gt_hi
    thr_lo, n_ge_lo, n_gt_lo = _radix_select16(lo_ref, n_chunks, k_f - n_gt_hi, n_bucket)
    live = thr_hi != -0x8000
    thr = jnp.where(live, thr_hi * 0x10000 + (thr_lo + 0x8000), INT_MIN)
    n_gt = n_gt_hi + n_gt_lo
    n_eq = n_ge_lo - n_gt_lo
    need = k_f - n_gt

    has_ties = jnp.max(jnp.where(live, n_eq - need, 0.0)) > 0.0

    need_ties = jnp.where(live, need, 0.0)

    def bias_chunk_ties(c, before):
        blk = key_ref[c]
        is_thr = blk == thr
        eq = jnp.where(is_thr, 1.0, 0.0)
        rank = _dot(tri_ref[...], eq.astype(BF16)) + before
        tie_ok = jnp.where(rank < need_ties, 0.0, NEG)
        bias_ref[c] = jnp.where(blk > thr, 0.0, jnp.where(is_thr, tie_ok, NEG))
        return before + jnp.sum(eq, axis=0, keepdims=True)

    thr_all = jnp.where(live, thr, INT_MIN + 1)

    def bias_chunk_no_ties(c, _):
        bias_ref[c] = jnp.where(key_ref[c] >= thr_all, 0.0, NEG)
        return 0

    @pl.when(has_ties)
    def _():
        lax.fori_loop(0, n_chunks, bias_chunk_ties, jnp.zeros((1, tq), F32))

    @pl.when(jnp.logical_not(has_ties))
    def _():
        lax.fori_loop(0, n_chunks, bias_chunk_no_ties, 0)

    def bias_fn(c):
        b = bias_ref[c]
        return jnp.concatenate([b, b], axis=1)

    def slab(j):
        lanes = slice(j * LANES, (j + 1) * LANES)
        q2 = jnp.concatenate(_split_heads(q_ref[0, :, lanes]), axis=0)
        return (q2, lambda c, rows=None: _kv_chunk(k_ref, c, tk, lanes, rows),
                lambda c: vt_ref[0, c, j * VT_ROWS:(j + 1) * VT_ROWS, :], bias_fn, None)

    _attend_to(o_ref, [slab(j) for j in range(N_SLABS)], n_chunks, False, None, tk, True)


def _dsa_attn(q, k, vt, qi, ki, wit):
    b, s, d = q.shape
    tq, tk = min(TQ, s), vt.shape[3]
    topk = min(TOPK_MAX, s // 4)
    per_q = lambda width: pl.BlockSpec((1, tq, width), lambda bi, i: (bi, i, 0))
    tri = jnp.tril(jnp.ones((tk, tk), BF16), -1)
    return pl.pallas_call(
        functools.partial(_dsa_attn_kernel, tq=tq, tk=tk, topk=topk),
        grid=(b, s // tq),
        in_specs=[per_q(d), _per_batch_spec(k), _per_batch_spec(vt),
                  per_q(qi.shape[2]), _per_batch_spec(ki),
                  pl.BlockSpec((1, IDX_HEADS, tq), lambda bi, i: (bi, 0, i)),
                  _const_spec(tri.shape)],
        out_specs=per_q(d),
        out_shape=jax.ShapeDtypeStruct((b, s, d), BF16),
        scratch_shapes=[pltpu.VMEM((IDX_HEADS, tq, LANES), BF16),
                        pltpu.VMEM((s // tk, tk, tq), I32),
                        pltpu.VMEM((s // tk, tk, tq), I16),
                        pltpu.VMEM((s // tk, tk, tq), I16),
                        pltpu.VMEM((s // tk, tk, tq), F32)],
        compiler_params=_params(2),
        name="dsa_attn",
    )(q, k, vt, qi, ki, wit, tri)


def _dsa_mixer(x2, b, s, w_in, tabs):
    q, k, vt, qi, ki, wi = _dsa_proj(x2, _prep_dsa_w(w_in), tabs, s)
    r3 = lambda a: a.reshape(b, s, a.shape[1])
    wit = r3(wi)[:, :, :IDX_HEADS].transpose(0, 2, 1)
    o = _dsa_attn(r3(q), r3(k), vt, r3(qi), r3(ki), wit)
    return o.reshape(b * s, MIX_WIDTH)


def _fox_mixer(x2, b, s, w_in, b_f):
    d = MIX_WIDTH
    wf = jnp.pad(w_in[:, 3 * d:], ((0, 0), (0, LANES - N_HEADS))).astype(BF16)
    bf = jnp.pad(b_f, (0, LANES - N_HEADS)).reshape(1, LANES)
    q, k, vt, cum = _fox_proj(x2, w_in[:, :3 * d].astype(BF16), wf, bf, s)
    r3 = lambda a: a.reshape(b, s, a.shape[1])
    cum_q = r3(cum)[:, :, :N_HEADS].transpose(0, 2, 1)
    o = _fox_attn(r3(q), r3(k), vt, cum_q, r3(cum))
    return o.reshape(b * s, d)


def _mla_mixer(x2, b, s, w_dqkv, gq, w_uq, gkv, w_ukv, tabs):
    wd, uq, uk, uv = _prep_mla_w(w_dqkv, w_uq, w_ukv)
    q, k, vt = _mla_proj(x2, wd, gq.reshape(1, -1), gkv.reshape(1, -1), uq, uk, uv, tabs, s)
    r3 = lambda a: a.reshape(b, s, a.shape[1])
    o = _mla_attn(r3(q), r3(k), vt)
    return o.reshape(b * s, MIX_WIDTH)


def kernel(x, ffn1_w13, ffn1_w2, ffn2_w13, ffn2_w2, ln_g, ln_b, w_out, dsa_w_in, fox_w_in, fox_b_f,
           mla_w_dqkv, mla_q_norm_g, mla_w_uq, mla_kv_norm_g, mla_w_ukv):
    b, s, dm = x.shape
    x2 = x.reshape(b * s, dm)
    tabs_p = _rope_tables(s, ROT_DIM, 0, HEAD_DIM)
    tabs_m = _rope_tables(s, MLA_ROPE, MLA_NOPE, LANES)
    for i in range(DEPTH):
        g = lambda r, i=i: ln_g[i, r].reshape(1, dm)
        be = lambda r, i=i: ln_b[i, r].reshape(1, dm)
        x2 = _ffn_ln(x2, ffn1_w13[i], ffn1_w2[i], g(0), be(0))
        kind, j = i % N_MIXERS, i // N_MIXERS
        if kind == 0:
            o = _dsa_mixer(x2, b, s, dsa_w_in[j], tabs_p)
        elif kind == 1:
            o = _fox_mixer(x2, b, s, fox_w_in[j], fox_b_f[j])
        else:
            o = _mla_mixer(x2, b, s, mla_w_dqkv[j], mla_q_norm_g[j], mla_w_uq[j],
                           mla_kv_norm_g[j], mla_w_ukv[j], tabs_m)
        x2 = _mix_ffn_ln(x2, o, w_out[i], g(1), be(1), ffn2_w13[i], ffn2_w2[i], g(2), be(2))
    return x2.reshape(b, s, dm)
```
